```python
import math
import jax, jax.numpy as jnp
from jax import lax
import numpy as np


D_MODEL = 1024
BATCH = 2
SEQ = 8192
DEPTH = 2

N_MIXERS = 2
N_POOL_LAYERS = (DEPTH + N_MIXERS - 1) // N_MIXERS
N_SSM_LAYERS = DEPTH // N_MIXERS
POOL_WINDOWS = (2, 4, 8, 16)
POOL_GROUPS = len(POOL_WINDOWS)
POOL_GROUP_DIM = D_MODEL // POOL_GROUPS
SSM_GROUP_DIM = 16
SSM_GROUPS = D_MODEL // SSM_GROUP_DIM
SSM_STATE = 64
N_DIRS = 2
D_FF = ((8 * D_MODEL) // 3 + 63) // 64 * 64
N_SUBLAYERS = 3
N_MOD = 3
EPS = 1e-6
DT_MIN = 1e-3
DT_MAX = 1e-1

kernel_name = 'hybrid_pool_s5_macaron_encoder'


def rms_norm(x, g):
    xf = x.astype(jnp.float32)
    y = xf * lax.rsqrt(jnp.mean(xf * xf, axis=-1, keepdims=True) + EPS)
    return (y * g.astype(jnp.float32)).astype(x.dtype)


def swiglu_ffn(h, w_in, w_out):
    gu = h @ w_in
    gate, up = jnp.split(gu, 2, axis=-1)
    return (jax.nn.silu(gate) * up) @ w_out


def centred_window_mean(u, window):
    S = u.shape[1]
    left = window // 2
    right = window - 1 - left
    csum = jnp.cumsum(u.astype(jnp.float32), axis=1)
    csum = jnp.pad(csum, ((0, 0), (1, 0), (0, 0)))
    t = jnp.arange(S, dtype=jnp.int32)
    lo = jnp.maximum(t - left, 0)
    hi = jnp.minimum(t + right, S - 1)
    total = jnp.take(csum, hi + 1, axis=1) - jnp.take(csum, lo, axis=1)
    count = (hi - lo + 1).astype(jnp.float32)
    return (total / count[None, :, None]).astype(u.dtype)


def pool_mixer(h, w_grp, b, scale):
    Bsz, S, D = h.shape
    hg = h.reshape(Bsz, S, POOL_GROUPS, POOL_GROUP_DIM)
    pooled = []
    for gi, window in enumerate(POOL_WINDOWS):
        ug = hg[:, :, gi]
        pooled.append(centred_window_mean(ug, window) - ug)
    p = jnp.stack(pooled, axis=2)
    y = jnp.einsum('bsgc,gcd->bsgd', p, w_grp).reshape(Bsz, S, D) + b
    return y * scale


def zoh_discretise(lam_re, lam_im, log_dt, b_re, b_im):
    dt = jnp.exp(log_dt)[:, None]
    mag = jnp.exp(lam_re * dt)
    ar = mag * jnp.cos(lam_im * dt)
    ai = mag * jnp.sin(lam_im * dt)
    den = lam_re * lam_re + lam_im * lam_im
    fr = ((ar - 1.0) * lam_re + ai * lam_im) / den
    fi = (ai * lam_re - (ar - 1.0) * lam_im) / den
    bb_re = fr[..., None] * b_re - fi[..., None] * b_im
    bb_im = fr[..., None] * b_im + fi[..., None] * b_re
    return ar, ai, bb_re, bb_im


def complex_linear_scan(a_re, a_im, bu_re, bu_im, reverse):
    S = bu_re.shape[1]
    a_re_s = jnp.broadcast_to(a_re, (1, S) + a_re.shape)
    a_im_s = jnp.broadcast_to(a_im, (1, S) + a_im.shape)

    def combine(e1, e2):
        ar1, ai1, br1, bi1 = e1
        ar2, ai2, br2, bi2 = e2
        ar = ar1 * ar2 - ai1 * ai2
        ai = ar1 * ai2 + ai1 * ar2
        br = ar2 * br1 - ai2 * bi1 + br2
        bi = ar2 * bi1 + ai2 * br1 + bi2
        return (ar, ai, br, bi)

    _, _, xr, xi = lax.associative_scan(combine, (a_re_s, a_im_s, bu_re, bu_im), reverse=reverse, axis=1)
    return xr, xi


def s5_mixer(h, lam_re, lam_im, log_dt, b_re, b_im, c_re, c_im, d_skip, w_glu, b_glu):
    Bsz, S, D = h.shape
    hf = h.astype(jnp.float32)
    u = hf.reshape(Bsz, S, SSM_GROUPS, SSM_GROUP_DIM)
    y = d_skip.astype(jnp.float32) * hf
    for direction in range(N_DIRS):
        ar, ai, bb_re, bb_im = zoh_discretise(
            lam_re[direction].astype(jnp.float32), lam_im[direction].astype(jnp.float32),
            log_dt[direction].astype(jnp.float32),
            b_re[direction].astype(jnp.float32), b_im[direction].astype(jnp.float32))
        bu_re = jnp.einsum('bsgh,gph->bsgp', u, bb_re)
        bu_im = jnp.einsum('bsgh,gph->bsgp', u, bb_im)
        xr, xi = complex_linear_scan(ar, ai, bu_re, bu_im, reverse=(direction == 1))
        yd = (jnp.einsum('bsgp,ghp->bsgh', xr, c_re[direction].astype(jnp.float32))
              - jnp.einsum('bsgp,ghp->bsgh', xi, c_im[direction].astype(jnp.float32)))
        y = y + yd.reshape(Bsz, S, D)
    y = jax.nn.gelu(y).astype(h.dtype)
    z = y @ w_glu + b_glu
    return z[..., :D] * jax.nn.sigmoid(z[..., D:])


def setup_inputs(seed: int = 0) -> dict:
    key = jax.random.key(seed)
    ks = jax.random.split(key, 24)
    D, G, P, H = D_MODEL, SSM_GROUPS, SSM_STATE, SSM_GROUP_DIM
    f32 = jnp.float32
    x = jax.random.normal(ks[0], (BATCH, SEQ, D), f32)
    c = jax.random.normal(ks[1], (BATCH, D), f32)
    mod_w = jax.random.normal(ks[2], (DEPTH, D, N_SUBLAYERS * N_MOD * D), f32) * (0.5 * D ** -0.5)
    mod_b = jax.random.normal(ks[3], (DEPTH, N_SUBLAYERS * N_MOD * D), f32) * 0.01
    norm_g = 1.0 + 0.05 * jax.random.normal(ks[4], (DEPTH, N_SUBLAYERS, D), f32)
    ffn_w_in = jax.random.normal(ks[5], (DEPTH, 2, D, 2 * D_FF), f32) * D ** -0.5
    ffn_w_out = jax.random.normal(ks[6], (DEPTH, 2, D_FF, D), f32) * D_FF ** -0.5
    pool_w = jax.random.normal(ks[7], (N_POOL_LAYERS, POOL_GROUPS, POOL_GROUP_DIM, POOL_GROUP_DIM), f32) * POOL_GROUP_DIM ** -0.5
    pool_b = jax.random.normal(ks[8], (N_POOL_LAYERS, D), f32) * 0.01
    pool_scale = 1.0 + 0.1 * jax.random.normal(ks[9], (N_POOL_LAYERS, D), f32)
    n_idx = jnp.arange(P, dtype=f32)
    ssm_lam_re = -0.5 + 0.01 * jax.random.normal(ks[10], (N_SSM_LAYERS, N_DIRS, G, P), f32)
    ssm_lam_im = math.pi * n_idx + 0.01 * jax.random.normal(ks[11], (N_SSM_LAYERS, N_DIRS, G, P), f32)
    ssm_log_dt = jax.random.uniform(ks[12], (N_SSM_LAYERS, N_DIRS, G), f32,
                                    minval=math.log(DT_MIN), maxval=math.log(DT_MAX))
    b_std = (2.0 * H) ** -0.5
    ssm_b_re = jax.random.normal(ks[13], (N_SSM_LAYERS, N_DIRS, G, P, H), f32) * b_std
    ssm_b_im = jax.random.normal(ks[14], (N_SSM_LAYERS, N_DIRS, G, P, H), f32) * b_std
    c_std = P ** -0.5
    ssm_c_re = jax.random.normal(ks[15], (N_SSM_LAYERS, N_DIRS, G, H, P), f32) * c_std
    ssm_c_im = jax.random.normal(ks[16], (N_SSM_LAYERS, N_DIRS, G, H, P), f32) * c_std
    ssm_d = jax.random.normal(ks[17], (N_SSM_LAYERS, D), f32)
    glu_w = jax.random.normal(ks[18], (N_SSM_LAYERS, D, 2 * D), f32) * D ** -0.5
    glu_b = jax.random.normal(ks[19], (N_SSM_LAYERS, 2 * D), f32) * 0.01
    final_g = 1.0 + 0.05 * jax.random.normal(ks[20], (D,), f32)
    return {'x': x, 'c': c, 'mod_w': mod_w, 'mod_b': mod_b, 'norm_g': norm_g,
            'ffn_w_in': ffn_w_in, 'ffn_w_out': ffn_w_out,
            'pool_w': pool_w, 'pool_b': pool_b, 'pool_scale': pool_scale,
            'ssm_lam_re': ssm_lam_re, 'ssm_lam_im': ssm_lam_im, 'ssm_log_dt': ssm_log_dt,
            'ssm_b_re': ssm_b_re, 'ssm_b_im': ssm_b_im, 'ssm_c_re': ssm_c_re, 'ssm_c_im': ssm_c_im,
            'ssm_d': ssm_d, 'glu_w': glu_w, 'glu_b': glu_b, 'final_g': final_g}


def reference(x, c, mod_w, mod_b, norm_g, ffn_w_in, ffn_w_out, pool_w, pool_b, pool_scale,
              ssm_lam_re, ssm_lam_im, ssm_log_dt, ssm_b_re, ssm_b_im, ssm_c_re, ssm_c_im,
              ssm_d, glu_w, glu_b, final_g):
    Bsz, S, D = x.shape
    cond = jax.nn.silu(c)
    for i in range(DEPTH):
        mod = (cond @ mod_w[i] + mod_b[i]).reshape(Bsz, N_SUBLAYERS, N_MOD, D)
        shift = mod[:, :, 0][:, :, None, :]
        scale = mod[:, :, 1][:, :, None, :]
        gate = mod[:, :, 2][:, :, None, :]

        h = rms_norm(x, norm_g[i, 0]) * (1.0 + scale[:, 0]) + shift[:, 0]
        x = x + 0.5 * gate[:, 0] * swiglu_ffn(h, ffn_w_in[i, 0], ffn_w_out[i, 0])

        h = rms_norm(x, norm_g[i, 1]) * (1.0 + scale[:, 1]) + shift[:, 1]
        j = i // N_MIXERS
        if i % N_MIXERS == 0:
            m = pool_mixer(h, pool_w[j], pool_b[j], pool_scale[j])
        else:
            m = s5_mixer(h, ssm_lam_re[j], ssm_lam_im[j], ssm_log_dt[j], ssm_b_re[j], ssm_b_im[j],
                         ssm_c_re[j], ssm_c_im[j], ssm_d[j], glu_w[j], glu_b[j])
        x = x + gate[:, 1] * m

        h = rms_norm(x, norm_g[i, 2]) * (1.0 + scale[:, 2]) + shift[:, 2]
        x = x + 0.5 * gate[:, 2] * swiglu_ffn(h, ffn_w_in[i, 1], ffn_w_out[i, 1])
    return rms_norm(x, final_g)
```

```python
import functools
import math

import jax
import jax.numpy as jnp
from jax import lax
from jax.experimental import pallas as pl
from jax.experimental.pallas import tpu as pltpu

F32 = jnp.float32
BF16 = jnp.bfloat16

EPS = 1e-6
N_SUB = 3
N_MOD = 3
POOL_WINDOWS = (2, 4, 8, 16)
POOL_HALO = 8
SSM_H = 16
SSM_P = 64
CHUNK = 32
FF_CHUNK = 256
LANES = 128
VMEM_LIMIT = 56 * 1024 * 1024


def _sigmoid(v):
    return 1.0 / (1.0 + jnp.exp(-v))


def _split3(a):
    hi = a.astype(BF16)
    lo = (a - hi.astype(F32)).astype(BF16)
    return hi, lo


def _dot(a, b):
    return jnp.dot(a, b, preferred_element_type=F32)


def _dot3(a, b):
    ah, al = _split3(a)
    bh, bl = _split3(b)
    return _dot(ah, bh) + (_dot(ah, bl) + _dot(al, bh))


def _mod_kernel(c_ref, w_ref, b_ref, o_ref):
    c = c_ref[...]
    cond = c * _sigmoid(c)
    o_ref[0] = _dot3(cond, w_ref[0]) + b_ref[0]


def _modulation(c, mod_w, mod_b):
    depth, d, n = mod_w.shape
    bsz = c.shape[0]
    rows = 8
    tn = 1024
    assert n % tn == 0 and bsz <= rows
    c_pad = jnp.zeros((rows, d), F32).at[:bsz].set(c.astype(F32))
    out = pl.pallas_call(
        _mod_kernel,
        grid=(depth, n // tn),
        in_specs=[
            pl.BlockSpec((rows, d), lambda i, j: (0, 0)),
            pl.BlockSpec((1, d, tn), lambda i, j: (i, 0, j)),
            pl.BlockSpec((1, 1, tn), lambda i, j: (i, 0, j)),
        ],
        out_specs=pl.BlockSpec((1, rows, tn), lambda i, j: (i, 0, j)),
        out_shape=jax.ShapeDtypeStruct((depth, rows, n), F32),
        compiler_params=pltpu.CompilerParams(
            dimension_semantics=("arbitrary", "arbitrary"), vmem_limit_bytes=VMEM_LIMIT),
        name="adaln_modulation",
    )(c_pad, mod_w.astype(F32), mod_b.astype(F32).reshape(depth, 1, n))
    return out[:, :bsz].reshape(depth, bsz, N_SUB * N_MOD, d)


def _rms(x, g):
    return x * lax.rsqrt(jnp.mean(x * x, axis=-1, keepdims=True) + EPS) * g


def _modnorm(x, g_ref, mod, sub):
    r = N_MOD * sub
    shift, scale = mod[r:r + 1], mod[r + 1:r + 2]
    return _rms(x, g_ref[sub:sub + 1, :]) * (1.0 + scale) + shift


def _swiglu_step(x, g_ref, mod, sub, win_ref, wout_ref, act_ref):
    h = _modnorm(x, g_ref, mod, sub).astype(BF16)
    n_chunks = act_ref.shape[1] // FF_CHUNK
    for k in range(n_chunks):
        gu = _dot(h, win_ref[:, 2 * FF_CHUNK * k:2 * FF_CHUNK * (k + 1)])
        gate, up = gu[:, :FF_CHUNK], gu[:, FF_CHUNK:]
        act_ref[:, FF_CHUNK * k:FF_CHUNK * (k + 1)] = (gate * _sigmoid(gate) * up).astype(BF16)
    out = _dot(act_ref[...], wout_ref[...])
    r = N_MOD * sub + 2
    return x + (0.5 * mod[r:r + 1]) * out


def _gelu_tanh(v):
    return 0.5 * v * (1.0 + jnp.tanh(math.sqrt(2.0 / math.pi) * (v + 0.044715 * (v * v * v))))


def _pool_update(x, xp_ref, xn_ref, g_ref, mod, pw_ref, pb_ref, ps_ref, hs_ref, seq_len):
    tm, d = x.shape
    cg = d // len(POOL_WINDOWS)
    tiles_per_seq = seq_len // tm
    j = pl.program_id(0) % tiles_per_seq
    h = _modnorm(x, g_ref, mod, 1)
    hp = _modnorm(xp_ref[...], g_ref, mod, 1)
    hn = _modnorm(xn_ref[...], g_ref, mod, 1)
    hs_ref[0:POOL_HALO, :] = jnp.where(j > 0, hp, 0.0)
    hs_ref[POOL_HALO:POOL_HALO + tm, :] = h
    hs_ref[POOL_HALO + tm:, :] = jnp.where(j < tiles_per_seq - 1, hn, 0.0)
    t = j * tm + lax.broadcasted_iota(jnp.int32, (tm, 1), 0)
    ys = []
    for gi, window in enumerate(POOL_WINDOWS):
        left = window // 2
        right = window - 1 - left
        cols = slice(gi * cg, (gi + 1) * cg)
        total = hs_ref[POOL_HALO - left:POOL_HALO - left + tm, cols]
        for s in range(-left + 1, right + 1):
            total = total + hs_ref[POOL_HALO + s:POOL_HALO + s + tm, cols]
        lo = jnp.maximum(t - left, 0)
        hi = jnp.minimum(t + right, seq_len - 1)
        count = (hi - lo + 1).astype(F32)
        p = total / count - h[:, cols]
        ys.append(_dot(p.astype(BF16), pw_ref[gi]))
    y = (jnp.concatenate(ys, axis=1) + pb_ref[...]) * ps_ref[...]
    return x + mod[N_MOD + 2:N_MOD + 3] * y


def _s5_tail_update(x, ys, g_ref, mod, dskip_ref, gw_ref, gb_ref):
    d = x.shape[1]
    h = _modnorm(x, g_ref, mod, 1)
    y = _gelu_tanh(dskip_ref[...] * h + ys)
    z = _dot(y.astype(BF16), gw_ref[...]) + gb_ref[...]
    m = z[:, :d] * _sigmoid(z[:, d:])
    return x + mod[N_MOD + 2:N_MOD + 3] * m


def _layer_kernel(mixer, emit_h, final_norm, seq_len, *refs):
    refs = list(refs)
    x_ref, mod_ref, g_ref = refs[:3]
    del refs[:3]
    x = x_ref[...]
    mod = mod_ref[0]
    if mixer == "pool":
        xp_ref, xn_ref, pw_ref, pb_ref, ps_ref = refs[:5]
        del refs[:5]
    elif mixer == "s5":
        ys_ref, dskip_ref, gw_ref, gb_ref = refs[:4]
        del refs[:4]
    win_ref, wout_ref = refs[:2]
    del refs[:2]
    if final_norm:
        fg_ref = refs.pop(0)
    o_ref = refs.pop(0)
    if emit_h:
        h_ref = refs.pop(0)
    act_ref = refs.pop(0)
    if mixer == "pool":
        hs_ref = refs.pop(0)
        x = _pool_update(x, xp_ref, xn_ref, g_ref, mod, pw_ref, pb_ref, ps_ref, hs_ref, seq_len)
    elif mixer == "s5":
        x = _s5_tail_update(x, ys_ref[...], g_ref, mod, dskip_ref, gw_ref, gb_ref)
    sub = 0 if mixer is None else 2
    x = _swiglu_step(x, g_ref, mod, sub, win_ref, wout_ref, act_ref)
    if final_norm:
        x = _rms(x, fg_ref[...])
    o_ref[...] = x
    if emit_h:
        h_ref[...] = _modnorm(x, g_ref, mod, 1).astype(BF16)


def _const_spec(shape):
    nd = len(shape)
    return pl.BlockSpec(shape, lambda i: (0,) * nd, pipeline_mode=pl.Buffered(1))


def _prep_ffn_weights(w_in, w_out):
    d, two_ff = w_in.shape
    d_ff = two_ff // 2
    n_chunks = -(-d_ff // FF_CHUNK)
    pad = n_chunks * FF_CHUNK - d_ff
    gate = jnp.pad(w_in[:, :d_ff].astype(BF16), ((0, 0), (0, pad))).reshape(d, n_chunks, FF_CHUNK)
    up = jnp.pad(w_in[:, d_ff:].astype(BF16), ((0, 0), (0, pad))).reshape(d, n_chunks, FF_CHUNK)
    win = jnp.concatenate([gate, up], axis=2).reshape(d, 2 * n_chunks * FF_CHUNK)
    wout = jnp.pad(w_out.astype(BF16), ((0, pad), (0, 0)))
    return win, wout


def _layer_call(x, mod, norm_g, w_in, w_out, seq_len, tm, *, mixer=None, mixer_args=(),
                emit_h=False, final_g=None):
    n_tok, d = x.shape
    assert seq_len % tm == 0 and tm % POOL_HALO == 0
    tiles_per_seq = seq_len // tm
    win, wout = _prep_ffn_weights(w_in, w_out)
    ffp = wout.shape[0]
    row_spec = pl.BlockSpec((tm, d), lambda i: (i, 0))
    args = [x, mod, norm_g.astype(F32)]
    specs = [row_spec,
             pl.BlockSpec((1, N_SUB * N_MOD, d), lambda i: (i // tiles_per_seq, 0, 0)),
             _const_spec((N_SUB, d))]
    scratch = [pltpu.VMEM((tm, ffp), BF16)]
    if mixer == "pool":
        pool_w, pool_b, pool_scale = mixer_args
        hb = tm // POOL_HALO
        last = n_tok // POOL_HALO - 1
        args += [x, x, pool_w.astype(BF16), pool_b.astype(F32).reshape(1, d),
                 pool_scale.astype(F32).reshape(1, d)]
        specs += [pl.BlockSpec((POOL_HALO, d), lambda i: (jnp.maximum(i * hb - 1, 0), 0)),
                  pl.BlockSpec((POOL_HALO, d), lambda i: (jnp.minimum((i + 1) * hb, last), 0)),
                  _const_spec(pool_w.shape), _const_spec((1, d)), _const_spec((1, d))]
        scratch.append(pltpu.VMEM((tm + 2 * POOL_HALO, d), F32))
    elif mixer == "s5":
        ys, d_skip, glu_w, glu_b = mixer_args
        args += [ys, d_skip.astype(F32).reshape(1, d), glu_w.astype(BF16),
                 glu_b.astype(F32).reshape(1, 2 * d)]
        specs += [row_spec, _const_spec((1, d)), _const_spec((d, 2 * d)), _const_spec((1, 2 * d))]
    args += [win, wout]
    specs += [_const_spec(win.shape), _const_spec(wout.shape)]
    if final_g is not None:
        args.append(final_g.astype(F32).reshape(1, d))
        specs.append(_const_spec((1, d)))
    out_shape = [jax.ShapeDtypeStruct((n_tok, d), F32)]
    out_specs = [row_spec]
    if emit_h:
        out_shape.append(jax.ShapeDtypeStruct((n_tok, d), BF16))
        out_specs.append(row_spec)
    outs = pl.pallas_call(
        functools.partial(_layer_kernel, mixer, emit_h, final_g is not None, seq_len),
        grid=(n_tok // tm,),
        in_specs=specs,
        out_specs=out_specs,
        out_shape=out_shape,
        scratch_shapes=scratch,
        compiler_params=pltpu.CompilerParams(
            dimension_semantics=("arbitrary",), vmem_limit_bytes=VMEM_LIMIT),
        name="layer_" + (mixer or "ffn"),
    )(*args)
    return outs if emit_h else outs[0]


def _s5_prep_kernel(n_levels, lr_ref, li_ref, ldt_ref, bre_ref, bim_ref, cre_ref, cim_ref,
                    t_ref, bz_ref, cx_ref, dec_ref):
    gb = lr_ref.shape[1]
    w = t_ref.shape[1]
    chunk = w // SSM_H
    lane = lax.broadcasted_iota(jnp.int32, (1, LANES), 1)
    re_half = lane < SSM_P
    lrow = (lax.broadcasted_iota(jnp.int32, (w, 1), 0) // SSM_H).astype(F32)
    wlane = lax.broadcasted_iota(jnp.int32, (1, w), 1)

    def tile_rows(v):
        return jnp.broadcast_to(v[None], (chunk, SSM_H, LANES)).reshape(w, LANES)

    for g in range(gb):
        rs, zs, cs = [], [], []
        for direction in range(2):
            lr, li = lr_ref[direction, g], li_ref[direction, g]
            dt = jnp.exp(ldt_ref[direction, g])
            mag1 = jnp.exp(lr * dt)
            ar, ai = mag1 * jnp.cos(li * dt), mag1 * jnp.sin(li * dt)
            den = lr * lr + li * li
            fr = ((ar - 1.0) * lr + ai * li) / den
            fi = (ai * lr - (ar - 1.0) * li) / den
            bre, bim = bre_ref[direction, g], bim_ref[direction, g]
            bbr, bbi = fr * bre - fi * bim, fr * bim + fi * bre
            cre, cim = tile_rows(cre_ref[direction, g]), tile_rows(cim_ref[direction, g])
            bbr_t, bbi_t = tile_rows(bbr), tile_rows(bbi)

            def powers(e):
                mag = jnp.exp((lr * dt) * e)
                ang = (li * dt) * e
                return mag * jnp.cos(ang), mag * jnp.sin(ang)

            def c_times(e):
                er, ei = powers(e)
                return jnp.where(re_half, cre * er - cim * ei, -(cre * ei + cim * er))

            def b_times(e):
                er, ei = powers(e)
                return jnp.where(re_half, bbr_t * er - bbi_t * ei, bbr_t * ei + bbi_t * er)

            fwd = direction == 0
            zs.append(b_times(chunk - 1.0 - lrow if fwd else lrow))
            cs.append(c_times(lrow + 1.0 if fwd else chunk - lrow))
            gmat = c_times(lrow if fwd else chunk - 1.0 - lrow).T
            bbt = jnp.where(re_half, bbr, bbi)
            rs.append(_dot3(bbt, gmat))
            for k in range(n_levels):
                er, ei = powers(float(chunk * (1 << k)))
                row = 2 * (direction * n_levels + k)
                dec_ref[g, row:row + 1, :] = er[0:1]
                dec_ref[g, row + 1:row + 2, :] = jnp.where(re_half, -ei[0:1], ei[0:1])
        for i in range(chunk):
            s = SSM_H * i
            fwd_rows = jnp.where(wlane >= s, pltpu.roll(rs[0], s, 1) if s else rs[0], 0.0)
            e = (s + SSM_H) % w
            bwd_rows = jnp.where(wlane < s + SSM_H, pltpu.roll(rs[1], e, 1) if e else rs[1], 0.0)
            t_ref[g, s:s + SSM_H, :] = (fwd_rows + bwd_rows).astype(BF16)
        bz_ref[g] = jnp.concatenate(zs, axis=1).astype(BF16)
        cx_ref[g] = jnp.concatenate([cs[0].T, cs[1].T], axis=0).astype(BF16)


def _s5_prep(lam_re, lam_im, log_dt, b_re, b_im, c_re, c_im, n_levels):
    n_dir, n_grp, n_state = lam_re.shape
    assert n_dir == 2 and n_state == SSM_P and b_re.shape[-1] == SSM_H
    w = CHUNK * SSM_H
    gb = 8
    assert n_grp % gb == 0

    def twice(v):
        return jnp.concatenate([v, v], axis=-1).astype(F32)

    lr = twice(lam_re)[:, :, None, :]
    li = twice(lam_im)[:, :, None, :]
    ldt = jnp.broadcast_to(log_dt.astype(F32)[:, :, None, None], (n_dir, n_grp, 1, LANES))
    bre, bim = twice(jnp.swapaxes(b_re, 2, 3)), twice(jnp.swapaxes(b_im, 2, 3))
    cre, cim = twice(c_re), twice(c_im)
    n_dec = 2 * 2 * n_levels
    row = lambda r: pl.BlockSpec((n_dir, gb, r, LANES), lambda i: (0, i, 0, 0))
    return pl.pallas_call(
        functools.partial(_s5_prep_kernel, n_levels),
        grid=(n_grp // gb,),
        in_specs=[row(1), row(1), row(1), row(SSM_H), row(SSM_H), row(SSM_H), row(SSM_H)],
        out_specs=[pl.BlockSpec((gb, w, w), lambda i: (i, 0, 0)),
                   pl.BlockSpec((gb, w, 2 * LANES), lambda i: (i, 0, 0)),
                   pl.BlockSpec((gb, 2 * LANES, w), lambda i: (i, 0, 0)),
                   pl.BlockSpec((gb, n_dec, LANES), lambda i: (i, 0, 0))],
        out_shape=[jax.ShapeDtypeStruct((n_grp, w, w), BF16),
                   jax.ShapeDtypeStruct((n_grp, w, 2 * LANES), BF16),
                   jax.ShapeDtypeStruct((n_grp, 2 * LANES, w), BF16),
                   jax.ShapeDtypeStruct((n_grp, n_dec, LANES), F32)],
        compiler_params=pltpu.CompilerParams(
            dimension_semantics=("arbitrary",), vmem_limit_bytes=VMEM_LIMIT),
        name="s5_prep",
    )(lr, li, ldt, bre, bim, cre, cim)


def _chunk_scan(z, dec_ref, g, base, n_levels, rows_per_seq, reverse):
    n_rows = z.shape[0]
    rc = lax.broadcasted_iota(jnp.int32, (n_rows, 1), 0) % rows_per_seq

    def shifted(v, dist):
        if reverse:
            return jnp.where(rc < rows_per_seq - dist, pltpu.roll(v, n_rows - dist, 0), 0.0)
        return jnp.where(rc >= dist, pltpu.roll(v, dist, 0), 0.0)

    s = shifted(z, 1)
    for k in range(n_levels):
        sh = shifted(s, 1 << k)
        row = base + 2 * k
        s = s + (sh * dec_ref[g, row:row + 1, :] + pltpu.roll(sh, SSM_P, 1) * dec_ref[g, row + 1:row + 2, :])
    return s


def _s5_core_kernel(n_levels, rows_per_seq, u_ref, t_ref, bz_ref, cx_ref, dec_ref, y_ref):
    for g in range(u_ref.shape[0]):
        u = u_ref[g]
        z = _dot(u, bz_ref[g])
        xf = _chunk_scan(z[:, :LANES], dec_ref, g, 0, n_levels, rows_per_seq, False)
        xb = _chunk_scan(z[:, LANES:], dec_ref, g, 2 * n_levels, n_levels, rows_per_seq, True)
        y = _dot(u, t_ref[g])
        y = y + _dot(xf.astype(BF16), cx_ref[g, :LANES, :])
        y = y + _dot(xb.astype(BF16), cx_ref[g, LANES:, :])
        y_ref[g] = y


def _s5_mix(h, seq_len, lam_re, lam_im, log_dt, b_re, b_im, c_re, c_im):
    n_tok, d = h.shape
    n_grp = d // SSM_H
    w = CHUNK * SSM_H
    assert seq_len % CHUNK == 0
    rows_per_seq = seq_len // CHUNK
    n_rows = n_tok // CHUNK
    n_levels = max(1, (rows_per_seq - 1).bit_length())
    t_mat, bz, cx, dec = _s5_prep(lam_re, lam_im, log_dt, b_re, b_im, c_re, c_im, n_levels)
    u = h.reshape(n_rows, CHUNK, n_grp, SSM_H).transpose(2, 0, 1, 3).reshape(n_grp, n_rows, w)
    gp = 4
    assert n_grp % gp == 0
    y = pl.pallas_call(
        functools.partial(_s5_core_kernel, n_levels, rows_per_seq),
        grid=(n_grp // gp,),
        in_specs=[pl.BlockSpec((gp, n_rows, w), lambda i: (i, 0, 0)),
                  pl.BlockSpec((gp, w, w), lambda i: (i, 0, 0)),
                  pl.BlockSpec((gp, w, 2 * LANES), lambda i: (i, 0, 0)),
                  pl.BlockSpec((gp, 2 * LANES, w), lambda i: (i, 0, 0)),
                  pl.BlockSpec((gp, dec.shape[1], LANES), lambda i: (i, 0, 0))],
        out_specs=pl.BlockSpec((gp, n_rows, w), lambda i: (i, 0, 0)),
        out_shape=jax.ShapeDtypeStruct((n_grp, n_rows, w), F32),
        compiler_params=pltpu.CompilerParams(
            dimension_semantics=("arbitrary",), vmem_limit_bytes=VMEM_LIMIT),
        name="s5_core",
    )(u, t_mat, bz, cx, dec)
    return y.reshape(n_grp, n_rows, CHUNK, SSM_H).transpose(1, 2, 0, 3).reshape(n_tok, d)


def kernel(x, c, mod_w, mod_b, norm_g, ffn_w_in, ffn_w_out, pool_w, pool_b, pool_scale,
           ssm_lam_re, ssm_lam_im, ssm_log_dt, ssm_b_re, ssm_b_im, ssm_c_re, ssm_c_im,
           ssm_d, glu_w, glu_b, final_g):
    bsz, seq_len, d = x.shape
    depth = mod_w.shape[0]
    n_mixers = 2
    tm = min(512, seq_len)
    mods = _modulation(c, mod_w, mod_b)
    xf = x.astype(F32).reshape(bsz * seq_len, d)
    for i in range(depth):
        j = i // n_mixers
        is_s5 = i % n_mixers == 1
        final = final_g if i == depth - 1 else None
        first = _layer_call(xf, mods[i], norm_g[i], ffn_w_in[i, 0], ffn_w_out[i, 0], seq_len, tm,
                            emit_h=is_s5)
        if is_s5:
            xf, h = first
            ys = _s5_mix(h, seq_len, ssm_lam_re[j], ssm_lam_im[j], ssm_log_dt[j], ssm_b_re[j],
                         ssm_b_im[j], ssm_c_re[j], ssm_c_im[j])
            xf = _layer_call(xf, mods[i], norm_g[i], ffn_w_in[i, 1], ffn_w_out[i, 1], seq_len, tm,
                             mixer="s5", mixer_args=(ys, ssm_d[j], glu_w[j], glu_b[j]),
                             final_g=final)
        else:
            xf = _layer_call(first, mods[i], norm_g[i], ffn_w_in[i, 1], ffn_w_out[i, 1], seq_len, tm,
                             mixer="pool", mixer_args=(pool_w[j], pool_b[j], pool_scale[j]),
                             final_g=final)
    return xf.reshape(bsz, seq_len, d).astype(x.dtype)
```

```python
import functools
import math

import numpy as np

import jax
import jax.numpy as jnp
from jax import lax
from jax.experimental import pallas as pl
from jax.experimental.pallas import tpu as pltpu

F32 = jnp.float32
BF16 = jnp.bfloat16

EPS = 1e-6
N_SUB = 3
N_MOD = 3
POOL_WINDOWS = (2, 4, 8, 16)
POOL_HALO = 8
SSM_H = 16
SSM_P = 64
CHUNK = 32
CHUNK_PITCH = CHUNK + 4
FF_CHUNK = 256
LANES = 128
SUBLANES = 8
GROUPS_PER_SLAB = LANES // SSM_H
VMEM_LIMIT = 56 * 1024 * 1024


def _sigmoid(v):
    return 1.0 / (1.0 + jnp.exp(-v))


def _split3(a):
    hi = a.astype(BF16)
    lo = (a - hi.astype(F32)).astype(BF16)
    return hi, lo


def _dot(a, b):
    return jnp.dot(a, b, preferred_element_type=F32)


def _dot3(a, b):
    ah, al = _split3(a)
    bh, bl = _split3(b)
    return _dot(ah, bh) + (_dot(ah, bl) + _dot(al, bh))


def _mod_kernel(c_ref, w_ref, b_ref, o_ref):
    c = c_ref[...]
    cond = c * _sigmoid(c)
    o_ref[0] = _dot3(cond, w_ref[0]) + b_ref[0]


def _modulation(c, mod_w, mod_b):
    depth, d, n = mod_w.shape
    bsz = c.shape[0]
    rows = SUBLANES
    tn = 1024
    assert n % tn == 0 and bsz <= rows
    c_pad = jnp.zeros((rows, d), F32).at[:bsz].set(c.astype(F32))
    out = pl.pallas_call(
        _mod_kernel,
        grid=(depth, n // tn),
        in_specs=[
            pl.BlockSpec((rows, d), lambda i, j: (0, 0)),
            pl.BlockSpec((1, d, tn), lambda i, j: (i, 0, j)),
            pl.BlockSpec((1, 1, tn), lambda i, j: (i, 0, j)),
        ],
        out_specs=pl.BlockSpec((1, rows, tn), lambda i, j: (i, 0, j)),
        out_shape=jax.ShapeDtypeStruct((depth, rows, n), F32),
        compiler_params=pltpu.CompilerParams(
            dimension_semantics=("arbitrary", "arbitrary"), vmem_limit_bytes=VMEM_LIMIT),
        name="adaln_modulation",
    )(c_pad, mod_w.astype(F32), mod_b.astype(F32).reshape(depth, 1, n))
    return out[:, :bsz].reshape(depth, bsz, N_SUB * N_MOD, d)


def _rms(x, g):
    return x * lax.rsqrt(jnp.mean(x * x, axis=-1, keepdims=True) + EPS) * g


def _modnorm(x, g_ref, mod, sub):
    r = N_MOD * sub
    shift, scale = mod[r:r + 1], mod[r + 1:r + 2]
    return _rms(x, g_ref[sub:sub + 1, :]) * (1.0 + scale) + shift


def _swiglu_step(x, g_ref, mod, sub, wg_ref, wu_ref, wout_ref, act_ref):
    h = _modnorm(x, g_ref, mod, sub).astype(BF16)
    n_chunks = act_ref.shape[1] // FF_CHUNK
    for k in range(n_chunks):
        cols = slice(FF_CHUNK * k, FF_CHUNK * (k + 1))
        gate = _dot(h, wg_ref[:, cols])
        up = _dot(h, wu_ref[:, cols])
        act_ref[:, cols] = (gate * _sigmoid(gate) * up).astype(BF16)
    out = _dot(act_ref[...], wout_ref[...])
    r = N_MOD * sub + 2
    return x + (0.5 * mod[r:r + 1]) * out


def _gelu_tanh(v):
    return 0.5 * v * (1.0 + jnp.tanh(math.sqrt(2.0 / math.pi) * (v + 0.044715 * (v * v * v))))


def _pool_update(x, xp_ref, xn_ref, g_ref, mod, pw_ref, pb_ref, ps_ref, hs_ref, seq_len):
    tm, d = x.shape
    cg = d // len(POOL_WINDOWS)
    tiles_per_seq = seq_len // tm
    j = pl.program_id(0) % tiles_per_seq
    h = _modnorm(x, g_ref, mod, 1)
    hp = _modnorm(xp_ref[...], g_ref, mod, 1)
    hn = _modnorm(xn_ref[...], g_ref, mod, 1)
    hs_ref[0:POOL_HALO, :] = jnp.where(j > 0, hp, 0.0)
    hs_ref[POOL_HALO:POOL_HALO + tm, :] = h
    hs_ref[POOL_HALO + tm:, :] = jnp.where(j < tiles_per_seq - 1, hn, 0.0)
    t = j * tm + lax.broadcasted_iota(jnp.int32, (tm, 1), 0)
    ys = []
    for gi, window in enumerate(POOL_WINDOWS):
        left = window // 2
        right = window - 1 - left
        cols = slice(gi * cg, (gi + 1) * cg)
        total = hs_ref[POOL_HALO - left:POOL_HALO - left + tm, cols]
        for s in range(-left + 1, right + 1):
            total = total + hs_ref[POOL_HALO + s:POOL_HALO + s + tm, cols]
        lo = jnp.maximum(t - left, 0)
        hi = jnp.minimum(t + right, seq_len - 1)
        count = (hi - lo + 1).astype(F32)
        p = total / count - h[:, cols]
        ys.append(_dot(p.astype(BF16), pw_ref[gi]))
    y = (jnp.concatenate(ys, axis=1) + pb_ref[...]) * ps_ref[...]
    return x + mod[N_MOD + 2:N_MOD + 3] * y


def _s5_tail_update(x, ys, g_ref, mod, dskip_ref, gw_ref, gb_ref):
    d = x.shape[1]
    h = _modnorm(x, g_ref, mod, 1)
    y = _gelu_tanh(dskip_ref[...] * h + ys)
    z = _dot(y.astype(BF16), gw_ref[...]) + gb_ref[...]
    m = z[:, :d] * _sigmoid(z[:, d:])
    return x + mod[N_MOD + 2:N_MOD + 3] * m


def _layer_kernel(mixer, emit_h, final_norm, seq_len, *refs):
    refs = list(refs)
    x_ref, mod_ref, g_ref = refs[:3]
    del refs[:3]
    x = x_ref[...]
    mod = mod_ref[0]
    if mixer == "pool":
        xp_ref, xn_ref, pw_ref, pb_ref, ps_ref = refs[:5]
        del refs[:5]
    elif mixer == "s5":
        ys_ref, dskip_ref, gw_ref, gb_ref = refs[:4]
        del refs[:4]
    wg_ref, wu_ref, wout_ref = refs[:3]
    del refs[:3]
    if final_norm:
        fg_ref = refs.pop(0)
    o_ref = refs.pop(0)
    if emit_h:
        h_ref = refs.pop(0)
    act_ref = refs.pop(0)
    if mixer == "pool":
        hs_ref = refs.pop(0)
        x = _pool_update(x, xp_ref, xn_ref, g_ref, mod, pw_ref, pb_ref, ps_ref, hs_ref, seq_len)
    elif mixer == "s5":
        rows_ref = refs.pop(0)
        n_slab, n_chunk = rows_ref.shape[0], ys_ref.shape[1]
        for l in range(CHUNK):
            for s in range(n_slab):
                rows_ref[s, pl.ds(l, n_chunk, stride=CHUNK_PITCH), :] = ys_ref[l, :, LANES * s:LANES * (s + 1)]
        ys = jnp.concatenate(
            [jnp.concatenate([rows_ref[s, CHUNK_PITCH * c:CHUNK_PITCH * c + CHUNK, :]
                              for c in range(n_chunk)], axis=0) for s in range(n_slab)], axis=1)
        x = _s5_tail_update(x, ys, g_ref, mod, dskip_ref, gw_ref, gb_ref)
    sub = 0 if mixer is None else 2
    x = _swiglu_step(x, g_ref, mod, sub, wg_ref, wu_ref, wout_ref, act_ref)
    if final_norm:
        x = _rms(x, fg_ref[...])
    o_ref[...] = x
    if emit_h:
        rows_ref = refs.pop(0)
        n_slab, n_chunk = rows_ref.shape[0], h_ref.shape[1]
        h = _modnorm(x, g_ref, mod, 1)
        for c in range(n_chunk):
            for s in range(n_slab):
                rows_ref[s, CHUNK_PITCH * c:CHUNK_PITCH * c + CHUNK, :] = (
                    h[CHUNK * c:CHUNK * (c + 1), LANES * s:LANES * (s + 1)])
        for l in range(CHUNK):
            for s in range(n_slab):
                h_ref[l, :, LANES * s:LANES * (s + 1)] = rows_ref[s, pl.ds(l, n_chunk, stride=CHUNK_PITCH), :]


def _const_spec(shape):
    nd = len(shape)
    return pl.BlockSpec(shape, lambda *_: (0,) * nd, pipeline_mode=pl.Buffered(1))


def _prep_ffn_weights(w_in, w_out):
    d_ff = w_in.shape[1] // 2
    pad = -d_ff % FF_CHUNK
    wg = jnp.pad(w_in[:, :d_ff].astype(BF16), ((0, 0), (0, pad)))
    wu = jnp.pad(w_in[:, d_ff:].astype(BF16), ((0, 0), (0, pad)))
    wout = jnp.pad(w_out.astype(BF16), ((0, pad), (0, 0)))
    return wg, wu, wout


def _layer_call(x, mod, norm_g, w_in, w_out, seq_len, tm, *, mixer=None, mixer_args=(),
                emit_h=False, final_g=None):
    n_tok, d = x.shape
    assert seq_len % tm == 0 and tm % POOL_HALO == 0 and tm % (CHUNK * SUBLANES) == 0
    tiles_per_seq = seq_len // tm
    wg, wu, wout = _prep_ffn_weights(w_in, w_out)
    ffp = wout.shape[0]
    row_spec = pl.BlockSpec((tm, d), lambda i: (i, 0))
    chunk_spec = pl.BlockSpec((None, CHUNK, tm // CHUNK, d),
                              lambda i: (i // tiles_per_seq, 0, i % tiles_per_seq, 0))
    args = [x, mod, norm_g.astype(F32)]
    specs = [row_spec,
             pl.BlockSpec((1, N_SUB * N_MOD, d), lambda i: (i // tiles_per_seq, 0, 0)),
             _const_spec((N_SUB, d))]
    scratch = [pltpu.VMEM((tm, ffp), BF16)]
    if mixer == "pool":
        pool_w, pool_b, pool_scale = mixer_args
        hb = tm // POOL_HALO
        last = n_tok // POOL_HALO - 1
        args += [x, x, pool_w.astype(BF16), pool_b.astype(F32).reshape(1, d),
                 pool_scale.astype(F32).reshape(1, d)]
        specs += [pl.BlockSpec((POOL_HALO, d), lambda i: (jnp.maximum(i * hb - 1, 0), 0)),
                  pl.BlockSpec((POOL_HALO, d), lambda i: (jnp.minimum((i + 1) * hb, last), 0)),
                  _const_spec(pool_w.shape), _const_spec((1, d)), _const_spec((1, d))]
        scratch.append(pltpu.VMEM((tm + 2 * POOL_HALO, d), F32))
    elif mixer == "s5":
        ys, d_skip, glu_w, glu_b = mixer_args
        args += [ys, d_skip.astype(F32).reshape(1, d), glu_w.astype(BF16),
                 glu_b.astype(F32).reshape(1, 2 * d)]
        specs += [chunk_spec, _const_spec((1, d)), _const_spec((d, 2 * d)), _const_spec((1, 2 * d))]
        scratch.append(pltpu.VMEM((d // LANES, tm // CHUNK * CHUNK_PITCH, LANES), F32))
    args += [wg, wu, wout]
    specs += [_const_spec(wg.shape), _const_spec(wu.shape), _const_spec(wout.shape)]
    if final_g is not None:
        args.append(final_g.astype(F32).reshape(1, d))
        specs.append(_const_spec((1, d)))
    out_shape = [jax.ShapeDtypeStruct((n_tok, d), F32)]
    out_specs = [row_spec]
    if emit_h:
        out_shape.append(jax.ShapeDtypeStruct((n_tok // seq_len, CHUNK, seq_len // CHUNK, d), F32))
        out_specs.append(chunk_spec)
        scratch.append(pltpu.VMEM((d // LANES, tm // CHUNK * CHUNK_PITCH, LANES), F32))
    outs = pl.pallas_call(
        functools.partial(_layer_kernel, mixer, emit_h, final_g is not None, seq_len),
        grid=(n_tok // tm,),
        in_specs=specs,
        out_specs=out_specs,
        out_shape=out_shape,
        scratch_shapes=scratch,
        compiler_params=pltpu.CompilerParams(
            dimension_semantics=("arbitrary",), vmem_limit_bytes=VMEM_LIMIT),
        name="layer_" + (mixer or "ffn"),
    )(*args)
    return outs if emit_h else outs[0]


def _cmul(ar, ai, br, bi):
    return ar * br - ai * bi, ar * bi + ai * br


def _s5_prep_kernel(n_levels, lr_ref, li_ref, ldt_ref, bre_ref, bim_ref, cre_ref, cim_ref,
                    t_ref, bz_ref, cx_ref, dec_ref):
    gb = lr_ref.shape[1]
    w = t_ref.shape[1]
    chunk = w // SSM_H
    n_dbl = chunk.bit_length() - 1
    assert chunk == 1 << n_dbl
    lane = lax.broadcasted_iota(jnp.int32, (1, LANES), 1)
    re_half = lane < SSM_P
    wlane = lax.broadcasted_iota(jnp.int32, (1, w), 1)
    level_e = (chunk << lax.broadcasted_iota(jnp.int32, (n_levels, 1), 0)).astype(F32)

    def tile_rows(v):
        return jnp.broadcast_to(v[None], (chunk, SSM_H, LANES)).reshape(w, LANES)

    for g in range(gb):
        rs, zs, cs = [], [], []
        for direction in range(2):
            lr, li = lr_ref[direction, g], li_ref[direction, g]
            dt = jnp.exp(ldt_ref[direction, g])
            lrdt, lidt = lr * dt, li * dt

            def a_pow(e):
                mag = jnp.exp(lrdt * e)
                return mag * jnp.cos(lidt * e), mag * jnp.sin(lidt * e)

            ar, ai = a_pow(1.0)
            den = lr * lr + li * li
            fr = ((ar - 1.0) * lr + ai * li) / den
            fi = (ai * lr - (ar - 1.0) * li) / den
            bre, bim = bre_ref[direction, g], bim_ref[direction, g]
            bbr, bbi = _cmul(fr, fi, bre, bim)
            cre, cim = tile_rows(cre_ref[direction, g]), tile_rows(cim_ref[direction, g])
            bbr_t, bbi_t = tile_rows(bbr), tile_rows(bbi)

            one = jnp.ones((SSM_H, LANES), F32)
            up_r, up_i, dn_r, dn_i = one, 0.0 * one, one, 0.0 * one
            for k in range(n_dbl):
                sr, si = a_pow(float(1 << k))
                nr, ni = _cmul(up_r, up_i, sr, si)
                up_r, up_i = jnp.concatenate([up_r, nr], axis=0), jnp.concatenate([up_i, ni], axis=0)
                nr, ni = _cmul(dn_r, dn_i, sr, si)
                dn_r, dn_i = jnp.concatenate([nr, dn_r], axis=0), jnp.concatenate([ni, dn_i], axis=0)
            up1_r, up1_i = _cmul(up_r, up_i, ar, ai)
            dn1_r, dn1_i = _cmul(dn_r, dn_i, ar, ai)

            def c_times(er, ei):
                pr, pi = _cmul(cre, cim, er, ei)
                return jnp.where(re_half, pr, -pi)

            def b_times(er, ei):
                pr, pi = _cmul(bbr_t, bbi_t, er, ei)
                return jnp.where(re_half, pr, pi)

            fwd = direction == 0
            zs.append(b_times(dn_r, dn_i) if fwd else b_times(up_r, up_i))
            cs.append(c_times(up1_r, up1_i) if fwd else c_times(dn1_r, dn1_i))
            gmat = (c_times(up_r, up_i) if fwd else c_times(dn_r, dn_i)).T
            bbt = jnp.where(re_half, bbr, bbi)
            rs.append(_dot3(bbt, gmat))
            er, ei = a_pow(level_e)
            base = 2 * n_levels * direction
            dec_ref[g, base:base + n_levels, :] = er
            dec_ref[g, base + n_levels:base + 2 * n_levels, :] = jnp.where(re_half, -ei, ei)
        for i in range(chunk):
            s = SSM_H * i
            fwd_rows = jnp.where(wlane >= s, pltpu.roll(rs[0], s, 1) if s else rs[0], 0.0)
            e = (s + SSM_H) % w
            bwd_rows = jnp.where(wlane < s + SSM_H, pltpu.roll(rs[1], e, 1) if e else rs[1], 0.0)
            t_ref[g, s:s + SSM_H, :] = (fwd_rows + bwd_rows).astype(BF16)
        bz_ref[g] = jnp.concatenate(zs, axis=1).astype(BF16)
        cx_ref[g] = jnp.concatenate([cs[0].T, cs[1].T], axis=0).astype(BF16)


def _s5_prep(lam_re, lam_im, log_dt, b_re, b_im, c_re, c_im, n_levels):
    n_dir, n_grp, n_state = lam_re.shape
    assert n_dir == 2 and n_state == SSM_P and b_re.shape[-1] == SSM_H
    w = CHUNK * SSM_H
    gb = GROUPS_PER_SLAB
    assert n_grp % gb == 0 and n_levels % SUBLANES == 0

    def twice(v):
        return jnp.concatenate([v, v], axis=-1).astype(F32)

    lr = twice(lam_re)[:, :, None, :]
    li = twice(lam_im)[:, :, None, :]
    ldt = jnp.broadcast_to(log_dt.astype(F32)[:, :, None, None], (n_dir, n_grp, 1, LANES))
    bre, bim = twice(jnp.swapaxes(b_re, 2, 3)), twice(jnp.swapaxes(b_im, 2, 3))
    cre, cim = twice(c_re), twice(c_im)
    n_dec = 2 * 2 * n_levels
    row = lambda r: pl.BlockSpec((n_dir, gb, r, LANES), lambda i: (0, i, 0, 0))
    return pl.pallas_call(
        functools.partial(_s5_prep_kernel, n_levels),
        grid=(n_grp // gb,),
        in_specs=[row(1), row(1), row(1), row(SSM_H), row(SSM_H), row(SSM_H), row(SSM_H)],
        out_specs=[pl.BlockSpec((gb, w, w), lambda i: (i, 0, 0)),
                   pl.BlockSpec((gb, w, 2 * LANES), lambda i: (i, 0, 0)),
                   pl.BlockSpec((gb, 2 * LANES, w), lambda i: (i, 0, 0)),
                   pl.BlockSpec((gb, n_dec, LANES), lambda i: (i, 0, 0))],
        out_shape=[jax.ShapeDtypeStruct((n_grp, w, w), BF16),
                   jax.ShapeDtypeStruct((n_grp, w, 2 * LANES), BF16),
                   jax.ShapeDtypeStruct((n_grp, 2 * LANES, w), BF16),
                   jax.ShapeDtypeStruct((n_grp, n_dec, LANES), F32)],
        compiler_params=pltpu.CompilerParams(
            dimension_semantics=("arbitrary",), vmem_limit_bytes=VMEM_LIMIT),
        name="s5_prep",
    )(lr, li, ldt, bre, bim, cre, cim)


def _chunk_scan(z, dec_ref, g, base, n_levels, reverse):
    n_rows = z.shape[0]
    rc = lax.broadcasted_iota(jnp.int32, (n_rows, 1), 0)

    def shifted(v, dist):
        if reverse:
            return jnp.where(rc < n_rows - dist, pltpu.roll(v, n_rows - dist, 0), 0.0)
        return jnp.where(rc >= dist, pltpu.roll(v, dist, 0), 0.0)

    s = shifted(z, 1)
    for k in range(n_levels):
        if (1 << k) >= n_rows:
            break
        sh = shifted(s, 1 << k)
        a_re = dec_ref[g, base + k:base + k + 1, :]
        a_im = dec_ref[g, base + n_levels + k:base + n_levels + k + 1, :]
        s = s + (sh * a_re + pltpu.roll(sh, SSM_P, 1) * a_im)
    return s


def _slab_permutation():
    n = GROUPS_PER_SLAB
    idx = np.arange(n * LANES)
    a, b, h = idx // LANES, (idx % LANES) // SSM_H, idx % SSM_H
    perm = np.zeros((n * LANES, n * LANES), np.float32)
    perm[idx, b * LANES + a * SSM_H + h] = 1.0
    return jnp.asarray(perm, BF16)


def _s5_core_kernel(n_levels, h_ref, perm_ref, t_ref, bz_ref, cx_ref, dec_ref, y_ref, u_ref, yg_ref):
    n_grp = GROUPS_PER_SLAB
    perm = perm_ref[...]
    for o in range(CHUNK // n_grp):
        v = jnp.concatenate([h_ref[n_grp * o + l8] for l8 in range(n_grp)], axis=1)
        uv = _dot(v.astype(BF16), perm).astype(BF16)
        for g in range(n_grp):
            u_ref[g, :, LANES * o:LANES * (o + 1)] = uv[:, LANES * g:LANES * (g + 1)]
    for g in range(n_grp):
        u = u_ref[g]
        z = _dot(u, bz_ref[g])
        xf = _chunk_scan(z[:, :LANES], dec_ref, g, 0, n_levels, False)
        xb = _chunk_scan(z[:, LANES:], dec_ref, g, 2 * n_levels, n_levels, True)
        y = _dot(u, t_ref[g])
        y = y + _dot(xf.astype(BF16), cx_ref[g, :LANES, :])
        y = y + _dot(xb.astype(BF16), cx_ref[g, LANES:, :])
        yg_ref[g] = y
    for o in range(CHUNK // n_grp):
        yv = jnp.concatenate([yg_ref[g, :, LANES * o:LANES * (o + 1)] for g in range(n_grp)], axis=1)
        hi, lo = _split3(yv)
        r = _dot(hi, perm) + _dot(lo, perm)
        for l8 in range(n_grp):
            y_ref[n_grp * o + l8] = r[:, LANES * l8:LANES * (l8 + 1)]


def _s5_mix(h, lam_re, lam_im, log_dt, b_re, b_im, c_re, c_im):
    bsz, _, rows, d = h.shape
    w = CHUNK * SSM_H
    gb = GROUPS_PER_SLAB
    assert h.shape[1] == CHUNK and CHUNK % gb == 0 and d % LANES == 0
    n_levels = -(-max(1, (rows - 1).bit_length()) // SUBLANES) * SUBLANES
    t_mat, bz, cx, dec = _s5_prep(lam_re, lam_im, log_dt, b_re, b_im, c_re, c_im, n_levels)
    perm = _slab_permutation()
    slab = pl.BlockSpec((None, CHUNK, rows, LANES), lambda s, b: (b, 0, 0, s))
    grp = lambda r, c: pl.BlockSpec((gb, r, c), lambda s, b: (s, 0, 0))
    return pl.pallas_call(
        functools.partial(_s5_core_kernel, n_levels),
        grid=(d // LANES, bsz),
        in_specs=[slab, _const_spec(perm.shape), grp(w, w), grp(w, 2 * LANES), grp(2 * LANES, w),
                  grp(dec.shape[1], LANES)],
        out_specs=slab,
        out_shape=jax.ShapeDtypeStruct(h.shape, F32),
        scratch_shapes=[pltpu.VMEM((gb, rows, w), BF16), pltpu.VMEM((gb, rows, w), F32)],
        compiler_params=pltpu.CompilerParams(
            dimension_semantics=("arbitrary", "arbitrary"), vmem_limit_bytes=VMEM_LIMIT),
        name="s5_core",
    )(h, perm, t_mat, bz, cx, dec)


def kernel(x, c, mod_w, mod_b, norm_g, ffn_w_in, ffn_w_out, pool_w, pool_b, pool_scale,
           ssm_lam_re, ssm_lam_im, ssm_log_dt, ssm_b_re, ssm_b_im, ssm_c_re, ssm_c_im,
           ssm_d, glu_w, glu_b, final_g):
    bsz, seq_len, d = x.shape
    depth = mod_w.shape[0]
    n_mixers = 2
    tm = min(512, seq_len)
    mods = _modulation(c, mod_w, mod_b)
    xf = x.astype(F32).reshape(bsz * seq_len, d)
    for i in range(depth):
        j = i // n_mixers
        is_s5 = i % n_mixers == 1
        final = final_g if i == depth - 1 else None
        first = _layer_call(xf, mods[i], norm_g[i], ffn_w_in[i, 0], ffn_w_out[i, 0], seq_len, tm,
                            emit_h=is_s5)
        if is_s5:
            xf, h = first
            ys = _s5_mix(h, ssm_lam_re[j], ssm_lam_im[j], ssm_log_dt[j], ssm_b_re[j],
                         ssm_b_im[j], ssm_c_re[j], ssm_c_im[j])
            xf = _layer_call(xf, mods[i], norm_g[i], ffn_w_in[i, 1], ffn_w_out[i, 1], seq_len, tm,
                             mixer="s5", mixer_args=(ys, ssm_d[j], glu_w[j], glu_b[j]),
                             final_g=final)
        else:
            xf = _layer_call(first, mods[i], norm_g[i], ffn_w_in[i, 1], ffn_w_out[i, 1], seq_len, tm,
                             mixer="pool", mixer_args=(pool_w[j], pool_b[j], pool_scale[j]),
                             final_g=final)
    return xf.reshape(bsz, seq_len, d).astype(x.dtype)
```

```python
import functools
import math

import numpy as np

import jax
import jax.numpy as jnp
from jax import lax
from jax.experimental import pallas as pl
from jax.experimental.pallas import tpu as pltpu

F32 = jnp.float32
BF16 = jnp.bfloat16

EPS = 1e-6
N_SUB = 3
N_MOD = 3
POOL_WINDOWS = (2, 4, 8, 16)
POOL_HALO = 8
SSM_H = 16
SSM_P = 64
CHUNK = 32
CHUNK_PITCH = CHUNK + 4
FF_CHUNK = 256
LANES = 128
SUBLANES = 8
GROUPS_PER_SLAB = LANES // SSM_H
VMEM_LIMIT = 56 * 1024 * 1024


def _sigmoid(v):
    return 1.0 / (1.0 + jnp.exp(-v))


def _split3(a):
    hi = a.astype(BF16)
    lo = (a - hi.astype(F32)).astype(BF16)
    return hi, lo


def _dot(a, b):
    return jnp.dot(a, b, preferred_element_type=F32)


def _dot3(a, b):
    ah, al = _split3(a)
    bh, bl = _split3(b)
    return _dot(ah, bh) + (_dot(ah, bl) + _dot(al, bh))


def _mod_kernel(c_ref, w_ref, b_ref, o_ref):
    c = c_ref[...]
    cond = c * _sigmoid(c)
    o_ref[0] = _dot3(cond, w_ref[0]) + b_ref[0]


def _modulation(c, mod_w, mod_b):
    depth, d, n = mod_w.shape
    bsz = c.shape[0]
    rows = SUBLANES
    tn = 1024
    assert n % tn == 0 and bsz <= rows
    c_pad = jnp.zeros((rows, d), F32).at[:bsz].set(c.astype(F32))
    out = pl.pallas_call(
        _mod_kernel,
        grid=(depth, n // tn),
        in_specs=[
            pl.BlockSpec((rows, d), lambda i, j: (0, 0)),
            pl.BlockSpec((1, d, tn), lambda i, j: (i, 0, j)),
            pl.BlockSpec((1, 1, tn), lambda i, j: (i, 0, j)),
        ],
        out_specs=pl.BlockSpec((1, rows, tn), lambda i, j: (i, 0, j)),
        out_shape=jax.ShapeDtypeStruct((depth, rows, n), F32),
        compiler_params=pltpu.CompilerParams(
            dimension_semantics=("arbitrary", "arbitrary"), vmem_limit_bytes=VMEM_LIMIT),
        name="adaln_modulation",
    )(c_pad, mod_w.astype(F32), mod_b.astype(F32).reshape(depth, 1, n))
    return out[:, :bsz].reshape(depth, bsz, N_SUB * N_MOD, d)


def _rms(x, g):
    return x * lax.rsqrt(jnp.mean(x * x, axis=-1, keepdims=True) + EPS) * g


def _modnorm(x, g_ref, mod, sub):
    r = N_MOD * sub
    shift, scale = mod[r:r + 1], mod[r + 1:r + 2]
    return _rms(x, g_ref[sub:sub + 1, :]) * (1.0 + scale) + shift


def _swiglu_step(x, g_ref, mod, sub, wg_ref, wu_ref, wout_ref, act_ref):
    h = _modnorm(x, g_ref, mod, sub).astype(BF16)
    n_chunks = act_ref.shape[1] // FF_CHUNK
    for k in range(n_chunks):
        cols = slice(FF_CHUNK * k, FF_CHUNK * (k + 1))
        gate = _dot(h, wg_ref[:, cols])
        up = _dot(h, wu_ref[:, cols])
        act_ref[:, cols] = (gate * _sigmoid(gate) * up).astype(BF16)
    out = _dot(act_ref[...], wout_ref[...])
    r = N_MOD * sub + 2
    return x + (0.5 * mod[r:r + 1]) * out


def _gelu_tanh(v):
    return 0.5 * v * (1.0 + jnp.tanh(math.sqrt(2.0 / math.pi) * (v + 0.044715 * (v * v * v))))


def _pool_update(x, xp_ref, xn_ref, g_ref, mod, pw_ref, pb_ref, ps_ref, hs_ref, seq_len):
    tm, d = x.shape
    cg = d // len(POOL_WINDOWS)
    tiles_per_seq = seq_len // tm
    j = pl.program_id(0) % tiles_per_seq
    h = _modnorm(x, g_ref, mod, 1)
    hp = _modnorm(xp_ref[...], g_ref, mod, 1)
    hn = _modnorm(xn_ref[...], g_ref, mod, 1)
    hs_ref[0:POOL_HALO, :] = jnp.where(j > 0, hp, 0.0)
    hs_ref[POOL_HALO:POOL_HALO + tm, :] = h
    hs_ref[POOL_HALO + tm:, :] = jnp.where(j < tiles_per_seq - 1, hn, 0.0)
    t = j * tm + lax.broadcasted_iota(jnp.int32, (tm, 1), 0)
    ys = []
    for gi, window in enumerate(POOL_WINDOWS):
        left = window // 2
        right = window - 1 - left
        cols = slice(gi * cg, (gi + 1) * cg)
        assert window & (window - 1) == 0 and right < POOL_HALO
        run = hs_ref[:, cols]
        span = 1
        while span < window:
            run = run + pltpu.roll(run, span, 0)
            span *= 2
        if right:
            run = pltpu.roll(run, tm + 2 * POOL_HALO - right, 0)
        total = run[POOL_HALO:POOL_HALO + tm]
        lo = jnp.maximum(t - left, 0)
        hi = jnp.minimum(t + right, seq_len - 1)
        count = (hi - lo + 1).astype(F32)
        p = total / count - h[:, cols]
        ys.append(_dot(p.astype(BF16), pw_ref[gi]))
    y = (jnp.concatenate(ys, axis=1) + pb_ref[...]) * ps_ref[...]
    return x + mod[N_MOD + 2:N_MOD + 3] * y


def _s5_tail_update(x, ys, g_ref, mod, dskip_ref, gw_ref, gb_ref):
    d = x.shape[1]
    h = _modnorm(x, g_ref, mod, 1)
    y = _gelu_tanh(dskip_ref[...] * h + ys)
    z = _dot(y.astype(BF16), gw_ref[...]) + gb_ref[...]
    m = z[:, :d] * _sigmoid(z[:, d:])
    return x + mod[N_MOD + 2:N_MOD + 3] * m


def _layer_kernel(mixer, emit_h, final_norm, seq_len, *refs):
    refs = list(refs)
    x_ref, mod_ref, g_ref = refs[:3]
    del refs[:3]
    x = x_ref[...]
    mod = mod_ref[0]
    if mixer == "pool":
        xp_ref, xn_ref, pw_ref, pb_ref, ps_ref = refs[:5]
        del refs[:5]
    elif mixer == "s5":
        ys_ref, dskip_ref, gw_ref, gb_ref = refs[:4]
        del refs[:4]
    wg_ref, wu_ref, wout_ref = refs[:3]
    del refs[:3]
    if final_norm:
        fg_ref = refs.pop(0)
    o_ref = refs.pop(0)
    if emit_h:
        h_ref = refs.pop(0)
    act_ref = refs.pop(0)
    if mixer == "pool":
        hs_ref = refs.pop(0)
        x = _pool_update(x, xp_ref, xn_ref, g_ref, mod, pw_ref, pb_ref, ps_ref, hs_ref, seq_len)
    elif mixer == "s5":
        rows_ref = refs.pop(0)
        n_slab, n_chunk = rows_ref.shape[0], ys_ref.shape[1]
        for l in range(CHUNK):
            for s in range(n_slab):
                rows_ref[s, pl.ds(l, n_chunk, stride=CHUNK_PITCH), :] = ys_ref[l, :, LANES * s:LANES * (s + 1)]
        ys = jnp.concatenate(
            [jnp.concatenate([rows_ref[s, CHUNK_PITCH * c:CHUNK_PITCH * c + CHUNK, :]
                              for c in range(n_chunk)], axis=0) for s in range(n_slab)], axis=1)
        x = _s5_tail_update(x, ys, g_ref, mod, dskip_ref, gw_ref, gb_ref)
    sub = 0 if mixer is None else 2
    x = _swiglu_step(x, g_ref, mod, sub, wg_ref, wu_ref, wout_ref, act_ref)
    if final_norm:
        x = _rms(x, fg_ref[...])
    o_ref[...] = x
    if emit_h:
        rows_ref = refs.pop(0)
        n_slab, n_chunk = rows_ref.shape[0], h_ref.shape[1]
        h = _modnorm(x, g_ref, mod, 1)
        for c in range(n_chunk):
            for s in range(n_slab):
                rows_ref[s, CHUNK_PITCH * c:CHUNK_PITCH * c + CHUNK, :] = (
                    h[CHUNK * c:CHUNK * (c + 1), LANES * s:LANES * (s + 1)])
        for l in range(CHUNK):
            for s in range(n_slab):
                h_ref[l, :, LANES * s:LANES * (s + 1)] = rows_ref[s, pl.ds(l, n_chunk, stride=CHUNK_PITCH), :]


def _const_spec(shape):
    nd = len(shape)
    return pl.BlockSpec(shape, lambda *_: (0,) * nd, pipeline_mode=pl.Buffered(1))


def _prep_ffn_weights(w_in, w_out):
    d_ff = w_in.shape[1] // 2
    pad = -d_ff % FF_CHUNK
    wg = jnp.pad(w_in[:, :d_ff].astype(BF16), ((0, 0), (0, pad)))
    wu = jnp.pad(w_in[:, d_ff:].astype(BF16), ((0, 0), (0, pad)))
    wout = jnp.pad(w_out.astype(BF16), ((0, pad), (0, 0)))
    return wg, wu, wout


def _layer_call(x, mod, norm_g, w_in, w_out, seq_len, tm, *, mixer=None, mixer_args=(),
                emit_h=False, final_g=None):
    n_tok, d = x.shape
    assert seq_len % tm == 0 and tm % POOL_HALO == 0 and tm % (CHUNK * SUBLANES) == 0
    tiles_per_seq = seq_len // tm
    wg, wu, wout = _prep_ffn_weights(w_in, w_out)
    ffp = wout.shape[0]
    row_spec = pl.BlockSpec((tm, d), lambda i: (i, 0))
    chunk_spec = pl.BlockSpec((None, CHUNK, tm // CHUNK, d),
                              lambda i: (i // tiles_per_seq, 0, i % tiles_per_seq, 0))
    args = [x, mod, norm_g.astype(F32)]
    specs = [row_spec,
             pl.BlockSpec((1, N_SUB * N_MOD, d), lambda i: (i // tiles_per_seq, 0, 0)),
             _const_spec((N_SUB, d))]
    scratch = [pltpu.VMEM((tm, ffp), BF16)]
    if mixer == "pool":
        pool_w, pool_b, pool_scale = mixer_args
        hb = tm // POOL_HALO
        last = n_tok // POOL_HALO - 1
        args += [x, x, pool_w.astype(BF16), pool_b.astype(F32).reshape(1, d),
                 pool_scale.astype(F32).reshape(1, d)]
        specs += [pl.BlockSpec((POOL_HALO, d), lambda i: (jnp.maximum(i * hb - 1, 0), 0)),
                  pl.BlockSpec((POOL_HALO, d), lambda i: (jnp.minimum((i + 1) * hb, last), 0)),
                  _const_spec(pool_w.shape), _const_spec((1, d)), _const_spec((1, d))]
        scratch.append(pltpu.VMEM((tm + 2 * POOL_HALO, d), F32))
    elif mixer == "s5":
        ys, d_skip, glu_w, glu_b = mixer_args
        args += [ys, d_skip.astype(F32).reshape(1, d), glu_w.astype(BF16),
                 glu_b.astype(F32).reshape(1, 2 * d)]
        specs += [chunk_spec, _const_spec((1, d)), _const_spec((d, 2 * d)), _const_spec((1, 2 * d))]
        scratch.append(pltpu.VMEM((d // LANES, tm // CHUNK * CHUNK_PITCH, LANES), F32))
    args += [wg, wu, wout]
    specs += [_const_spec(wg.shape), _const_spec(wu.shape), _const_spec(wout.shape)]
    if final_g is not None:
        args.append(final_g.astype(F32).reshape(1, d))
        specs.append(_const_spec((1, d)))
    out_shape = [jax.ShapeDtypeStruct((n_tok, d), F32)]
    out_specs = [row_spec]
    if emit_h:
        out_shape.append(jax.ShapeDtypeStruct((n_tok // seq_len, CHUNK, seq_len // CHUNK, d), F32))
        out_specs.append(chunk_spec)
        scratch.append(pltpu.VMEM((d // LANES, tm // CHUNK * CHUNK_PITCH, LANES), F32))
    outs = pl.pallas_call(
        functools.partial(_layer_kernel, mixer, emit_h, final_g is not None, seq_len),
        grid=(n_tok // tm,),
        in_specs=specs,
        out_specs=out_specs,
        out_shape=out_shape,
        scratch_shapes=scratch,
        compiler_params=pltpu.CompilerParams(
            dimension_semantics=("arbitrary",), vmem_limit_bytes=VMEM_LIMIT),
        name="layer_" + (mixer or "ffn"),
    )(*args)
    return outs if emit_h else outs[0]


def _cmul(ar, ai, br, bi):
    return ar * br - ai * bi, ar * bi + ai * br


def _s5_prep_kernel(n_levels, lr_ref, li_ref, ldt_ref, bre_ref, bim_ref, cre_ref, cim_ref,
                    t_ref, bz_ref, cx_ref, dec_ref):
    gb = lr_ref.shape[1]
    w = t_ref.shape[1]
    chunk = w // SSM_H
    n_dbl = chunk.bit_length() - 1
    assert chunk == 1 << n_dbl
    lane = lax.broadcasted_iota(jnp.int32, (1, LANES), 1)
    re_half = lane < SSM_P
    wlane = lax.broadcasted_iota(jnp.int32, (1, w), 1)
    level_e = (chunk << lax.broadcasted_iota(jnp.int32, (n_levels, 1), 0)).astype(F32)

    def tile_rows(v):
        return jnp.broadcast_to(v[None], (chunk, SSM_H, LANES)).reshape(w, LANES)

    for g in range(gb):
        rs, zs, cs = [], [], []
        for direction in range(2):
            lr, li = lr_ref[direction, g], li_ref[direction, g]
            dt = jnp.exp(ldt_ref[direction, g])
            lrdt, lidt = lr * dt, li * dt

            def a_pow(e):
                mag = jnp.exp(lrdt * e)
                return mag * jnp.cos(lidt * e), mag * jnp.sin(lidt * e)

            ar, ai = a_pow(1.0)
            den = lr * lr + li * li
            fr = ((ar - 1.0) * lr + ai * li) / den
            fi = (ai * lr - (ar - 1.0) * li) / den
            bre, bim = bre_ref[direction, g], bim_ref[direction, g]
            bbr, bbi = _cmul(fr, fi, bre, bim)
            cre, cim = tile_rows(cre_ref[direction, g]), tile_rows(cim_ref[direction, g])
            bbr_t, bbi_t = tile_rows(bbr), tile_rows(bbi)

            one = jnp.ones((SSM_H, LANES), F32)
            up_r, up_i, dn_r, dn_i = one, 0.0 * one, one, 0.0 * one
            for k in range(n_dbl):
                sr, si = a_pow(float(1 << k))
                nr, ni = _cmul(up_r, up_i, sr, si)
                up_r, up_i = jnp.concatenate([up_r, nr], axis=0), jnp.concatenate([up_i, ni], axis=0)
                nr, ni = _cmul(dn_r, dn_i, sr, si)
                dn_r, dn_i = jnp.concatenate([nr, dn_r], axis=0), jnp.concatenate([ni, dn_i], axis=0)
            up1_r, up1_i = _cmul(up_r, up_i, ar, ai)
            dn1_r, dn1_i = _cmul(dn_r, dn_i, ar, ai)

            def c_times(er, ei):
                pr, pi = _cmul(cre, cim, er, ei)
                return jnp.where(re_half, pr, -pi)

            def b_times(er, ei):
                pr, pi = _cmul(bbr_t, bbi_t, er, ei)
                return jnp.where(re_half, pr, pi)

            fwd = direction == 0
            zs.append(b_times(dn_r, dn_i) if fwd else b_times(up_r, up_i))
            cs.append(c_times(up1_r, up1_i) if fwd else c_times(dn1_r, dn1_i))
            gmat = (c_times(up_r, up_i) if fwd else c_times(dn_r, dn_i)).T
            bbt = jnp.where(re_half, bbr, bbi)
            rs.append(_dot3(bbt, gmat))
            er, ei = a_pow(level_e)
            base = 2 * n_levels * direction
            dec_ref[g, base:base + n_levels, :] = er
            dec_ref[g, base + n_levels:base + 2 * n_levels, :] = ei
        for i in range(chunk):
            s = SSM_H * i
            fwd_rows = jnp.where(wlane >= s, pltpu.roll(rs[0], s, 1) if s else rs[0], 0.0)
            e = (s + SSM_H) % w
            bwd_rows = jnp.where(wlane < s + SSM_H, pltpu.roll(rs[1], e, 1) if e else rs[1], 0.0)
            t_ref[g, s:s + SSM_H, :] = (fwd_rows + bwd_rows).astype(BF16)
        bz_ref[g] = jnp.concatenate(zs, axis=1).astype(BF16)
        cx_ref[g] = jnp.concatenate([cs[0].T, cs[1].T], axis=0).astype(BF16)


def _s5_prep(lam_re, lam_im, log_dt, b_re, b_im, c_re, c_im, n_levels):
    n_dir, n_grp, n_state = lam_re.shape
    assert n_dir == 2 and n_state == SSM_P and b_re.shape[-1] == SSM_H
    w = CHUNK * SSM_H
    gb = GROUPS_PER_SLAB
    assert n_grp % gb == 0 and n_levels % SUBLANES == 0

    def twice(v):
        return jnp.concatenate([v, v], axis=-1).astype(F32)

    lr = twice(lam_re)[:, :, None, :]
    li = twice(lam_im)[:, :, None, :]
    ldt = jnp.broadcast_to(log_dt.astype(F32)[:, :, None, None], (n_dir, n_grp, 1, LANES))
    bre, bim = twice(jnp.swapaxes(b_re, 2, 3)), twice(jnp.swapaxes(b_im, 2, 3))
    cre, cim = twice(c_re), twice(c_im)
    n_dec = 2 * 2 * n_levels
    row = lambda r: pl.BlockSpec((n_dir, gb, r, LANES), lambda i: (0, i, 0, 0))
    return pl.pallas_call(
        functools.partial(_s5_prep_kernel, n_levels),
        grid=(n_grp // gb,),
        in_specs=[row(1), row(1), row(1), row(SSM_H), row(SSM_H), row(SSM_H), row(SSM_H)],
        out_specs=[pl.BlockSpec((gb, w, w), lambda i: (i, 0, 0)),
                   pl.BlockSpec((gb, w, 2 * LANES), lambda i: (i, 0, 0)),
                   pl.BlockSpec((gb, 2 * LANES, w), lambda i: (i, 0, 0)),
                   pl.BlockSpec((gb, n_dec, LANES), lambda i: (i, 0, 0))],
        out_shape=[jax.ShapeDtypeStruct((n_grp, w, w), BF16),
                   jax.ShapeDtypeStruct((n_grp, w, 2 * LANES), BF16),
                   jax.ShapeDtypeStruct((n_grp, 2 * LANES, w), BF16),
                   jax.ShapeDtypeStruct((n_grp, n_dec, LANES), F32)],
        compiler_params=pltpu.CompilerParams(
            dimension_semantics=("arbitrary",), vmem_limit_bytes=VMEM_LIMIT),
        name="s5_prep",
    )(lr, li, ldt, bre, bim, cre, cim)


def _chunk_scan(zr, zi, a_re, a_im, reverse):
    n_rows = zr.shape[0]
    rc = lax.broadcasted_iota(jnp.int32, (n_rows, 1), 0)

    def shifted(v, dist):
        if reverse:
            return jnp.where(rc < n_rows - dist, pltpu.roll(v, n_rows - dist, 0), 0.0)
        return jnp.where(rc >= dist, pltpu.roll(v, dist, 0), 0.0)

    sr, si = shifted(zr, 1), shifted(zi, 1)
    for k, (ar, ai) in enumerate(zip(a_re, a_im)):
        if (1 << k) >= n_rows:
            break
        hr, hi = shifted(sr, 1 << k), shifted(si, 1 << k)
        sr, si = sr + (hr * ar - hi * ai), si + (hr * ai + hi * ar)
    return sr, si


def _slab_permutation():
    n = GROUPS_PER_SLAB
    idx = np.arange(n * LANES)
    a, b, h = idx // LANES, (idx % LANES) // SSM_H, idx % SSM_H
    perm = np.zeros((n * LANES, n * LANES), np.float32)
    perm[idx, b * LANES + a * SSM_H + h] = 1.0
    return jnp.asarray(perm, BF16)


def _unit_transpose(vs):
    vs = list(vs)
    n = len(vs)
    assert n * SSM_H == LANES
    unit = lax.broadcasted_iota(jnp.int32, (1, LANES), 1) // SSM_H
    d = n // 2
    while d:
        upper = (unit & d) != 0
        for i in range(n):
            if i & d:
                continue
            a, b = vs[i], vs[i + d]
            vs[i] = jnp.where(upper, pltpu.roll(b, d * SSM_H, 1), a)
            vs[i + d] = jnp.where(upper, b, pltpu.roll(a, LANES - d * SSM_H, 1))
        d //= 2
    return vs


def _s5_core_kernel(n_levels, h_ref, perm_ref, t_ref, bz_ref, cx_ref, dec_ref, y_ref, u_ref, yg_ref):
    n_grp = GROUPS_PER_SLAB
    perm = perm_ref[...]
    for o in range(CHUNK // n_grp):
        v = jnp.concatenate([h_ref[n_grp * o + l8] for l8 in range(n_grp)], axis=1)
        uv = _dot(v.astype(BF16), perm).astype(BF16)
        for g in range(n_grp):
            u_ref[g, :, LANES * o:LANES * (o + 1)] = uv[:, LANES * g:LANES * (g + 1)]
    first_half = lax.broadcasted_iota(jnp.int32, (1, LANES), 1) < SSM_P

    def swap_halves(v):
        return pltpu.roll(v, SSM_P, 1)

    for g0 in range(0, n_grp, 2):
        z = [_dot(u_ref[g], bz_ref[g]) for g in (g0, g0 + 1)]
        xs = [[], []]
        for direction in range(2):
            za, zb = (v[:, LANES * direction:LANES * (direction + 1)] for v in z)
            zr = jnp.where(first_half, za, swap_halves(zb))
            zi = jnp.where(first_half, swap_halves(za), zb)
            base = 2 * n_levels * direction
            a_re = [jnp.where(first_half, dec_ref[g0, r:r + 1, :], dec_ref[g0 + 1, r:r + 1, :])
                    for r in range(base, base + n_levels)]
            a_im = [jnp.where(first_half, dec_ref[g0, r:r + 1, :], dec_ref[g0 + 1, r:r + 1, :])
                    for r in range(base + n_levels, base + 2 * n_levels)]
            xr, xi = _chunk_scan(zr, zi, a_re, a_im, direction == 1)
            xs[0].append(jnp.where(first_half, xr, swap_halves(xi)))
            xs[1].append(jnp.where(first_half, swap_halves(xr), xi))
        for k, g in enumerate((g0, g0 + 1)):
            y = _dot(u_ref[g], t_ref[g])
            y = y + _dot(xs[k][0].astype(BF16), cx_ref[g, :LANES, :])
            y = y + _dot(xs[k][1].astype(BF16), cx_ref[g, LANES:, :])
            yg_ref[g] = y
    for o in range(CHUNK // n_grp):
        ys = _unit_transpose([yg_ref[g, :, LANES * o:LANES * (o + 1)] for g in range(n_grp)])
        for l8 in range(n_grp):
            y_ref[n_grp * o + l8] = ys[l8]


def _s5_mix(h, lam_re, lam_im, log_dt, b_re, b_im, c_re, c_im):
    bsz, _, rows, d = h.shape
    w = CHUNK * SSM_H
    gb = GROUPS_PER_SLAB
    assert h.shape[1] == CHUNK and CHUNK % gb == 0 and d % LANES == 0
    n_levels = -(-max(1, (rows - 1).bit_length()) // SUBLANES) * SUBLANES
    t_mat, bz, cx, dec = _s5_prep(lam_re, lam_im, log_dt, b_re, b_im, c_re, c_im, n_levels)
    perm = _slab_permutation()
    slab = pl.BlockSpec((None, CHUNK, rows, LANES), lambda s, b: (b, 0, 0, s))
    grp = lambda r, c: pl.BlockSpec((gb, r, c), lambda s, b: (s, 0, 0))
    return pl.pallas_call(
        functools.partial(_s5_core_kernel, n_levels),
        grid=(d // LANES, bsz),
        in_specs=[slab, _const_spec(perm.shape), grp(w, w), grp(w, 2 * LANES), grp(2 * LANES, w),
                  grp(dec.shape[1], LANES)],
        out_specs=slab,
        out_shape=jax.ShapeDtypeStruct(h.shape, F32),
        scratch_shapes=[pltpu.VMEM((gb, rows, w), BF16), pltpu.VMEM((gb, rows, w), F32)],
        compiler_params=pltpu.CompilerParams(
            dimension_semantics=("arbitrary", "arbitrary"), vmem_limit_bytes=VMEM_LIMIT),
        name="s5_core",
    )(h, perm, t_mat, bz, cx, dec)


def kernel(x, c, mod_w, mod_b, norm_g, ffn_w_in, ffn_w_out, pool_w, pool_b, pool_scale,
           ssm_lam_re, ssm_lam_im, ssm_log_dt, ssm_b_re, ssm_b_im, ssm_c_re, ssm_c_im,
           ssm_d, glu_w, glu_b, final_g):
    bsz, seq_len, d = x.shape
    depth = mod_w.shape[0]
    n_mixers = 2
    tm = min(512, seq_len)
    mods = _modulation(c, mod_w, mod_b)
    xf = x.astype(F32).reshape(bsz * seq_len, d)
    for i in range(depth):
        j = i // n_mixers
        is_s5 = i % n_mixers == 1
        final = final_g if i == depth - 1 else None
        first = _layer_call(xf, mods[i], norm_g[i], ffn_w_in[i, 0], ffn_w_out[i, 0], seq_len, tm,
                            emit_h=is_s5)
        if is_s5:
            xf, h = first
            ys = _s5_mix(h, ssm_lam_re[j], ssm_lam_im[j], ssm_log_dt[j], ssm_b_re[j],
                         ssm_b_im[j], ssm_c_re[j], ssm_c_im[j])
            xf = _layer_call(xf, mods[i], norm_g[i], ffn_w_in[i, 1], ffn_w_out[i, 1], seq_len, tm,
                             mixer="s5", mixer_args=(ys, ssm_d[j], glu_w[j], glu_b[j]),
                             final_g=final)
        else:
            xf = _layer_call(first, mods[i], norm_g[i], ffn_w_in[i, 1], ffn_w_out[i, 1], seq_len, tm,
                             mixer="pool", mixer_args=(pool_w[j], pool_b[j], pool_scale[j]),
                             final_g=final)
    return xf.reshape(bsz, seq_len, d).astype(x.dtype)
```

```python
import functools
import math

import numpy as np

import jax
import jax.numpy as jnp
from jax import lax
from jax.experimental import pallas as pl
from jax.experimental.pallas import tpu as pltpu

F32 = jnp.float32
BF16 = jnp.bfloat16

EPS = 1e-6
N_SUB = 3
N_MOD = 3
POOL_WINDOWS = (2, 4, 8, 16)
POOL_HALO = 8
SSM_H = 16
SSM_P = 64
CHUNK = 32
CHUNK_PITCH = CHUNK + 4
FF_CHUNK = 256
LANES = 128
CAST_BLOCKS = 16
SUBLANES = 8
GROUPS_PER_SLAB = LANES // SSM_H
VMEM_LIMIT = 56 * 1024 * 1024


def _sigmoid(v):
    return 1.0 / (1.0 + jnp.exp(-v))


def _split3(a):
    hi = a.astype(BF16)
    lo = (a - hi.astype(F32)).astype(BF16)
    return hi, lo


def _dot(a, b):
    return jnp.dot(a, b, preferred_element_type=F32)


def _dot3(a, b):
    ah, al = _split3(a)
    bh, bl = _split3(b)
    return _dot(ah, bh) + (_dot(ah, bl) + _dot(al, bh))


def _mod_kernel(c_ref, w_ref, b_ref, o_ref):
    c = c_ref[...]
    cond = c * _sigmoid(c)
    o_ref[0] = _dot3(cond, w_ref[0]) + b_ref[0]


def _modulation(c, mod_w, mod_b):
    depth, d, n = mod_w.shape
    bsz = c.shape[0]
    rows = SUBLANES
    tn = 1024
    assert n % tn == 0 and bsz <= rows
    c_pad = jnp.zeros((rows, d), F32).at[:bsz].set(c.astype(F32))
    out = pl.pallas_call(
        _mod_kernel,
        grid=(depth, n // tn),
        in_specs=[
            pl.BlockSpec((rows, d), lambda i, j: (0, 0)),
            pl.BlockSpec((1, d, tn), lambda i, j: (i, 0, j)),
            pl.BlockSpec((1, 1, tn), lambda i, j: (i, 0, j)),
        ],
        out_specs=pl.BlockSpec((1, rows, tn), lambda i, j: (i, 0, j)),
        out_shape=jax.ShapeDtypeStruct((depth, rows, n), F32),
        compiler_params=pltpu.CompilerParams(
            dimension_semantics=("arbitrary", "arbitrary"), vmem_limit_bytes=VMEM_LIMIT),
        name="adaln_modulation",
    )(c_pad, mod_w.astype(F32), mod_b.astype(F32).reshape(depth, 1, n))
    return out[:, :bsz].reshape(depth, bsz, N_SUB * N_MOD, d)


def _rms(x, g):
    return x * lax.rsqrt(jnp.mean(x * x, axis=-1, keepdims=True) + EPS) * g


def _modnorm(x, g_ref, mod, sub):
    r = N_MOD * sub
    shift, scale = mod[r:r + 1], mod[r + 1:r + 2]
    return _rms(x, g_ref[sub:sub + 1, :]) * (1.0 + scale) + shift


def _swiglu_step(x, g_ref, mod, sub, wg_ref, wu_ref, wom_ref, wot_ref, act_ref):
    h = _modnorm(x, g_ref, mod, sub).astype(BF16)
    n_chunks = act_ref.shape[1] // FF_CHUNK
    for k in range(n_chunks):
        cols = slice(FF_CHUNK * k, FF_CHUNK * (k + 1))
        gate = _dot(h, wg_ref[:, cols])
        up = _dot(h, wu_ref[:, cols])
        act_ref[:, cols] = (gate * _sigmoid(gate) * up).astype(BF16)
    k_main = wom_ref.shape[0]
    out = _dot(act_ref[:, :k_main], wom_ref[...])
    if act_ref.shape[1] > k_main:
        out = out + _dot(act_ref[:, k_main:], wot_ref[...])
    r = N_MOD * sub + 2
    return x + (0.5 * mod[r:r + 1]) * out


def _gelu_tanh(v):
    return 0.5 * v * (1.0 + jnp.tanh(math.sqrt(2.0 / math.pi) * (v + 0.044715 * (v * v * v))))


def _pool_fill(x_ref, xp_ref, xn_ref, g_ref, mod, hs_ref, seq_len):
    tm = x_ref.shape[0]
    tiles_per_seq = seq_len // tm
    j = pl.program_id(0) % tiles_per_seq
    hp = _modnorm(xp_ref[...], g_ref, mod, 1)
    hn = _modnorm(xn_ref[...], g_ref, mod, 1)
    hs_ref[0:POOL_HALO, :] = jnp.where(j > 0, hp, 0.0)
    hs_ref[POOL_HALO:POOL_HALO + tm, :] = _modnorm(x_ref[...], g_ref, mod, 1)
    hs_ref[POOL_HALO + tm:, :] = jnp.where(j < tiles_per_seq - 1, hn, 0.0)


def _pool_update(x, row0, tm, mod, pw_ref, pb_ref, ps_ref, hs_ref, seq_len):
    ts, d = x.shape
    cg = d // len(POOL_WINDOWS)
    j = pl.program_id(0) % (seq_len // tm)
    t = j * tm + row0 + lax.broadcasted_iota(jnp.int32, (ts, 1), 0)
    n_ext = ts + 2 * POOL_HALO
    ys = []
    for gi, window in enumerate(POOL_WINDOWS):
        left = window // 2
        right = window - 1 - left
        cols = slice(gi * cg, (gi + 1) * cg)
        assert window & (window - 1) == 0 and right < POOL_HALO
        ext = hs_ref[row0:row0 + n_ext, cols]
        run = ext
        span = 1
        while span < window:
            run = run + pltpu.roll(run, span, 0)
            span *= 2
        if right:
            run = pltpu.roll(run, n_ext - right, 0)
        total = run[POOL_HALO:POOL_HALO + ts]
        lo = jnp.maximum(t - left, 0)
        hi = jnp.minimum(t + right, seq_len - 1)
        count = (hi - lo + 1).astype(F32)
        p = total / count - ext[POOL_HALO:POOL_HALO + ts]
        ys.append(_dot(p.astype(BF16), pw_ref[gi]))
    y = (jnp.concatenate(ys, axis=1) + pb_ref[...]) * ps_ref[...]
    return x + mod[N_MOD + 2:N_MOD + 3] * y


def _s5_tail_update(x, ys, g_ref, mod, dskip_ref, gw_ref, gb_ref):
    d = x.shape[1]
    h = _modnorm(x, g_ref, mod, 1)
    y = _gelu_tanh(dskip_ref[...] * h + ys)
    z = _dot(y.astype(BF16), gw_ref[...]) + gb_ref[...]
    m = z[:, :d] * _sigmoid(z[:, d:])
    return x + mod[N_MOD + 2:N_MOD + 3] * m


def _cast_weight_block(win_ref, wout_ref, wg_ref, wu_ref, wom_ref):
    d_ff = win_ref.shape[1] // 2
    w = win_ref[...]
    for src, dst in ((w[:, :d_ff], wg_ref), (w[:, d_ff:], wu_ref)):
        dst[:, :d_ff] = src.astype(BF16)
        if dst.shape[1] > d_ff:
            dst[:, d_ff:] = jnp.zeros((dst.shape[0], dst.shape[1] - d_ff), BF16)
    wom_ref[...] = wout_ref[...].astype(BF16)


def _layer_kernel(mixer, emit_h, final_norm, cast_next, seq_len, n_sub, *refs):
    refs = list(refs)
    x_ref, mod_ref, g_ref = refs[:3]
    del refs[:3]
    mod = mod_ref[0]
    if mixer == "pool":
        xp_ref, xn_ref, pw_ref, pb_ref, ps_ref = refs[:5]
        del refs[:5]
    elif mixer == "s5":
        ys_ref, dskip_ref, gw_ref, gb_ref = refs[:4]
        del refs[:4]
    wg_ref, wu_ref, wom_ref, wot_ref = refs[:4]
    del refs[:4]
    if final_norm:
        fg_ref = refs.pop(0)
    if cast_next:
        cast_in = refs[:2]
        del refs[:2]
    o_ref = refs.pop(0)
    if emit_h:
        h_ref = refs.pop(0)
    if cast_next:
        _cast_weight_block(*cast_in, *refs[:3])
        del refs[:3]
    act_ref = refs.pop(0)
    tm = x_ref.shape[0]
    ts = tm // n_sub
    chunks = ts // CHUNK
    if mixer == "pool":
        hs_ref = refs.pop(0)
        _pool_fill(x_ref, xp_ref, xn_ref, g_ref, mod, hs_ref, seq_len)
    elif mixer == "s5":
        rows_ref = refs.pop(0)
        n_slab = rows_ref.shape[0]
        for l in range(CHUNK):
            for s in range(n_slab):
                rows_ref[s, pl.ds(l, tm // CHUNK, stride=CHUNK_PITCH), :] = ys_ref[l, :, LANES * s:LANES * (s + 1)]
    if emit_h:
        hrows_ref = refs.pop(0)
    for si in range(n_sub):
        row0 = si * ts
        x = x_ref[row0:row0 + ts, :]
        if mixer == "pool":
            x = _pool_update(x, row0, tm, mod, pw_ref, pb_ref, ps_ref, hs_ref, seq_len)
        elif mixer == "s5":
            ys = jnp.concatenate(
                [jnp.concatenate([rows_ref[s, CHUNK_PITCH * c:CHUNK_PITCH * c + CHUNK, :]
                                  for c in range(si * chunks, (si + 1) * chunks)], axis=0)
                 for s in range(n_slab)], axis=1)
            x = _s5_tail_update(x, ys, g_ref, mod, dskip_ref, gw_ref, gb_ref)
        sub = 0 if mixer is None else 2
        x = _swiglu_step(x, g_ref, mod, sub, wg_ref, wu_ref, wom_ref, wot_ref, act_ref.at[si])
        if final_norm:
            x = _rms(x, fg_ref[...])
        o_ref[row0:row0 + ts, :] = x
        if emit_h:
            h = _modnorm(x, g_ref, mod, 1)
            for c in range(chunks):
                base = CHUNK_PITCH * (si * chunks + c)
                for s in range(hrows_ref.shape[0]):
                    hrows_ref[s, base:base + CHUNK, :] = h[CHUNK * c:CHUNK * (c + 1), LANES * s:LANES * (s + 1)]
    if emit_h:
        for l in range(CHUNK):
            for s in range(hrows_ref.shape[0]):
                h_ref[l, :, LANES * s:LANES * (s + 1)] = hrows_ref[s, pl.ds(l, tm // CHUNK, stride=CHUNK_PITCH), :]


def _const_spec(shape):
    nd = len(shape)
    return pl.BlockSpec(shape, lambda *_: (0,) * nd, pipeline_mode=pl.Buffered(1))


def _ffn_dims(ffn_w_in):
    d_ff = ffn_w_in.shape[-1] // 2
    return d_ff, d_ff + -d_ff % FF_CHUNK, d_ff // FF_CHUNK * FF_CHUNK


def _cast_io(ffn_w_in, ffn_w_out, layer, which, steps):
    d = ffn_w_in.shape[2]
    d_ff, ffp, k_main = _ffn_dims(ffn_w_in)
    n = 1
    while n < CAST_BLOCKS and steps % (2 * n) == 0:
        n *= 2
    assert d % (n * 2 * SUBLANES) == 0 and k_main % (n * 2 * SUBLANES) == 0
    blk = lambda i: i * n // steps
    in_specs = [pl.BlockSpec((None, None, d // n, 2 * d_ff), lambda i: (layer, which, blk(i), 0)),
                pl.BlockSpec((None, None, k_main // n, d), lambda i: (layer, which, blk(i), 0))]
    out_specs = [pl.BlockSpec((d // n, ffp), lambda i: (blk(i), 0)),
                 pl.BlockSpec((d // n, ffp), lambda i: (blk(i), 0)),
                 pl.BlockSpec((k_main // n, d), lambda i: (blk(i), 0))]
    out_shape = [jax.ShapeDtypeStruct((d, ffp), BF16), jax.ShapeDtypeStruct((d, ffp), BF16),
                 jax.ShapeDtypeStruct((k_main, d), BF16)]
    return [ffn_w_in.astype(F32), ffn_w_out.astype(F32)], in_specs, out_specs, out_shape


def _w_out_tail(ffn_w_out, layer, which, k_main, ffp):
    tail = ffn_w_out[layer, which, k_main:, :].astype(BF16)
    return jnp.pad(tail, ((0, ffp - k_main - tail.shape[0]), (0, 0)))


def _cast_first_weights(ffn_w_in, ffn_w_out, layer, which):
    args, in_specs, out_specs, out_shape = _cast_io(ffn_w_in, ffn_w_out, layer, which, CAST_BLOCKS)
    return pl.pallas_call(
        _cast_weight_block,
        grid=(CAST_BLOCKS,),
        in_specs=in_specs,
        out_specs=out_specs,
        out_shape=out_shape,
        compiler_params=pltpu.CompilerParams(
            dimension_semantics=("arbitrary",), vmem_limit_bytes=VMEM_LIMIT),
        name="weight_cast",
    )(*args)


def _tile_plan(seq_len, mixer, emit_h):
    wide = mixer != "s5" and not emit_h
    tm = min(1024 if wide else 512, seq_len)
    return tm, max(1, tm // 512)


def _layer_call(x, mod, norm_g, weights, seq_len, *, mixer=None, mixer_args=(),
                emit_h=False, final_g=None, next_ffn=None):
    n_tok, d = x.shape
    tm, n_sub = _tile_plan(seq_len, mixer, emit_h)
    assert seq_len % tm == 0 and tm % (n_sub * CHUNK * SUBLANES) == 0
    tiles_per_seq = seq_len // tm
    wg, wu, wom, wot = weights
    ffp = wg.shape[1]
    row_spec = pl.BlockSpec((tm, d), lambda i: (i, 0))
    chunk_spec = pl.BlockSpec((None, CHUNK, tm // CHUNK, d),
                              lambda i: (i // tiles_per_seq, 0, i % tiles_per_seq, 0))
    args = [x, mod, norm_g.astype(F32)]
    specs = [row_spec,
             pl.BlockSpec((1, N_SUB * N_MOD, d), lambda i: (i // tiles_per_seq, 0, 0)),
             _const_spec((N_SUB, d))]
    scratch = [pltpu.VMEM((n_sub, tm // n_sub, ffp), BF16)]
    if mixer == "pool":
        pool_w, pool_b, pool_scale = mixer_args
        hb = tm // POOL_HALO
        last = n_tok // POOL_HALO - 1
        args += [x, x, pool_w.astype(BF16), pool_b.astype(F32).reshape(1, d),
                 pool_scale.astype(F32).reshape(1, d)]
        specs += [pl.BlockSpec((POOL_HALO, d), lambda i: (jnp.maximum(i * hb - 1, 0), 0)),
                  pl.BlockSpec((POOL_HALO, d), lambda i: (jnp.minimum((i + 1) * hb, last), 0)),
                  _const_spec(pool_w.shape), _const_spec((1, d)), _const_spec((1, d))]
        scratch.append(pltpu.VMEM((tm + 2 * POOL_HALO, d), F32))
    elif mixer == "s5":
        ys, d_skip, glu_w, glu_b = mixer_args
        args += [ys, d_skip.astype(F32).reshape(1, d), glu_w.astype(BF16),
                 glu_b.astype(F32).reshape(1, 2 * d)]
        specs += [chunk_spec, _const_spec((1, d)), _const_spec((d, 2 * d)), _const_spec((1, 2 * d))]
        scratch.append(pltpu.VMEM((d // LANES, tm // CHUNK * CHUNK_PITCH, LANES), F32))
    args += [wg, wu, wom, wot]
    specs += [_const_spec(w.shape) for w in (wg, wu, wom, wot)]
    if final_g is not None:
        args.append(final_g.astype(F32).reshape(1, d))
        specs.append(_const_spec((1, d)))
    if next_ffn is not None:
        cast_args, cast_in, cast_out, cast_shape = _cast_io(*next_ffn, n_tok // tm)
        args += cast_args
        specs += cast_in
    out_shape = [jax.ShapeDtypeStruct((n_tok, d), F32)]
    out_specs = [row_spec]
    if emit_h:
        out_shape.append(jax.ShapeDtypeStruct((n_tok // seq_len, CHUNK, seq_len // CHUNK, d), F32))
        out_specs.append(chunk_spec)
        scratch.append(pltpu.VMEM((d // LANES, tm // CHUNK * CHUNK_PITCH, LANES), F32))
    if next_ffn is not None:
        out_shape += cast_shape
        out_specs += cast_out
    outs = pl.pallas_call(
        functools.partial(_layer_kernel, mixer, emit_h, final_g is not None, next_ffn is not None,
                          seq_len, n_sub),
        grid=(n_tok // tm,),
        in_specs=specs,
        out_specs=out_specs,
        out_shape=out_shape,
        scratch_shapes=scratch,
        compiler_params=pltpu.CompilerParams(
            dimension_semantics=("arbitrary",), vmem_limit_bytes=VMEM_LIMIT),
        name="layer_" + (mixer or "ffn"),
    )(*args)
    n_act = 2 if emit_h else 1
    acts = tuple(outs[:n_act]) if emit_h else outs[0]
    if next_ffn is None:
        return acts
    _, ffp_next, k_main = _ffn_dims(next_ffn[0])
    return acts, (*outs[n_act:], _w_out_tail(next_ffn[1], next_ffn[2], next_ffn[3], k_main, ffp_next))


def _cmul(ar, ai, br, bi):
    return ar * br - ai * bi, ar * bi + ai * br


def _s5_prep_kernel(n_levels, lr_ref, li_ref, ldt_ref, bre_ref, bim_ref, cre_ref, cim_ref,
                    t_ref, bz_ref, cx_ref, dec_ref):
    gb = lr_ref.shape[1]
    w = t_ref.shape[1]
    chunk = w // SSM_H
    n_dbl = chunk.bit_length() - 1
    assert chunk == 1 << n_dbl
    lane = lax.broadcasted_iota(jnp.int32, (1, LANES), 1)
    re_half = lane < SSM_P
    wlane = lax.broadcasted_iota(jnp.int32, (1, w), 1)
    level_e = (chunk << lax.broadcasted_iota(jnp.int32, (n_levels, 1), 0)).astype(F32)

    def tile_rows(v):
        return jnp.broadcast_to(v[None], (chunk, SSM_H, LANES)).reshape(w, LANES)

    for g in range(gb):
        rs, zs, cs = [], [], []
        for direction in range(2):
            lr, li = lr_ref[direction, g], li_ref[direction, g]
            dt = jnp.exp(ldt_ref[direction, g])
            lrdt, lidt = lr * dt, li * dt

            def a_pow(e):
                mag = jnp.exp(lrdt * e)
                return mag * jnp.cos(lidt * e), mag * jnp.sin(lidt * e)

            ar, ai = a_pow(1.0)
            den = lr * lr + li * li
            fr = ((ar - 1.0) * lr + ai * li) / den
            fi = (ai * lr - (ar - 1.0) * li) / den
            bre, bim = bre_ref[direction, g], bim_ref[direction, g]
            bbr, bbi = _cmul(fr, fi, bre, bim)
            cre, cim = tile_rows(cre_ref[direction, g]), tile_rows(cim_ref[direction, g])
            bbr_t, bbi_t = tile_rows(bbr), tile_rows(bbi)

            one = jnp.ones((SSM_H, LANES), F32)
            up_r, up_i, dn_r, dn_i = one, 0.0 * one, one, 0.0 * one
            for k in range(n_dbl):
                sr, si = a_pow(float(1 << k))
                nr, ni = _cmul(up_r, up_i, sr, si)
                up_r, up_i = jnp.concatenate([up_r, nr], axis=0), jnp.concatenate([up_i, ni], axis=0)
                nr, ni = _cmul(dn_r, dn_i, sr, si)
                dn_r, dn_i = jnp.concatenate([nr, dn_r], axis=0), jnp.concatenate([ni, dn_i], axis=0)
            up1_r, up1_i = _cmul(up_r, up_i, ar, ai)
            dn1_r, dn1_i = _cmul(dn_r, dn_i, ar, ai)

            def c_times(er, ei):
                pr, pi = _cmul(cre, cim, er, ei)
                return jnp.where(re_half, pr, -pi)

            def b_times(er, ei):
                pr, pi = _cmul(bbr_t, bbi_t, er, ei)
                return jnp.where(re_half, pr, pi)

            fwd = direction == 0
            zs.append(b_times(dn_r, dn_i) if fwd else b_times(up_r, up_i))
            cs.append(c_times(up1_r, up1_i) if fwd else c_times(dn1_r, dn1_i))
            gmat = (c_times(up_r, up_i) if fwd else c_times(dn_r, dn_i)).T
            bbt = jnp.where(re_half, bbr, bbi)
            rs.append(_dot3(bbt, gmat))
            er, ei = a_pow(level_e)
            base = 2 * n_levels * direction
            dec_ref[g, base:base + n_levels, :] = er
            dec_ref[g, base + n_levels:base + 2 * n_levels, :] = ei
        for i in range(chunk):
            s = SSM_H * i
            fwd_rows = jnp.where(wlane >= s, pltpu.roll(rs[0], s, 1) if s else rs[0], 0.0)
            e = (s + SSM_H) % w
            bwd_rows = jnp.where(wlane < s + SSM_H, pltpu.roll(rs[1], e, 1) if e else rs[1], 0.0)
            t_ref[g, s:s + SSM_H, :] = (fwd_rows + bwd_rows).astype(BF16)
        bz_ref[g] = jnp.concatenate(zs, axis=1).astype(BF16)
        cx_ref[g] = jnp.concatenate([cs[0].T, cs[1].T], axis=0).astype(BF16)


def _s5_prep(lam_re, lam_im, log_dt, b_re, b_im, c_re, c_im, n_levels):
    n_dir, n_grp, n_state = lam_re.shape
    assert n_dir == 2 and n_state == SSM_P and b_re.shape[-1] == SSM_H
    w = CHUNK * SSM_H
    gb = GROUPS_PER_SLAB
    assert n_grp % gb == 0 and n_levels % SUBLANES == 0

    def twice(v):
        return jnp.concatenate([v, v], axis=-1).astype(F32)

    lr = twice(lam_re)[:, :, None, :]
    li = twice(lam_im)[:, :, None, :]
    ldt = jnp.broadcast_to(log_dt.astype(F32)[:, :, None, None], (n_dir, n_grp, 1, LANES))
    bre, bim = twice(jnp.swapaxes(b_re, 2, 3)), twice(jnp.swapaxes(b_im, 2, 3))
    cre, cim = twice(c_re), twice(c_im)
    n_dec = 2 * 2 * n_levels
    row = lambda r: pl.BlockSpec((n_dir, gb, r, LANES), lambda i: (0, i, 0, 0))
    return pl.pallas_call(
        functools.partial(_s5_prep_kernel, n_levels),
        grid=(n_grp // gb,),
        in_specs=[row(1), row(1), row(1), row(SSM_H), row(SSM_H), row(SSM_H), row(SSM_H)],
        out_specs=[pl.BlockSpec((gb, w, w), lambda i: (i, 0, 0)),
                   pl.BlockSpec((gb, w, 2 * LANES), lambda i: (i, 0, 0)),
                   pl.BlockSpec((gb, 2 * LANES, w), lambda i: (i, 0, 0)),
                   pl.BlockSpec((gb, n_dec, LANES), lambda i: (i, 0, 0))],
        out_shape=[jax.ShapeDtypeStruct((n_grp, w, w), BF16),
                   jax.ShapeDtypeStruct((n_grp, w, 2 * LANES), BF16),
                   jax.ShapeDtypeStruct((n_grp, 2 * LANES, w), BF16),
                   jax.ShapeDtypeStruct((n_grp, n_dec, LANES), F32)],
        compiler_params=pltpu.CompilerParams(
            dimension_semantics=("arbitrary",), vmem_limit_bytes=VMEM_LIMIT),
        name="s5_prep",
    )(lr, li, ldt, bre, bim, cre, cim)


def _chunk_scan(zr, zi, a_re, a_im, reverse):
    n_rows = zr.shape[0]
    rc = lax.broadcasted_iota(jnp.int32, (n_rows, 1), 0)

    def shifted(v, dist):
        if reverse:
            return jnp.where(rc < n_rows - dist, pltpu.roll(v, n_rows - dist, 0), 0.0)
        return jnp.where(rc >= dist, pltpu.roll(v, dist, 0), 0.0)

    sr, si = shifted(zr, 1), shifted(zi, 1)
    for k, (ar, ai) in enumerate(zip(a_re, a_im)):
        if (1 << k) >= n_rows:
            break
        hr, hi = shifted(sr, 1 << k), shifted(si, 1 << k)
        sr, si = sr + (hr * ar - hi * ai), si + (hr * ai + hi * ar)
    return sr, si


def _slab_permutation():
    n = GROUPS_PER_SLAB
    idx = np.arange(n * LANES)
    a, b, h = idx // LANES, (idx % LANES) // SSM_H, idx % SSM_H
    perm = np.zeros((n * LANES, n * LANES), np.float32)
    perm[idx, b * LANES + a * SSM_H + h] = 1.0
    return jnp.asarray(perm, BF16)


def _unit_transpose(vs):
    vs = list(vs)
    n = len(vs)
    assert n * SSM_H == LANES
    unit = lax.broadcasted_iota(jnp.int32, (1, LANES), 1) // SSM_H
    d = n // 2
    while d:
        upper = (unit & d) != 0
        for i in range(n):
            if i & d:
                continue
            a, b = vs[i], vs[i + d]
            vs[i] = jnp.where(upper, pltpu.roll(b, d * SSM_H, 1), a)
            vs[i + d] = jnp.where(upper, b, pltpu.roll(a, LANES - d * SSM_H, 1))
        d //= 2
    return vs


def _s5_core_kernel(n_levels, h_ref, perm_ref, t_ref, bz_ref, cx_ref, dec_ref, y_ref, u_ref, yg_ref):
    n_grp = GROUPS_PER_SLAB
    perm = perm_ref[...]
    for o in range(CHUNK // n_grp):
        v = jnp.concatenate([h_ref[n_grp * o + l8] for l8 in range(n_grp)], axis=1)
        uv = _dot(v.astype(BF16), perm).astype(BF16)
        for g in range(n_grp):
            u_ref[g, :, LANES * o:LANES * (o + 1)] = uv[:, LANES * g:LANES * (g + 1)]
    first_half = lax.broadcasted_iota(jnp.int32, (1, LANES), 1) < SSM_P

    def swap_halves(v):
        return pltpu.roll(v, SSM_P, 1)

    for g0 in range(0, n_grp, 2):
        z = [_dot(u_ref[g], bz_ref[g]) for g in (g0, g0 + 1)]
        xs = [[], []]
        for direction in range(2):
            za, zb = (v[:, LANES * direction:LANES * (direction + 1)] for v in z)
            zr = jnp.where(first_half, za, swap_halves(zb))
            zi = jnp.where(first_half, swap_halves(za), zb)
            base = 2 * n_levels * direction
            a_re = [jnp.where(first_half, dec_ref[g0, r:r + 1, :], dec_ref[g0 + 1, r:r + 1, :])
                    for r in range(base, base + n_levels)]
            a_im = [jnp.where(first_half, dec_ref[g0, r:r + 1, :], dec_ref[g0 + 1, r:r + 1, :])
                    for r in range(base + n_levels, base + 2 * n_levels)]
            xr, xi = _chunk_scan(zr, zi, a_re, a_im, direction == 1)
            xs[0].append(jnp.where(first_half, xr, swap_halves(xi)))
            xs[1].append(jnp.where(first_half, swap_halves(xr), xi))
        for k, g in enumerate((g0, g0 + 1)):
            y = _dot(u_ref[g], t_ref[g])
            y = y + _dot(xs[k][0].astype(BF16), cx_ref[g, :LANES, :])
            y = y + _dot(xs[k][1].astype(BF16), cx_ref[g, LANES:, :])
            yg_ref[g] = y
    for o in range(CHUNK // n_grp):
        ys = _unit_transpose([yg_ref[g, :, LANES * o:LANES * (o + 1)] for g in range(n_grp)])
        for l8 in range(n_grp):
            y_ref[n_grp * o + l8] = ys[l8]


def _s5_mix(h, lam_re, lam_im, log_dt, b_re, b_im, c_re, c_im):
    bsz, _, rows, d = h.shape
    w = CHUNK * SSM_H
    gb = GROUPS_PER_SLAB
    assert h.shape[1] == CHUNK and CHUNK % gb == 0 and d % LANES == 0
    n_levels = -(-max(1, (rows - 1).bit_length()) // SUBLANES) * SUBLANES
    t_mat, bz, cx, dec = _s5_prep(lam_re, lam_im, log_dt, b_re, b_im, c_re, c_im, n_levels)
    perm = _slab_permutation()
    slab = pl.BlockSpec((None, CHUNK, rows, LANES), lambda s, b: (b, 0, 0, s))
    grp = lambda r, c: pl.BlockSpec((gb, r, c), lambda s, b: (s, 0, 0))
    return pl.pallas_call(
        functools.partial(_s5_core_kernel, n_levels),
        grid=(d // LANES, bsz),
        in_specs=[slab, _const_spec(perm.shape), grp(w, w), grp(w, 2 * LANES), grp(2 * LANES, w),
                  grp(dec.shape[1], LANES)],
        out_specs=slab,
        out_shape=jax.ShapeDtypeStruct(h.shape, F32),
        scratch_shapes=[pltpu.VMEM((gb, rows, w), BF16), pltpu.VMEM((gb, rows, w), F32)],
        compiler_params=pltpu.CompilerParams(
            dimension_semantics=("arbitrary", "arbitrary"), vmem_limit_bytes=VMEM_LIMIT),
        name="s5_core",
    )(h, perm, t_mat, bz, cx, dec)


def kernel(x, c, mod_w, mod_b, norm_g, ffn_w_in, ffn_w_out, pool_w, pool_b, pool_scale,
           ssm_lam_re, ssm_lam_im, ssm_log_dt, ssm_b_re, ssm_b_im, ssm_c_re, ssm_c_im,
           ssm_d, glu_w, glu_b, final_g):
    bsz, seq_len, d = x.shape
    depth = mod_w.shape[0]
    n_mixers = 2
    mods = _modulation(c, mod_w, mod_b)
    xf = x.astype(F32).reshape(bsz * seq_len, d)
    _, ffp, k_main = _ffn_dims(ffn_w_in)
    weights = (*_cast_first_weights(ffn_w_in, ffn_w_out, 0, 0), _w_out_tail(ffn_w_out, 0, 0, k_main, ffp))
    for i in range(depth):
        j = i // n_mixers
        is_s5 = i % n_mixers == 1
        last = i == depth - 1
        acts, weights = _layer_call(xf, mods[i], norm_g[i], weights, seq_len, emit_h=is_s5,
                                    next_ffn=(ffn_w_in, ffn_w_out, i, 1))
        if is_s5:
            xf, h = acts
            ys = _s5_mix(h, ssm_lam_re[j], ssm_lam_im[j], ssm_log_dt[j], ssm_b_re[j],
                         ssm_b_im[j], ssm_c_re[j], ssm_c_im[j])
            mixer, mixer_args = "s5", (ys, ssm_d[j], glu_w[j], glu_b[j])
        else:
            xf = acts
            mixer, mixer_args = "pool", (pool_w[j], pool_b[j], pool_scale[j])
        res = _layer_call(xf, mods[i], norm_g[i], weights, seq_len, mixer=mixer, mixer_args=mixer_args,
                          final_g=final_g if last else None,
                          next_ffn=None if last else (ffn_w_in, ffn_w_out, i + 1, 0))
        xf, weights = (res, None) if last else res
    return xf.reshape(bsz, seq_len, d).astype(x.dtype)
```

```python
import functools
import math

import numpy as np

import jax
import jax.numpy as jnp
from jax import lax
from jax.experimental import pallas as pl
from jax.experimental.pallas import tpu as pltpu

F32 = jnp.float32
BF16 = jnp.bfloat16

EPS = 1e-6
N_SUB = 3
N_MOD = 3
POOL_WINDOWS = (2, 4, 8, 16)
POOL_HALO = 8
SSM_H = 16
SSM_P = 64
CHUNK = 32
CHUNK_PITCH = CHUNK + 4
FF_CHUNK = 256
GLU_CHUNK = 512
MATMUL_PHASE = "matmul phase"
LANES = 128
CAST_BLOCKS = 16
SUBLANES = 8
GROUPS_PER_SLAB = LANES // SSM_H
VMEM_LIMIT = 56 * 1024 * 1024


def _sigmoid(v):
    return 1.0 / (1.0 + jnp.exp(-v))


def _split3(a):
    hi = a.astype(BF16)
    lo = (a - hi.astype(F32)).astype(BF16)
    return hi, lo


def _dot(a, b):
    return jnp.dot(a, b, preferred_element_type=F32)


def _dot3(a, b):
    ah, al = _split3(a)
    bh, bl = _split3(b)
    return _dot(ah, bh) + (_dot(ah, bl) + _dot(al, bh))


def _mod_kernel(c_ref, w_ref, b_ref, o_ref):
    c = c_ref[...]
    cond = c * _sigmoid(c)
    o_ref[0] = _dot3(cond, w_ref[0]) + b_ref[0]


def _modulation(c, mod_w, mod_b):
    depth, d, n = mod_w.shape
    bsz = c.shape[0]
    rows = SUBLANES
    tn = 1024
    assert n % tn == 0 and bsz <= rows
    c_pad = jnp.zeros((rows, d), F32).at[:bsz].set(c.astype(F32))
    out = pl.pallas_call(
        _mod_kernel,
        grid=(depth, n // tn),
        in_specs=[
            pl.BlockSpec((rows, d), lambda i, j: (0, 0)),
            pl.BlockSpec((1, d, tn), lambda i, j: (i, 0, j)),
            pl.BlockSpec((1, 1, tn), lambda i, j: (i, 0, j)),
        ],
        out_specs=pl.BlockSpec((1, rows, tn), lambda i, j: (i, 0, j)),
        out_shape=jax.ShapeDtypeStruct((depth, rows, n), F32),
        compiler_params=pltpu.CompilerParams(
            dimension_semantics=("arbitrary", "arbitrary"), vmem_limit_bytes=VMEM_LIMIT),
        name="adaln_modulation",
    )(c_pad, mod_w.astype(F32), mod_b.astype(F32).reshape(depth, 1, n))
    return out[:, :bsz].reshape(depth, bsz, N_SUB * N_MOD, d)


def _rms(x, g):
    return x * lax.rsqrt(jnp.mean(x * x, axis=-1, keepdims=True) + EPS) * g


def _modnorm(x, g_ref, mod, sub):
    r = N_MOD * sub
    shift, scale = mod[r:r + 1], mod[r + 1:r + 2]
    return _rms(x, g_ref[sub:sub + 1, :]) * (1.0 + scale) + shift


def _swiglu_step(x, g_ref, mod, sub, wg_ref, wu_ref, wom_ref, wot_ref, act_ref):
    h = _modnorm(x, g_ref, mod, sub).astype(BF16)
    yield MATMUL_PHASE
    n_chunks = act_ref.shape[1] // FF_CHUNK
    for k in range(n_chunks):
        cols = slice(FF_CHUNK * k, FF_CHUNK * (k + 1))
        gate = _dot(h, wg_ref[:, cols])
        up = _dot(h, wu_ref[:, cols])
        act_ref[:, cols] = (gate * _sigmoid(gate) * up).astype(BF16)
        yield
    k_main = wom_ref.shape[0]
    out = _dot(act_ref[:, :k_main], wom_ref[...])
    if act_ref.shape[1] > k_main:
        out = out + _dot(act_ref[:, k_main:], wot_ref[...])
    r = N_MOD * sub + 2
    return x + (0.5 * mod[r:r + 1]) * out


def _gelu_tanh(v):
    return 0.5 * v * (1.0 + jnp.tanh(math.sqrt(2.0 / math.pi) * (v + 0.044715 * (v * v * v))))


def _pool_fill(x_ref, xp_ref, xn_ref, g_ref, mod, hs_ref, seq_len):
    tm = x_ref.shape[0]
    tiles_per_seq = seq_len // tm
    j = pl.program_id(0) % tiles_per_seq
    hp = _modnorm(xp_ref[...], g_ref, mod, 1)
    hn = _modnorm(xn_ref[...], g_ref, mod, 1)
    hs_ref[0:POOL_HALO, :] = jnp.where(j > 0, hp, 0.0)
    hs_ref[POOL_HALO:POOL_HALO + tm, :] = _modnorm(x_ref[...], g_ref, mod, 1)
    hs_ref[POOL_HALO + tm:, :] = jnp.where(j < tiles_per_seq - 1, hn, 0.0)


def _pool_update(x, row0, tm, mod, pw_ref, pb_ref, ps_ref, hs_ref, seq_len):
    ts, d = x.shape
    cg = d // len(POOL_WINDOWS)
    j = pl.program_id(0) % (seq_len // tm)
    t = j * tm + row0 + lax.broadcasted_iota(jnp.int32, (ts, 1), 0)
    n_ext = ts + 2 * POOL_HALO
    ys = []
    for gi, window in enumerate(POOL_WINDOWS):
        left = window // 2
        right = window - 1 - left
        cols = slice(gi * cg, (gi + 1) * cg)
        assert window & (window - 1) == 0 and right < POOL_HALO
        ext = hs_ref[row0:row0 + n_ext, cols]
        run = ext
        span = 1
        while span < window:
            run = run + pltpu.roll(run, span, 0)
            span *= 2
        if right:
            run = pltpu.roll(run, n_ext - right, 0)
        total = run[POOL_HALO:POOL_HALO + ts]
        lo = jnp.maximum(t - left, 0)
        hi = jnp.minimum(t + right, seq_len - 1)
        count = (hi - lo + 1).astype(F32)
        p = total / count - ext[POOL_HALO:POOL_HALO + ts]
        ys.append(_dot(p.astype(BF16), pw_ref[gi]))
        yield
    y = (jnp.concatenate(ys, axis=1) + pb_ref[...]) * ps_ref[...]
    return x + mod[N_MOD + 2:N_MOD + 3] * y


def _s5_tail_update(x, ys, g_ref, mod, dskip_ref, gw_ref, gb_ref):
    d = x.shape[1]
    h = _modnorm(x, g_ref, mod, 1)
    y = _gelu_tanh(dskip_ref[...] * h + ys).astype(BF16)
    yield
    ms = []
    for k in range(d // GLU_CHUNK):
        cv = slice(GLU_CHUNK * k, GLU_CHUNK * (k + 1))
        cg = slice(d + GLU_CHUNK * k, d + GLU_CHUNK * (k + 1))
        val = _dot(y, gw_ref[:, cv]) + gb_ref[:, cv]
        gate = _dot(y, gw_ref[:, cg]) + gb_ref[:, cg]
        ms.append(val * _sigmoid(gate))
        yield
    return x + mod[N_MOD + 2:N_MOD + 3] * jnp.concatenate(ms, axis=1)


def _cast_weight_block(n_extra, win_ref, wout_ref, *refs):
    wg_ref, wu_ref, wom_ref = refs[n_extra:n_extra + 3]
    for src, dst in zip(refs[:n_extra], refs[n_extra + 3:]):
        dst[...] = src[...].astype(BF16)
    d_ff = win_ref.shape[1] // 2
    w = win_ref[...]
    for src, dst in ((w[:, :d_ff], wg_ref), (w[:, d_ff:], wu_ref)):
        dst[:, :d_ff] = src.astype(BF16)
        if dst.shape[1] > d_ff:
            dst[:, d_ff:] = jnp.zeros((dst.shape[0], dst.shape[1] - d_ff), BF16)
    wom_ref[...] = wout_ref[...].astype(BF16)


def _layer_kernel(mixer, emit_h, final_norm, cast_next, seq_len, n_sub, *refs):
    refs = list(refs)
    x_ref, mod_ref, g_ref = refs[:3]
    del refs[:3]
    mod = mod_ref[0]
    if mixer == "pool":
        xp_ref, xn_ref, pw_ref, pb_ref, ps_ref = refs[:5]
        del refs[:5]
    elif mixer == "s5":
        ys_ref, dskip_ref, gw_ref, gb_ref = refs[:4]
        del refs[:4]
    wg_ref, wu_ref, wom_ref, wot_ref = refs[:4]
    del refs[:4]
    if final_norm:
        fg_ref = refs.pop(0)
    if cast_next is not None:
        cast_in = refs[:2 + cast_next]
        del refs[:2 + cast_next]
    o_ref = refs.pop(0)
    if emit_h:
        h_ref = refs.pop(0)
    if cast_next is not None:
        _cast_weight_block(cast_next, *cast_in, *refs[:3 + cast_next])
        del refs[:3 + cast_next]
    act_ref = refs.pop(0)
    tm = x_ref.shape[0]
    ts = tm // n_sub
    chunks = ts // CHUNK
    if mixer == "pool":
        hs_ref = refs.pop(0)
        _pool_fill(x_ref, xp_ref, xn_ref, g_ref, mod, hs_ref, seq_len)
    elif mixer == "s5":
        rows_ref = refs.pop(0)
        n_slab = rows_ref.shape[0]
        for l in range(CHUNK):
            for s in range(n_slab):
                rows_ref[s, pl.ds(l, tm // CHUNK, stride=CHUNK_PITCH), :] = ys_ref[l, :, LANES * s:LANES * (s + 1)]
    if emit_h:
        hrows_ref = refs.pop(0)

    def sub_tile(si):
        row0 = si * ts
        x = x_ref[row0:row0 + ts, :]
        if mixer == "pool":
            x = yield from _pool_update(x, row0, tm, mod, pw_ref, pb_ref, ps_ref, hs_ref, seq_len)
        elif mixer == "s5":
            ys = jnp.concatenate(
                [jnp.concatenate([rows_ref[s, CHUNK_PITCH * c:CHUNK_PITCH * c + CHUNK, :]
                                  for c in range(si * chunks, (si + 1) * chunks)], axis=0)
                 for s in range(n_slab)], axis=1)
            x = yield from _s5_tail_update(x, ys, g_ref, mod, dskip_ref, gw_ref, gb_ref)
        sub = 0 if mixer is None else 2
        x = yield from _swiglu_step(x, g_ref, mod, sub, wg_ref, wu_ref, wom_ref, wot_ref, act_ref.at[si])
        if final_norm:
            x = _rms(x, fg_ref[...])
        o_ref[row0:row0 + ts, :] = x
        if emit_h:
            h = _modnorm(x, g_ref, mod, 1)
            for c in range(chunks):
                base = CHUNK_PITCH * (si * chunks + c)
                for s in range(hrows_ref.shape[0]):
                    hrows_ref[s, base:base + CHUNK, :] = h[CHUNK * c:CHUNK * (c + 1), LANES * s:LANES * (s + 1)]

    tiles = [sub_tile(si) for si in range(n_sub)]
    pending = [True] * n_sub
    for marker in tiles[0]:
        if marker == MATMUL_PHASE:
            pending[0] = False
            break
    for si, tile in enumerate(tiles):
        nxt = si + 1 if si + 1 < n_sub else None
        for _ in tile:
            if nxt is not None and pending[nxt]:
                pending[nxt] = next(tiles[nxt]) != MATMUL_PHASE
        while nxt is not None and pending[nxt]:
            pending[nxt] = next(tiles[nxt]) != MATMUL_PHASE
    if emit_h:
        for l in range(CHUNK):
            for s in range(hrows_ref.shape[0]):
                h_ref[l, :, LANES * s:LANES * (s + 1)] = hrows_ref[s, pl.ds(l, tm // CHUNK, stride=CHUNK_PITCH), :]


def _const_spec(shape):
    nd = len(shape)
    return pl.BlockSpec(shape, lambda *_: (0,) * nd, pipeline_mode=pl.Buffered(1))


def _ffn_dims(ffn_w_in):
    d_ff = ffn_w_in.shape[-1] // 2
    return d_ff, d_ff + -d_ff % FF_CHUNK, d_ff // FF_CHUNK * FF_CHUNK


def _cast_io(ffn_w_in, ffn_w_out, layer, which, extras, steps):
    d = ffn_w_in.shape[2]
    d_ff, ffp, k_main = _ffn_dims(ffn_w_in)
    n = 1
    while n < CAST_BLOCKS and steps % (2 * n) == 0:
        n *= 2
    assert d % (n * 2 * SUBLANES) == 0 and k_main % (n * 2 * SUBLANES) == 0
    blk = lambda i: i * n // steps
    in_specs = [pl.BlockSpec((None, None, d // n, 2 * d_ff), lambda i: (layer, which, blk(i), 0)),
                pl.BlockSpec((None, None, k_main // n, d), lambda i: (layer, which, blk(i), 0))]
    out_specs = [pl.BlockSpec((d // n, ffp), lambda i: (blk(i), 0)),
                 pl.BlockSpec((d // n, ffp), lambda i: (blk(i), 0)),
                 pl.BlockSpec((k_main // n, d), lambda i: (blk(i), 0))]
    out_shape = [jax.ShapeDtypeStruct((d, ffp), BF16), jax.ShapeDtypeStruct((d, ffp), BF16),
                 jax.ShapeDtypeStruct((k_main, d), BF16)]
    for m in extras:
        rows, cols = m.shape
        assert rows % (n * 2 * SUBLANES) == 0
        in_specs.append(pl.BlockSpec((rows // n, cols), lambda i: (blk(i), 0)))
        out_specs.append(pl.BlockSpec((rows // n, cols), lambda i: (blk(i), 0)))
        out_shape.append(jax.ShapeDtypeStruct(m.shape, BF16))
    args = [ffn_w_in.astype(F32), ffn_w_out.astype(F32)] + [m.astype(F32) for m in extras]
    return args, in_specs, out_specs, out_shape


def _w_out_tail(ffn_w_out, layer, which, k_main, ffp):
    tail = ffn_w_out[layer, which, k_main:, :].astype(BF16)
    return jnp.pad(tail, ((0, ffp - k_main - tail.shape[0]), (0, 0)))


def _cast_first_weights(ffn_w_in, ffn_w_out, layer, which):
    args, in_specs, out_specs, out_shape = _cast_io(ffn_w_in, ffn_w_out, layer, which, (), CAST_BLOCKS)
    return pl.pallas_call(
        functools.partial(_cast_weight_block, 0),
        grid=(CAST_BLOCKS,),
        in_specs=in_specs,
        out_specs=out_specs,
        out_shape=out_shape,
        compiler_params=pltpu.CompilerParams(
            dimension_semantics=("arbitrary",), vmem_limit_bytes=VMEM_LIMIT),
        name="weight_cast",
    )(*args)


def _tile_plan(seq_len, mixer, emit_h):
    wide = mixer != "s5" and not emit_h
    tm = min(1024 if wide else 512, seq_len)
    return tm, max(1, tm // 256)


def _layer_call(x, mod, norm_g, weights, seq_len, *, mixer=None, mixer_args=(),
                emit_h=False, final_g=None, next_ffn=None):
    n_tok, d = x.shape
    tm, n_sub = _tile_plan(seq_len, mixer, emit_h)
    assert seq_len % tm == 0 and tm % (n_sub * CHUNK * SUBLANES) == 0
    tiles_per_seq = seq_len // tm
    wg, wu, wom, wot = weights
    ffp = wg.shape[1]
    row_spec = pl.BlockSpec((tm, d), lambda i: (i, 0))
    chunk_spec = pl.BlockSpec((None, CHUNK, tm // CHUNK, d),
                              lambda i: (i // tiles_per_seq, 0, i % tiles_per_seq, 0))
    args = [x, mod, norm_g.astype(F32)]
    specs = [row_spec,
             pl.BlockSpec((1, N_SUB * N_MOD, d), lambda i: (i // tiles_per_seq, 0, 0)),
             _const_spec((N_SUB, d))]
    scratch = [pltpu.VMEM((n_sub, tm // n_sub, ffp), BF16)]
    if mixer == "pool":
        pool_w, pool_b, pool_scale = mixer_args
        hb = tm // POOL_HALO
        last = n_tok // POOL_HALO - 1
        args += [x, x, pool_w.astype(BF16), pool_b.astype(F32).reshape(1, d),
                 pool_scale.astype(F32).reshape(1, d)]
        specs += [pl.BlockSpec((POOL_HALO, d), lambda i: (jnp.maximum(i * hb - 1, 0), 0)),
                  pl.BlockSpec((POOL_HALO, d), lambda i: (jnp.minimum((i + 1) * hb, last), 0)),
                  _const_spec(pool_w.shape), _const_spec((1, d)), _const_spec((1, d))]
        scratch.append(pltpu.VMEM((tm + 2 * POOL_HALO, d), F32))
    elif mixer == "s5":
        ys, d_skip, glu_w, glu_b = mixer_args
        args += [ys, d_skip.astype(F32).reshape(1, d), glu_w.astype(BF16),
                 glu_b.astype(F32).reshape(1, 2 * d)]
        specs += [chunk_spec, _const_spec((1, d)), _const_spec((d, 2 * d)), _const_spec((1, 2 * d))]
        scratch.append(pltpu.VMEM((d // LANES, tm // CHUNK * CHUNK_PITCH, LANES), F32))
    args += [wg, wu, wom, wot]
    specs += [_const_spec(w.shape) for w in (wg, wu, wom, wot)]
    if final_g is not None:
        args.append(final_g.astype(F32).reshape(1, d))
        specs.append(_const_spec((1, d)))
    if next_ffn is not None:
        cast_args, cast_in, cast_out, cast_shape = _cast_io(*next_ffn, n_tok // tm)
        args += cast_args
        specs += cast_in
    out_shape = [jax.ShapeDtypeStruct((n_tok, d), F32)]
    out_specs = [row_spec]
    if emit_h:
        out_shape.append(jax.ShapeDtypeStruct((n_tok // seq_len, CHUNK, seq_len // CHUNK, d), F32))
        out_specs.append(chunk_spec)
        scratch.append(pltpu.VMEM((d // LANES, tm // CHUNK * CHUNK_PITCH, LANES), F32))
    if next_ffn is not None:
        out_shape += cast_shape
        out_specs += cast_out
    outs = pl.pallas_call(
        functools.partial(_layer_kernel, mixer, emit_h, final_g is not None,
                          None if next_ffn is None else len(next_ffn[4]), seq_len, n_sub),
        grid=(n_tok // tm,),
        in_specs=specs,
        out_specs=out_specs,
        out_shape=out_shape,
        scratch_shapes=scratch,
        compiler_params=pltpu.CompilerParams(
            dimension_semantics=("arbitrary",), vmem_limit_bytes=VMEM_LIMIT),
        name="layer_" + (mixer or "ffn"),
    )(*args)
    n_act = 2 if emit_h else 1
    acts = tuple(outs[:n_act]) if emit_h else outs[0]
    if next_ffn is None:
        return acts
    _, ffp_next, k_main = _ffn_dims(next_ffn[0])
    tail = _w_out_tail(next_ffn[1], next_ffn[2], next_ffn[3], k_main, ffp_next)
    return acts, (*outs[n_act:n_act + 3], tail), tuple(outs[n_act + 3:])


def _cmul(ar, ai, br, bi):
    return ar * br - ai * bi, ar * bi + ai * br


def _s5_prep_kernel(n_levels, lr_ref, li_ref, ldt_ref, bre_ref, bim_ref, cre_ref, cim_ref,
                    t_ref, bz_ref, cx_ref, dec_ref):
    gb = lr_ref.shape[1]
    w = t_ref.shape[1]
    chunk = w // SSM_H
    n_dbl = chunk.bit_length() - 1
    assert chunk == 1 << n_dbl
    lane = lax.broadcasted_iota(jnp.int32, (1, LANES), 1)
    re_half = lane < SSM_P
    wlane = lax.broadcasted_iota(jnp.int32, (1, w), 1)
    level_e = (chunk << lax.broadcasted_iota(jnp.int32, (n_levels, 1), 0)).astype(F32)

    def tile_rows(v):
        return jnp.broadcast_to(v[None], (chunk, SSM_H, LANES)).reshape(w, LANES)

    for g in range(gb):
        rs, zs, cs = [], [], []
        for direction in range(2):
            lr, li = lr_ref[direction, g], li_ref[direction, g]
            dt = jnp.exp(ldt_ref[direction, g])
            lrdt, lidt = lr * dt, li * dt

            def a_pow(e):
                mag = jnp.exp(lrdt * e)
                return mag * jnp.cos(lidt * e), mag * jnp.sin(lidt * e)

            ar, ai = a_pow(1.0)
            den = lr * lr + li * li
            fr = ((ar - 1.0) * lr + ai * li) / den
            fi = (ai * lr - (ar - 1.0) * li) / den
            bre, bim = bre_ref[direction, g], bim_ref[direction, g]
            bbr, bbi = _cmul(fr, fi, bre, bim)
            cre, cim = tile_rows(cre_ref[direction, g]), tile_rows(cim_ref[direction, g])
            bbr_t, bbi_t = tile_rows(bbr), tile_rows(bbi)

            one = jnp.ones((SSM_H, LANES), F32)
            up_r, up_i, dn_r, dn_i = one, 0.0 * one, one, 0.0 * one
            for k in range(n_dbl):
                sr, si = a_pow(float(1 << k))
                nr, ni = _cmul(up_r, up_i, sr, si)
                up_r, up_i = jnp.concatenate([up_r, nr], axis=0), jnp.concatenate([up_i, ni], axis=0)
                nr, ni = _cmul(dn_r, dn_i, sr, si)
                dn_r, dn_i = jnp.concatenate([nr, dn_r], axis=0), jnp.concatenate([ni, dn_i], axis=0)
            up1_r, up1_i = _cmul(up_r, up_i, ar, ai)
            dn1_r, dn1_i = _cmul(dn_r, dn_i, ar, ai)

            def c_times(er, ei):
                pr, pi = _cmul(cre, cim, er, ei)
                return jnp.where(re_half, pr, -pi)

            def b_times(er, ei):
                pr, pi = _cmul(bbr_t, bbi_t, er, ei)
                return jnp.where(re_half, pr, pi)

            fwd = direction == 0
            zs.append(b_times(dn_r, dn_i) if fwd else b_times(up_r, up_i))
            cs.append(c_times(up1_r, up1_i) if fwd else c_times(dn1_r, dn1_i))
            gmat = (c_times(up_r, up_i) if fwd else c_times(dn_r, dn_i)).T
            bbt = jnp.where(re_half, bbr, bbi)
            rs.append(_dot3(bbt, gmat))
            er, ei = a_pow(level_e)
            base = 2 * n_levels * direction
            dec_ref[g, base:base + n_levels, :] = er
            dec_ref[g, base + n_levels:base + 2 * n_levels, :] = ei
        for i in range(chunk):
            s = SSM_H * i
            fwd_rows = jnp.where(wlane >= s, pltpu.roll(rs[0], s, 1) if s else rs[0], 0.0)
            e = (s + SSM_H) % w
            bwd_rows = jnp.where(wlane < s + SSM_H, pltpu.roll(rs[1], e, 1) if e else rs[1], 0.0)
            t_ref[g, s:s + SSM_H, :] = (fwd_rows + bwd_rows).astype(BF16)
        bz_ref[g] = jnp.concatenate(zs, axis=1).astype(BF16)
        cx_ref[g] = jnp.concatenate([cs[0].T, cs[1].T], axis=0).astype(BF16)


def _s5_prep(lam_re, lam_im, log_dt, b_re, b_im, c_re, c_im, n_levels):
    n_dir, n_grp, n_state = lam_re.shape
    assert n_dir == 2 and n_state == SSM_P and b_re.shape[-1] == SSM_H
    w = CHUNK * SSM_H
    gb = GROUPS_PER_SLAB
    assert n_grp % gb == 0 and n_levels % SUBLANES == 0

    def twice(v):
        return jnp.concatenate([v, v], axis=-1).astype(F32)

    lr = twice(lam_re)[:, :, None, :]
    li = twice(lam_im)[:, :, None, :]
    ldt = jnp.broadcast_to(log_dt.astype(F32)[:, :, None, None], (n_dir, n_grp, 1, LANES))
    bre, bim = twice(jnp.swapaxes(b_re, 2, 3)), twice(jnp.swapaxes(b_im, 2, 3))
    cre, cim = twice(c_re), twice(c_im)
    n_dec = 2 * 2 * n_levels
    row = lambda r: pl.BlockSpec((n_dir, gb, r, LANES), lambda i: (0, i, 0, 0))
    return pl.pallas_call(
        functools.partial(_s5_prep_kernel, n_levels),
        grid=(n_grp // gb,),
        in_specs=[row(1), row(1), row(1), row(SSM_H), row(SSM_H), row(SSM_H), row(SSM_H)],
        out_specs=[pl.BlockSpec((gb, w, w), lambda i: (i, 0, 0)),
                   pl.BlockSpec((gb, w, 2 * LANES), lambda i: (i, 0, 0)),
                   pl.BlockSpec((gb, 2 * LANES, w), lambda i: (i, 0, 0)),
                   pl.BlockSpec((gb, n_dec, LANES), lambda i: (i, 0, 0))],
        out_shape=[jax.ShapeDtypeStruct((n_grp, w, w), BF16),
                   jax.ShapeDtypeStruct((n_grp, w, 2 * LANES), BF16),
                   jax.ShapeDtypeStruct((n_grp, 2 * LANES, w), BF16),
                   jax.ShapeDtypeStruct((n_grp, n_dec, LANES), F32)],
        compiler_params=pltpu.CompilerParams(
            dimension_semantics=("arbitrary",), vmem_limit_bytes=VMEM_LIMIT),
        name="s5_prep",
    )(lr, li, ldt, bre, bim, cre, cim)


def _chunk_scan(zr, zi, a_re, a_im, reverse):
    n_rows = zr.shape[0]
    rc = lax.broadcasted_iota(jnp.int32, (n_rows, 1), 0)

    def shifted(v, dist):
        if reverse:
            return jnp.where(rc < n_rows - dist, pltpu.roll(v, n_rows - dist, 0), 0.0)
        return jnp.where(rc >= dist, pltpu.roll(v, dist, 0), 0.0)

    sr, si = shifted(zr, 1), shifted(zi, 1)
    for k, (ar, ai) in enumerate(zip(a_re, a_im)):
        if (1 << k) >= n_rows:
            break
        hr, hi = shifted(sr, 1 << k), shifted(si, 1 << k)
        sr, si = sr + (hr * ar - hi * ai), si + (hr * ai + hi * ar)
    return sr, si


def _slab_permutation():
    n = GROUPS_PER_SLAB
    idx = np.arange(n * LANES)
    a, b, h = idx // LANES, (idx % LANES) // SSM_H, idx % SSM_H
    perm = np.zeros((n * LANES, n * LANES), np.float32)
    perm[idx, b * LANES + a * SSM_H + h] = 1.0
    return jnp.asarray(perm, BF16)


def _unit_transpose(vs):
    vs = list(vs)
    n = len(vs)
    assert n * SSM_H == LANES
    unit = lax.broadcasted_iota(jnp.int32, (1, LANES), 1) // SSM_H
    d = n // 2
    while d:
        upper = (unit & d) != 0
        for i in range(n):
            if i & d:
                continue
            a, b = vs[i], vs[i + d]
            vs[i] = jnp.where(upper, pltpu.roll(b, d * SSM_H, 1), a)
            vs[i + d] = jnp.where(upper, b, pltpu.roll(a, LANES - d * SSM_H, 1))
        d //= 2
    return vs


def _s5_core_kernel(n_levels, h_ref, perm_ref, t_ref, bz_ref, cx_ref, dec_ref, y_ref, u_ref, yg_ref):
    n_grp = GROUPS_PER_SLAB
    perm = perm_ref[...]
    for o in range(CHUNK // n_grp):
        v = jnp.concatenate([h_ref[n_grp * o + l8] for l8 in range(n_grp)], axis=1)
        uv = _dot(v.astype(BF16), perm).astype(BF16)
        for g in range(n_grp):
            u_ref[g, :, LANES * o:LANES * (o + 1)] = uv[:, LANES * g:LANES * (g + 1)]
    first_half = lax.broadcasted_iota(jnp.int32, (1, LANES), 1) < SSM_P

    def swap_halves(v):
        return pltpu.roll(v, SSM_P, 1)

    for g0 in range(0, n_grp, 2):
        z = [_dot(u_ref[g], bz_ref[g]) for g in (g0, g0 + 1)]
        xs = [[], []]
        for direction in range(2):
            za, zb = (v[:, LANES * direction:LANES * (direction + 1)] for v in z)
            zr = jnp.where(first_half, za, swap_halves(zb))
            zi = jnp.where(first_half, swap_halves(za), zb)
            base = 2 * n_levels * direction
            a_re = [jnp.where(first_half, dec_ref[g0, r:r + 1, :], dec_ref[g0 + 1, r:r + 1, :])
                    for r in range(base, base + n_levels)]
            a_im = [jnp.where(first_half, dec_ref[g0, r:r + 1, :], dec_ref[g0 + 1, r:r + 1, :])
                    for r in range(base + n_levels, base + 2 * n_levels)]
            xr, xi = _chunk_scan(zr, zi, a_re, a_im, direction == 1)
            xs[0].append(jnp.where(first_half, xr, swap_halves(xi)))
            xs[1].append(jnp.where(first_half, swap_halves(xr), xi))
        for k, g in enumerate((g0, g0 + 1)):
            y = _dot(u_ref[g], t_ref[g])
            y = y + _dot(xs[k][0].astype(BF16), cx_ref[g, :LANES, :])
            y = y + _dot(xs[k][1].astype(BF16), cx_ref[g, LANES:, :])
            yg_ref[g] = y
    for o in range(CHUNK // n_grp):
        ys = _unit_transpose([yg_ref[g, :, LANES * o:LANES * (o + 1)] for g in range(n_grp)])
        for l8 in range(n_grp):
            y_ref[n_grp * o + l8] = ys[l8]


def _s5_mix(h, lam_re, lam_im, log_dt, b_re, b_im, c_re, c_im):
    bsz, _, rows, d = h.shape
    w = CHUNK * SSM_H
    gb = GROUPS_PER_SLAB
    assert h.shape[1] == CHUNK and CHUNK % gb == 0 and d % LANES == 0
    n_levels = -(-max(1, (rows - 1).bit_length()) // SUBLANES) * SUBLANES
    t_mat, bz, cx, dec = _s5_prep(lam_re, lam_im, log_dt, b_re, b_im, c_re, c_im, n_levels)
    perm = _slab_permutation()
    slab = pl.BlockSpec((None, CHUNK, rows, LANES), lambda s, b: (b, 0, 0, s))
    grp = lambda r, c: pl.BlockSpec((gb, r, c), lambda s, b: (s, 0, 0))
    return pl.pallas_call(
        functools.partial(_s5_core_kernel, n_levels),
        grid=(d // LANES, bsz),
        in_specs=[slab, _const_spec(perm.shape), grp(w, w), grp(w, 2 * LANES), grp(2 * LANES, w),
                  grp(dec.shape[1], LANES)],
        out_specs=slab,
        out_shape=jax.ShapeDtypeStruct(h.shape, F32),
        scratch_shapes=[pltpu.VMEM((gb, rows, w), BF16), pltpu.VMEM((gb, rows, w), F32)],
        compiler_params=pltpu.CompilerParams(
            dimension_semantics=("arbitrary", "arbitrary"), vmem_limit_bytes=VMEM_LIMIT),
        name="s5_core",
    )(h, perm, t_mat, bz, cx, dec)


def kernel(x, c, mod_w, mod_b, norm_g, ffn_w_in, ffn_w_out, pool_w, pool_b, pool_scale,
           ssm_lam_re, ssm_lam_im, ssm_log_dt, ssm_b_re, ssm_b_im, ssm_c_re, ssm_c_im,
           ssm_d, glu_w, glu_b, final_g):
    bsz, seq_len, d = x.shape
    depth = mod_w.shape[0]
    n_mixers = 2
    mods = _modulation(c, mod_w, mod_b)
    xf = x.astype(F32).reshape(bsz * seq_len, d)
    _, ffp, k_main = _ffn_dims(ffn_w_in)
    weights = (*_cast_first_weights(ffn_w_in, ffn_w_out, 0, 0), _w_out_tail(ffn_w_out, 0, 0, k_main, ffp))
    for i in range(depth):
        j = i // n_mixers
        is_s5 = i % n_mixers == 1
        last = i == depth - 1
        extras = (glu_w[j],) if is_s5 else ()
        acts, weights, extras = _layer_call(xf, mods[i], norm_g[i], weights, seq_len, emit_h=is_s5,
                                            next_ffn=(ffn_w_in, ffn_w_out, i, 1, extras))
        if is_s5:
            xf, h = acts
            ys = _s5_mix(h, ssm_lam_re[j], ssm_lam_im[j], ssm_log_dt[j], ssm_b_re[j],
                         ssm_b_im[j], ssm_c_re[j], ssm_c_im[j])
            mixer, mixer_args = "s5", (ys, ssm_d[j], extras[0], glu_b[j])
        else:
            xf = acts
            mixer, mixer_args = "pool", (pool_w[j], pool_b[j], pool_scale[j])
        res = _layer_call(xf, mods[i], norm_g[i], weights, seq_len, mixer=mixer, mixer_args=mixer_args,
                          final_g=final_g if last else None,
                          next_ffn=None if last else (ffn_w_in, ffn_w_out, i + 1, 0, ()))
        xf, weights = (res, None) if last else res[:2]
    return xf.reshape(bsz, seq_len, d).astype(x.dtype)
```

```python
import functools
import math

import numpy as np

import jax
import jax.numpy as jnp
from jax import lax
from jax.experimental import pallas as pl
from jax.experimental.pallas import tpu as pltpu

F32 = jnp.float32
BF16 = jnp.bfloat16

EPS = 1e-6
N_SUB = 3
N_MOD = 3
POOL_WINDOWS = (2, 4, 8, 16)
POOL_HALO = 8
SSM_H = 16
SSM_P = 64
CHUNK = 32
CHUNK_PITCH = CHUNK + 4
FF_CHUNK = 256
GLU_CHUNK = 512
MATMUL_PHASE = "matmul phase"
LANES = 128
CAST_BLOCKS = 16
SUBLANES = 8
GROUPS_PER_SLAB = LANES // SSM_H
VMEM_LIMIT = 56 * 1024 * 1024


def _sigmoid(v):
    return 1.0 / (1.0 + jnp.exp(-v))


def _split3(a):
    hi = a.astype(BF16)
    lo = (a - hi.astype(F32)).astype(BF16)
    return hi, lo


def _dot(a, b):
    return jnp.dot(a, b, preferred_element_type=F32)


def _dot3(a, b):
    ah, al = _split3(a)
    bh, bl = _split3(b)
    return _dot(ah, bh) + (_dot(ah, bl) + _dot(al, bh))


def _mod_kernel(c_ref, w_ref, b_ref, o_ref):
    c = c_ref[...]
    cond = c * _sigmoid(c)
    o_ref[0] = _dot3(cond, w_ref[0]) + b_ref[0]


def _modulation(c, mod_w, mod_b):
    depth, d, n = mod_w.shape
    bsz = c.shape[0]
    rows = SUBLANES
    tn = 1024
    assert n % tn == 0 and bsz <= rows
    c_pad = jnp.zeros((rows, d), F32).at[:bsz].set(c.astype(F32))
    out = pl.pallas_call(
        _mod_kernel,
        grid=(depth, n // tn),
        in_specs=[
            pl.BlockSpec((rows, d), lambda i, j: (0, 0)),
            pl.BlockSpec((1, d, tn), lambda i, j: (i, 0, j)),
            pl.BlockSpec((1, 1, tn), lambda i, j: (i, 0, j)),
        ],
        out_specs=pl.BlockSpec((1, rows, tn), lambda i, j: (i, 0, j)),
        out_shape=jax.ShapeDtypeStruct((depth, rows, n), F32),
        compiler_params=pltpu.CompilerParams(
            dimension_semantics=("arbitrary", "arbitrary"), vmem_limit_bytes=VMEM_LIMIT),
        name="adaln_modulation",
    )(c_pad, mod_w.astype(F32), mod_b.astype(F32).reshape(depth, 1, n))
    return out[:, :bsz].reshape(depth, bsz, N_SUB * N_MOD, d)


def _rms(x, g):
    return x * lax.rsqrt(jnp.mean(x * x, axis=-1, keepdims=True) + EPS) * g


def _modnorm(x, g_ref, mod, sub):
    r = N_MOD * sub
    shift, scale = mod[r:r + 1], mod[r + 1:r + 2]
    return _rms(x, g_ref[sub:sub + 1, :]) * (1.0 + scale) + shift


def _swiglu_step(x, g_ref, mod, sub, wg_ref, wu_ref, wom_ref, wot_ref, act_ref):
    h = _modnorm(x, g_ref, mod, sub).astype(BF16)
    yield MATMUL_PHASE
    n_chunks = act_ref.shape[1] // FF_CHUNK
    for k in range(n_chunks):
        cols = slice(FF_CHUNK * k, FF_CHUNK * (k + 1))
        gate = _dot(h, wg_ref[:, cols])
        up = _dot(h, wu_ref[:, cols])
        act_ref[:, cols] = (gate * _sigmoid(gate) * up).astype(BF16)
        yield
    k_main = wom_ref.shape[0]
    out = _dot(act_ref[:, :k_main], wom_ref[...])
    if act_ref.shape[1] > k_main:
        out = out + _dot(act_ref[:, k_main:], wot_ref[...])
    r = N_MOD * sub + 2
    return x + (0.5 * mod[r:r + 1]) * out


def _gelu_tanh(v):
    return 0.5 * v * (1.0 + jnp.tanh(math.sqrt(2.0 / math.pi) * (v + 0.044715 * (v * v * v))))


def _pool_fill(x_ref, xp_ref, xn_ref, g_ref, mod, hs_ref, seq_len):
    tm = x_ref.shape[0]
    tiles_per_seq = seq_len // tm
    j = pl.program_id(0) % tiles_per_seq
    hp = _modnorm(xp_ref[...], g_ref, mod, 1)
    hn = _modnorm(xn_ref[...], g_ref, mod, 1)
    hs_ref[0:POOL_HALO, :] = jnp.where(j > 0, hp, 0.0)
    hs_ref[POOL_HALO:POOL_HALO + tm, :] = _modnorm(x_ref[...], g_ref, mod, 1)
    hs_ref[POOL_HALO + tm:, :] = jnp.where(j < tiles_per_seq - 1, hn, 0.0)


def _pool_update(x, row0, tm, mod, pw_ref, pb_ref, ps_ref, hs_ref, seq_len):
    ts, d = x.shape
    cg = d // len(POOL_WINDOWS)
    j = pl.program_id(0) % (seq_len // tm)
    t = j * tm + row0 + lax.broadcasted_iota(jnp.int32, (ts, 1), 0)
    n_ext = ts + 2 * POOL_HALO
    ys = []
    for gi, window in enumerate(POOL_WINDOWS):
        left = window // 2
        right = window - 1 - left
        cols = slice(gi * cg, (gi + 1) * cg)
        assert window & (window - 1) == 0 and right < POOL_HALO
        ext = hs_ref[row0:row0 + n_ext, cols]
        run = ext
        span = 1
        while span < window:
            run = run + pltpu.roll(run, span, 0)
            span *= 2
        if right:
            run = pltpu.roll(run, n_ext - right, 0)
        total = run[POOL_HALO:POOL_HALO + ts]
        lo = jnp.maximum(t - left, 0)
        hi = jnp.minimum(t + right, seq_len - 1)
        count = (hi - lo + 1).astype(F32)
        p = total / count - ext[POOL_HALO:POOL_HALO + ts]
        ys.append(_dot(p.astype(BF16), pw_ref[gi]))
        yield
    y = (jnp.concatenate(ys, axis=1) + pb_ref[...]) * ps_ref[...]
    return x + mod[N_MOD + 2:N_MOD + 3] * y


def _s5_tail_update(x, ys, g_ref, mod, dskip_ref, gw_ref, gb_ref):
    d = x.shape[1]
    h = _modnorm(x, g_ref, mod, 1)
    y = _gelu_tanh(dskip_ref[...] * h + ys).astype(BF16)
    yield
    ms = []
    for k in range(d // GLU_CHUNK):
        cv = slice(GLU_CHUNK * k, GLU_CHUNK * (k + 1))
        cg = slice(d + GLU_CHUNK * k, d + GLU_CHUNK * (k + 1))
        val = _dot(y, gw_ref[:, cv]) + gb_ref[:, cv]
        gate = _dot(y, gw_ref[:, cg]) + gb_ref[:, cg]
        ms.append(val * _sigmoid(gate))
        yield
    return x + mod[N_MOD + 2:N_MOD + 3] * jnp.concatenate(ms, axis=1)


def _cast_weight_block(n_extra, win_ref, wout_ref, *refs):
    wg_ref, wu_ref, wom_ref = refs[n_extra:n_extra + 3]
    for src, dst in zip(refs[:n_extra], refs[n_extra + 3:]):
        dst[...] = src[...].astype(BF16)
    d_ff = win_ref.shape[1] // 2
    w = win_ref[...]
    for src, dst in ((w[:, :d_ff], wg_ref), (w[:, d_ff:], wu_ref)):
        dst[:, :d_ff] = src.astype(BF16)
        if dst.shape[1] > d_ff:
            dst[:, d_ff:] = jnp.zeros((dst.shape[0], dst.shape[1] - d_ff), BF16)
    wom_ref[...] = wout_ref[...].astype(BF16)


def _layer_kernel(mixer, emit_h, final_norm, cast_next, seq_len, n_sub, *refs):
    refs = list(refs)
    x_ref, mod_ref, g_ref = refs[:3]
    del refs[:3]
    mod = mod_ref[0]
    if mixer == "pool":
        xp_ref, xn_ref, pw_ref, pb_ref, ps_ref = refs[:5]
        del refs[:5]
    elif mixer == "s5":
        ys_ref, dskip_ref, gw_ref, gb_ref = refs[:4]
        del refs[:4]
    wg_ref, wu_ref, wom_ref, wot_ref = refs[:4]
    del refs[:4]
    if final_norm:
        fg_ref = refs.pop(0)
    if cast_next is not None:
        cast_in = refs[:2 + cast_next]
        del refs[:2 + cast_next]
    o_ref = refs.pop(0)
    if emit_h:
        h_ref = refs.pop(0)
    if cast_next is not None:
        _cast_weight_block(cast_next, *cast_in, *refs[:3 + cast_next])
        del refs[:3 + cast_next]
    act_ref = refs.pop(0)
    tm = x_ref.shape[0]
    ts = tm // n_sub
    chunks = ts // CHUNK
    if mixer == "pool":
        hs_ref = refs.pop(0)
        _pool_fill(x_ref, xp_ref, xn_ref, g_ref, mod, hs_ref, seq_len)
    elif mixer == "s5":
        rows_ref = refs.pop(0)
        n_slab = rows_ref.shape[0]
        for l in range(CHUNK):
            for s in range(n_slab):
                rows_ref[s, pl.ds(l, tm // CHUNK, stride=CHUNK_PITCH), :] = ys_ref[l, :, LANES * s:LANES * (s + 1)]
    if emit_h:
        hrows_ref = refs.pop(0)

    def sub_tile(si):
        row0 = si * ts
        x = x_ref[row0:row0 + ts, :]
        if mixer == "pool":
            x = yield from _pool_update(x, row0, tm, mod, pw_ref, pb_ref, ps_ref, hs_ref, seq_len)
        elif mixer == "s5":
            ys = jnp.concatenate(
                [jnp.concatenate([rows_ref[s, CHUNK_PITCH * c:CHUNK_PITCH * c + CHUNK, :]
                                  for c in range(si * chunks, (si + 1) * chunks)], axis=0)
                 for s in range(n_slab)], axis=1)
            x = yield from _s5_tail_update(x, ys, g_ref, mod, dskip_ref, gw_ref, gb_ref)
        sub = 0 if mixer is None else 2
        x = yield from _swiglu_step(x, g_ref, mod, sub, wg_ref, wu_ref, wom_ref, wot_ref, act_ref.at[si])
        if final_norm:
            x = _rms(x, fg_ref[...])
        o_ref[row0:row0 + ts, :] = x
        if emit_h:
            h = _modnorm(x, g_ref, mod, 1)
            for c in range(chunks):
                base = CHUNK_PITCH * (si * chunks + c)
                for s in range(hrows_ref.shape[0]):
                    hrows_ref[s, base:base + CHUNK, :] = h[CHUNK * c:CHUNK * (c + 1), LANES * s:LANES * (s + 1)]

    tiles = [sub_tile(si) for si in range(n_sub)]
    pending = [True] * n_sub
    for marker in tiles[0]:
        if marker == MATMUL_PHASE:
            pending[0] = False
            break
    for si, tile in enumerate(tiles):
        nxt = si + 1 if si + 1 < n_sub else None
        for _ in tile:
            if nxt is not None and pending[nxt]:
                pending[nxt] = next(tiles[nxt]) != MATMUL_PHASE
        while nxt is not None and pending[nxt]:
            pending[nxt] = next(tiles[nxt]) != MATMUL_PHASE
    if emit_h:
        for l in range(CHUNK):
            for s in range(hrows_ref.shape[0]):
                h_ref[l, :, LANES * s:LANES * (s + 1)] = hrows_ref[s, pl.ds(l, tm // CHUNK, stride=CHUNK_PITCH), :]


def _const_spec(shape):
    nd = len(shape)
    return pl.BlockSpec(shape, lambda *_: (0,) * nd, pipeline_mode=pl.Buffered(1))


def _ffn_dims(ffn_w_in):
    d_ff = ffn_w_in.shape[-1] // 2
    return d_ff, d_ff + -d_ff % FF_CHUNK, d_ff // FF_CHUNK * FF_CHUNK


def _cast_io(ffn_w_in, ffn_w_out, layer, which, extras, steps):
    d = ffn_w_in.shape[2]
    d_ff, ffp, k_main = _ffn_dims(ffn_w_in)
    n = 1
    while n < CAST_BLOCKS and steps % (2 * n) == 0:
        n *= 2
    assert d % (n * 2 * SUBLANES) == 0 and k_main % (n * 2 * SUBLANES) == 0
    blk = lambda i: i * n // steps
    in_specs = [pl.BlockSpec((None, None, d // n, 2 * d_ff), lambda i: (layer, which, blk(i), 0)),
                pl.BlockSpec((None, None, k_main // n, d), lambda i: (layer, which, blk(i), 0))]
    out_specs = [pl.BlockSpec((d // n, ffp), lambda i: (blk(i), 0)),
                 pl.BlockSpec((d // n, ffp), lambda i: (blk(i), 0)),
                 pl.BlockSpec((k_main // n, d), lambda i: (blk(i), 0))]
    out_shape = [jax.ShapeDtypeStruct((d, ffp), BF16), jax.ShapeDtypeStruct((d, ffp), BF16),
                 jax.ShapeDtypeStruct((k_main, d), BF16)]
    for m in extras:
        rows, cols = m.shape
        assert rows % (n * 2 * SUBLANES) == 0
        in_specs.append(pl.BlockSpec((rows // n, cols), lambda i: (blk(i), 0)))
        out_specs.append(pl.BlockSpec((rows // n, cols), lambda i: (blk(i), 0)))
        out_shape.append(jax.ShapeDtypeStruct(m.shape, BF16))
    args = [ffn_w_in.astype(F32), ffn_w_out.astype(F32)] + [m.astype(F32) for m in extras]
    return args, in_specs, out_specs, out_shape


def _w_out_tail(ffn_w_out, layer, which, k_main, ffp):
    tail = lax.optimization_barrier(ffn_w_out[layer, which, k_main:, :]).astype(BF16)
    return jnp.pad(tail, ((0, ffp - k_main - tail.shape[0]), (0, 0)))


def _cast_first_weights(ffn_w_in, ffn_w_out, layer, which):
    args, in_specs, out_specs, out_shape = _cast_io(ffn_w_in, ffn_w_out, layer, which, (), CAST_BLOCKS)
    return pl.pallas_call(
        functools.partial(_cast_weight_block, 0),
        grid=(CAST_BLOCKS,),
        in_specs=in_specs,
        out_specs=out_specs,
        out_shape=out_shape,
        compiler_params=pltpu.CompilerParams(
            dimension_semantics=("arbitrary",), vmem_limit_bytes=VMEM_LIMIT),
        name="weight_cast",
    )(*args)


def _tile_plan(seq_len, mixer, emit_h):
    wide = mixer != "s5" and not emit_h
    tm = min(1024 if wide else 512, seq_len)
    return tm, max(1, tm // (256 if mixer == "pool" else 512))


def _layer_call(x, mod, norm_g, weights, seq_len, *, mixer=None, mixer_args=(),
                emit_h=False, final_g=None, next_ffn=None):
    n_tok, d = x.shape
    tm, n_sub = _tile_plan(seq_len, mixer, emit_h)
    assert seq_len % tm == 0 and tm % (n_sub * CHUNK * SUBLANES) == 0
    tiles_per_seq = seq_len // tm
    wg, wu, wom, wot = weights
    ffp = wg.shape[1]
    row_spec = pl.BlockSpec((tm, d), lambda i: (i, 0))
    chunk_spec = pl.BlockSpec((None, CHUNK, tm // CHUNK, d),
                              lambda i: (i // tiles_per_seq, 0, i % tiles_per_seq, 0))
    args = [x, mod, norm_g.astype(F32)]
    specs = [row_spec,
             pl.BlockSpec((1, N_SUB * N_MOD, d), lambda i: (i // tiles_per_seq, 0, 0)),
             _const_spec((N_SUB, d))]
    scratch = [pltpu.VMEM((n_sub, tm // n_sub, ffp), BF16)]
    if mixer == "pool":
        pool_w, pool_b, pool_scale = mixer_args
        hb = tm // POOL_HALO
        last = n_tok // POOL_HALO - 1
        args += [x, x, pool_w.astype(BF16), pool_b.astype(F32).reshape(1, d),
                 pool_scale.astype(F32).reshape(1, d)]
        specs += [pl.BlockSpec((POOL_HALO, d), lambda i: (jnp.maximum(i * hb - 1, 0), 0)),
                  pl.BlockSpec((POOL_HALO, d), lambda i: (jnp.minimum((i + 1) * hb, last), 0)),
                  _const_spec(pool_w.shape), _const_spec((1, d)), _const_spec((1, d))]
        scratch.append(pltpu.VMEM((tm + 2 * POOL_HALO, d), F32))
    elif mixer == "s5":
        ys, d_skip, glu_w, glu_b = mixer_args
        args += [ys, d_skip.astype(F32).reshape(1, d), glu_w.astype(BF16),
                 glu_b.astype(F32).reshape(1, 2 * d)]
        specs += [chunk_spec, _const_spec((1, d)), _const_spec((d, 2 * d)), _const_spec((1, 2 * d))]
        scratch.append(pltpu.VMEM((d // LANES, tm // CHUNK * CHUNK_PITCH, LANES), F32))
    args += [wg, wu, wom, wot]
    specs += [_const_spec(w.shape) for w in (wg, wu, wom, wot)]
    if final_g is not None:
        args.append(final_g.astype(F32).reshape(1, d))
        specs.append(_const_spec((1, d)))
    if next_ffn is not None:
        cast_args, cast_in, cast_out, cast_shape = _cast_io(*next_ffn, n_tok // tm)
        args += cast_args
        specs += cast_in
    out_shape = [jax.ShapeDtypeStruct((n_tok, d), F32)]
    out_specs = [row_spec]
    if emit_h:
        out_shape.append(jax.ShapeDtypeStruct((n_tok // seq_len, CHUNK, seq_len // CHUNK, d), F32))
        out_specs.append(chunk_spec)
        scratch.append(pltpu.VMEM((d // LANES, tm // CHUNK * CHUNK_PITCH, LANES), F32))
    if next_ffn is not None:
        out_shape += cast_shape
        out_specs += cast_out
    outs = pl.pallas_call(
        functools.partial(_layer_kernel, mixer, emit_h, final_g is not None,
                          None if next_ffn is None else len(next_ffn[4]), seq_len, n_sub),
        grid=(n_tok // tm,),
        in_specs=specs,
        out_specs=out_specs,
        out_shape=out_shape,
        scratch_shapes=scratch,
        compiler_params=pltpu.CompilerParams(
            dimension_semantics=("arbitrary",), vmem_limit_bytes=VMEM_LIMIT),
        name="layer_" + (mixer or "ffn"),
    )(*args)
    n_act = 2 if emit_h else 1
    acts = tuple(outs[:n_act]) if emit_h else outs[0]
    if next_ffn is None:
        return acts
    _, ffp_next, k_main = _ffn_dims(next_ffn[0])
    tail = _w_out_tail(next_ffn[1], next_ffn[2], next_ffn[3], k_main, ffp_next)
    return acts, (*outs[n_act:n_act + 3], tail), tuple(outs[n_act + 3:])


def _cmul(ar, ai, br, bi):
    return ar * br - ai * bi, ar * bi + ai * br


def _s5_prep_kernel(n_levels, lr_ref, li_ref, ldt_ref, bre_ref, bim_ref, cre_ref, cim_ref,
                    t_ref, bz_ref, cx_ref, dec_ref):
    gb = lr_ref.shape[1]
    w = t_ref.shape[1]
    chunk = w // SSM_H
    n_dbl = chunk.bit_length() - 1
    assert chunk == 1 << n_dbl
    lane = lax.broadcasted_iota(jnp.int32, (1, LANES), 1)
    re_half = lane < SSM_P
    wlane = lax.broadcasted_iota(jnp.int32, (1, w), 1)
    step_e = (1 << lax.broadcasted_iota(jnp.int32, (SUBLANES, 1), 0)).astype(F32)
    level_e = (chunk << lax.broadcasted_iota(jnp.int32, (n_levels, 1), 0)).astype(F32)
    assert n_dbl < SUBLANES

    def flip(v):
        return jnp.concatenate([v[SSM_H * l:SSM_H * (l + 1)] for l in reversed(range(chunk))], axis=0)

    for g in range(gb):
        rs, zs, cs = [], [], []
        for direction in range(2):
            lr, li = lr_ref[direction, g], li_ref[direction, g]
            dt = jnp.exp(ldt_ref[direction, g])
            lrdt, lidt = lr * dt, li * dt

            def a_pow(e):
                mag = jnp.exp(lrdt * e)
                return mag * jnp.cos(lidt * e), mag * jnp.sin(lidt * e)

            step_r, step_i = a_pow(step_e)
            ar, ai = step_r[0:1], step_i[0:1]
            den = lr * lr + li * li
            fr = ((ar - 1.0) * lr + ai * li) / den
            fi = (ai * lr - (ar - 1.0) * li) / den
            bre, bim = bre_ref[direction, g], bim_ref[direction, g]
            bbr, bbi = _cmul(fr, fi, bre, bim)
            cre16, cim16 = cre_ref[direction, g], cim_ref[direction, g]

            b_r, b_i, c_r, c_i = bbr, bbi, cre16, cim16
            for k in range(n_dbl):
                sr, si = step_r[k:k + 1], step_i[k:k + 1]
                nr, ni = _cmul(b_r, b_i, sr, si)
                b_r, b_i = jnp.concatenate([b_r, nr], axis=0), jnp.concatenate([b_i, ni], axis=0)
                nr, ni = _cmul(c_r, c_i, sr, si)
                c_r, c_i = jnp.concatenate([c_r, nr], axis=0), jnp.concatenate([c_i, ni], axis=0)
            top_r, top_i = _cmul(cre16, cim16, step_r[n_dbl:n_dbl + 1], step_i[n_dbl:n_dbl + 1])
            c1_r = jnp.concatenate([c_r[SSM_H:], top_r], axis=0)
            c1_i = jnp.concatenate([c_i[SSM_H:], top_i], axis=0)

            def mine(v):
                return jnp.where(re_half if g % 2 == 0 else ~re_half, v, 0.0)

            if direction == 0:
                b_r, b_i = flip(b_r), flip(b_i)
            else:
                c1_r, c1_i, c_r, c_i = flip(c1_r), flip(c1_i), flip(c_r), flip(c_i)
            zs += [mine(b_r), mine(b_i)]
            cs += [mine(c1_r).T, mine(-c1_i).T]
            gmat = jnp.where(re_half, c_r, -c_i).T
            bbt = jnp.where(re_half, bbr, bbi)
            rs.append(_dot3(bbt, gmat))
            er, ei = a_pow(level_e)
            base = 2 * n_levels * direction
            dec_ref[g, base:base + n_levels, :] = er
            dec_ref[g, base + n_levels:base + 2 * n_levels, :] = ei
        for i in range(chunk):
            s = SSM_H * i
            fwd_rows = jnp.where(wlane >= s, pltpu.roll(rs[0], s, 1) if s else rs[0], 0.0)
            e = (s + SSM_H) % w
            bwd_rows = jnp.where(wlane < s + SSM_H, pltpu.roll(rs[1], e, 1) if e else rs[1], 0.0)
            t_ref[g, s:s + SSM_H, :] = (fwd_rows + bwd_rows).astype(BF16)
        bz_ref[g] = jnp.concatenate(zs, axis=1).astype(BF16)
        cx_ref[g] = jnp.concatenate(cs, axis=0).astype(BF16)


def _s5_prep(lam_re, lam_im, log_dt, b_re, b_im, c_re, c_im, n_levels):
    n_dir, n_grp, n_state = lam_re.shape
    assert n_dir == 2 and n_state == SSM_P and b_re.shape[-1] == SSM_H
    w = CHUNK * SSM_H
    gb = GROUPS_PER_SLAB
    assert n_grp % gb == 0 and n_levels % SUBLANES == 0

    def twice(v):
        return jnp.concatenate([v, v], axis=-1).astype(F32)

    lr = twice(lam_re)[:, :, None, :]
    li = twice(lam_im)[:, :, None, :]
    ldt = jnp.broadcast_to(log_dt.astype(F32)[:, :, None, None], (n_dir, n_grp, 1, LANES))
    bre, bim = twice(jnp.swapaxes(b_re, 2, 3)), twice(jnp.swapaxes(b_im, 2, 3))
    cre, cim = twice(c_re), twice(c_im)
    n_dec = 2 * 2 * n_levels
    row = lambda r: pl.BlockSpec((n_dir, gb, r, LANES), lambda i: (0, i, 0, 0))
    return pl.pallas_call(
        functools.partial(_s5_prep_kernel, n_levels),
        grid=(n_grp // gb,),
        in_specs=[row(1), row(1), row(1), row(SSM_H), row(SSM_H), row(SSM_H), row(SSM_H)],
        out_specs=[pl.BlockSpec((gb, w, w), lambda i: (i, 0, 0)),
                   pl.BlockSpec((gb, w, 4 * LANES), lambda i: (i, 0, 0)),
                   pl.BlockSpec((gb, 4 * LANES, w), lambda i: (i, 0, 0)),
                   pl.BlockSpec((gb, n_dec, LANES), lambda i: (i, 0, 0))],
        out_shape=[jax.ShapeDtypeStruct((n_grp, w, w), BF16),
                   jax.ShapeDtypeStruct((n_grp, w, 4 * LANES), BF16),
                   jax.ShapeDtypeStruct((n_grp, 4 * LANES, w), BF16),
                   jax.ShapeDtypeStruct((n_grp, n_dec, LANES), F32)],
        compiler_params=pltpu.CompilerParams(
            dimension_semantics=("arbitrary",), vmem_limit_bytes=VMEM_LIMIT),
        name="s5_prep",
    )(lr, li, ldt, bre, bim, cre, cim)


def _chunk_scan(zr, zi, a_re, a_im, reverse):
    n_rows = zr.shape[0]
    assert n_rows % SUBLANES == 0
    row = lax.broadcasted_iota(jnp.int32, (SUBLANES, 1), 0)
    edge = SUBLANES - 1 if reverse else 0

    def shifted(v, dist):
        if reverse:
            return jnp.where(row < SUBLANES - dist, pltpu.roll(v, SUBLANES - dist, 0), 0.0)
        return jnp.where(row >= dist, pltpu.roll(v, dist, 0), 0.0)

    def local_scan(vr, vi):
        for k in range(SUBLANES.bit_length() - 1):
            hr, hi = shifted(vr, 1 << k), shifted(vi, 1 << k)
            vr, vi = vr + (hr * a_re[k] - hi * a_im[k]), vi + (hr * a_im[k] + hi * a_re[k])
        return vr, vi

    def spread(v, r):
        return jnp.broadcast_to(v[r:r + 1, :], v.shape)

    unit = jnp.where(row == edge, 1.0, 0.0) + jnp.zeros((SUBLANES, zr.shape[1]), F32)
    pr, pi = local_scan(unit, jnp.zeros_like(unit))
    pr, pi = pr * a_re[0] - pi * a_im[0], pr * a_im[0] + pi * a_re[0]
    n_tiles = n_rows // SUBLANES
    order = range(n_tiles - 1, -1, -1) if reverse else range(n_tiles)
    cr = ci = jnp.zeros_like(unit)
    out_r, out_i = [None] * n_tiles, [None] * n_tiles
    for v in order:
        rows = slice(SUBLANES * v, SUBLANES * (v + 1))
        sr, si = local_scan(zr[rows], zi[rows])
        sr, si = sr + (pr * cr - pi * ci), si + (pr * ci + pi * cr)
        out_r[v] = jnp.where(row == edge, cr, shifted(sr, 1))
        out_i[v] = jnp.where(row == edge, ci, shifted(si, 1))
        cr, ci = spread(sr, SUBLANES - 1 - edge), spread(si, SUBLANES - 1 - edge)
    return jnp.concatenate(out_r, axis=0), jnp.concatenate(out_i, axis=0)


def _slab_permutation():
    n = GROUPS_PER_SLAB
    idx = np.arange(n * LANES)
    a, b, h = idx // LANES, (idx % LANES) // SSM_H, idx % SSM_H
    perm = np.zeros((n * LANES, n * LANES), np.float32)
    perm[idx, b * LANES + a * SSM_H + h] = 1.0
    return jnp.asarray(perm, BF16)


def _unit_transpose(vs):
    vs = list(vs)
    n = len(vs)
    assert n * SSM_H == LANES
    unit = lax.broadcasted_iota(jnp.int32, (1, LANES), 1) // SSM_H
    d = n // 2
    while d:
        upper = (unit & d) != 0
        for i in range(n):
            if i & d:
                continue
            a, b = vs[i], vs[i + d]
            vs[i] = jnp.where(upper, pltpu.roll(b, d * SSM_H, 1), a)
            vs[i + d] = jnp.where(upper, b, pltpu.roll(a, LANES - d * SSM_H, 1))
        d //= 2
    return vs


def _s5_core_kernel(n_levels, h_ref, perm_ref, t_ref, bz_ref, cx_ref, dec_ref, y_ref, u_ref, yg_ref):
    n_grp = GROUPS_PER_SLAB
    perm = perm_ref[...]
    for o in range(CHUNK // n_grp):
        v = jnp.concatenate([h_ref[n_grp * o + l8] for l8 in range(n_grp)], axis=1)
        uv = _dot(v.astype(BF16), perm).astype(BF16)
        for g in range(n_grp):
            u_ref[g, :, LANES * o:LANES * (o + 1)] = uv[:, LANES * g:LANES * (g + 1)]
    first_half = lax.broadcasted_iota(jnp.int32, (1, LANES), 1) < SSM_P
    for g0 in range(0, n_grp, 2):
        pair = (g0, g0 + 1)
        z = _dot(u_ref[g0], bz_ref[g0]) + _dot(u_ref[g0 + 1], bz_ref[g0 + 1])
        xs = []
        for direction in range(2):
            base = 2 * n_levels * direction
            a_re, a_im = ([jnp.where(first_half, dec_ref[g0, r:r + 1, :], dec_ref[g0 + 1, r:r + 1, :])
                           for r in range(lo, lo + n_levels)] for lo in (base, base + n_levels))
            cols = 2 * LANES * direction
            xs += _chunk_scan(z[:, cols:cols + LANES], z[:, cols + LANES:cols + 2 * LANES],
                              a_re, a_im, direction == 1)
        x_all = jnp.concatenate(xs, axis=1).astype(BF16)
        for g in pair:
            yg_ref[g] = _dot(u_ref[g], t_ref[g]) + _dot(x_all, cx_ref[g])
    for o in range(CHUNK // n_grp):
        ys = _unit_transpose([yg_ref[g, :, LANES * o:LANES * (o + 1)] for g in range(n_grp)])
        for l8 in range(n_grp):
            y_ref[n_grp * o + l8] = ys[l8]


def _s5_mix(h, lam_re, lam_im, log_dt, b_re, b_im, c_re, c_im):
    bsz, _, rows, d = h.shape
    w = CHUNK * SSM_H
    gb = GROUPS_PER_SLAB
    assert h.shape[1] == CHUNK and CHUNK % gb == 0 and d % LANES == 0
    n_levels = -(-max(1, (rows - 1).bit_length()) // SUBLANES) * SUBLANES
    t_mat, bz, cx, dec = _s5_prep(lam_re, lam_im, log_dt, b_re, b_im, c_re, c_im, n_levels)
    perm = _slab_permutation()
    slab = pl.BlockSpec((None, CHUNK, rows, LANES), lambda s, b: (b, 0, 0, s))
    grp = lambda r, c: pl.BlockSpec((gb, r, c), lambda s, b: (s, 0, 0))
    return pl.pallas_call(
        functools.partial(_s5_core_kernel, n_levels),
        grid=(d // LANES, bsz),
        in_specs=[slab, _const_spec(perm.shape), grp(w, w), grp(w, 4 * LANES), grp(4 * LANES, w),
                  grp(dec.shape[1], LANES)],
        out_specs=slab,
        out_shape=jax.ShapeDtypeStruct(h.shape, F32),
        scratch_shapes=[pltpu.VMEM((gb, rows, w), BF16), pltpu.VMEM((gb, rows, w), F32)],
        compiler_params=pltpu.CompilerParams(
            dimension_semantics=("arbitrary", "arbitrary"), vmem_limit_bytes=VMEM_LIMIT),
        name="s5_core",
    )(h, perm, t_mat, bz, cx, dec)


def kernel(x, c, mod_w, mod_b, norm_g, ffn_w_in, ffn_w_out, pool_w, pool_b, pool_scale,
           ssm_lam_re, ssm_lam_im, ssm_log_dt, ssm_b_re, ssm_b_im, ssm_c_re, ssm_c_im,
           ssm_d, glu_w, glu_b, final_g):
    bsz, seq_len, d = x.shape
    depth = mod_w.shape[0]
    n_mixers = 2
    mods = _modulation(c, mod_w, mod_b)
    xf = x.astype(F32).reshape(bsz * seq_len, d)
    _, ffp, k_main = _ffn_dims(ffn_w_in)
    weights = (*_cast_first_weights(ffn_w_in, ffn_w_out, 0, 0), _w_out_tail(ffn_w_out, 0, 0, k_main, ffp))
    for i in range(depth):
        j = i // n_mixers
        is_s5 = i % n_mixers == 1
        last = i == depth - 1
        extras = (glu_w[j],) if is_s5 else ()
        acts, weights, extras = _layer_call(xf, mods[i], norm_g[i], weights, seq_len, emit_h=is_s5,
                                            next_ffn=(ffn_w_in, ffn_w_out, i, 1, extras))
        if is_s5:
            xf, h = acts
            ys = _s5_mix(h, ssm_lam_re[j], ssm_lam_im[j], ssm_log_dt[j], ssm_b_re[j],
                         ssm_b_im[j], ssm_c_re[j], ssm_c_im[j])
            mixer, mixer_args = "s5", (ys, ssm_d[j], extras[0], glu_b[j])
        else:
            xf = acts
            mixer, mixer_args = "pool", (pool_w[j], pool_b[j], pool_scale[j])
        res = _layer_call(xf, mods[i], norm_g[i], weights, seq_len, mixer=mixer, mixer_args=mixer_args,
                          final_g=final_g if last else None,
                          next_ffn=None if last else (ffn_w_in, ffn_w_out, i + 1, 0, ()))
        xf, weights = (res, None) if last else res[:2]
    return xf.reshape(bsz, seq_len, d).astype(x.dtype)
```

```python
import functools
import math

import numpy as np

import jax
import jax.numpy as jnp
from jax import lax
from jax.experimental import pallas as pl
from jax.experimental.pallas import tpu as pltpu

F32 = jnp.float32
BF16 = jnp.bfloat16

EPS = 1e-6
N_SUB = 3
N_MOD = 3
POOL_WINDOWS = (2, 4, 8, 16)
POOL_HALO = 8
SSM_H = 16
SSM_P = 64
CHUNK = 32
CHUNK_PITCH = CHUNK + 4
FF_CHUNK = 256
GLU_CHUNK = 512
MATMUL_PHASE = "matmul phase"
LANES = 128
CAST_BLOCKS = 16
SUBLANES = 8
GROUPS_PER_SLAB = LANES // SSM_H
VMEM_LIMIT = 56 * 1024 * 1024


def _sigmoid(v):
    return 1.0 / (1.0 + jnp.exp(-v))


def _split3(a):
    hi = a.astype(BF16)
    lo = (a - hi.astype(F32)).astype(BF16)
    return hi, lo


def _dot(a, b):
    return jnp.dot(a, b, preferred_element_type=F32)


def _dot3(a, b):
    ah, al = _split3(a)
    bh, bl = _split3(b)
    return _dot(ah, bh) + (_dot(ah, bl) + _dot(al, bh))


def _mod_kernel(c_ref, w_ref, b_ref, o_ref):
    c = c_ref[...]
    cond = c * _sigmoid(c)
    o_ref[0] = _dot3(cond, w_ref[0]) + b_ref[0]


def _modulation(c, mod_w, mod_b):
    depth, d, n = mod_w.shape
    bsz = c.shape[0]
    rows = SUBLANES
    tn = n // 4 if n % (4 * LANES) == 0 else n
    assert n % tn == 0 and bsz <= rows
    c_pad = jnp.zeros((rows, d), F32).at[:bsz].set(c.astype(F32))
    out = pl.pallas_call(
        _mod_kernel,
        grid=(depth, n // tn),
        in_specs=[
            pl.BlockSpec((rows, d), lambda i, j: (0, 0)),
            pl.BlockSpec((1, d, tn), lambda i, j: (i, 0, j)),
            pl.BlockSpec((1, 1, tn), lambda i, j: (i, 0, j)),
        ],
        out_specs=pl.BlockSpec((1, rows, tn), lambda i, j: (i, 0, j)),
        out_shape=jax.ShapeDtypeStruct((depth, rows, n), F32),
        compiler_params=pltpu.CompilerParams(
            dimension_semantics=("arbitrary", "arbitrary"), vmem_limit_bytes=VMEM_LIMIT),
        name="adaln_modulation",
    )(c_pad, mod_w.astype(F32), mod_b.astype(F32).reshape(depth, 1, n))
    return out[:, :bsz].reshape(depth, bsz, N_SUB * N_MOD, d)


def _rms(x, g):
    return x * lax.rsqrt(jnp.mean(x * x, axis=-1, keepdims=True) + EPS) * g


def _modnorm(x, g_ref, mod, sub):
    r = N_MOD * sub
    shift, scale = mod[r:r + 1], mod[r + 1:r + 2]
    return _rms(x, g_ref[sub:sub + 1, :]) * (1.0 + scale) + shift


def _swiglu_step(x, g_ref, mod, sub, wg_ref, wu_ref, wom_ref, wot_ref, act_ref):
    h = _modnorm(x, g_ref, mod, sub).astype(BF16)
    yield MATMUL_PHASE
    n_chunks = act_ref.shape[1] // FF_CHUNK
    for k in range(n_chunks):
        cols = slice(FF_CHUNK * k, FF_CHUNK * (k + 1))
        gate = _dot(h, wg_ref[:, cols])
        up = _dot(h, wu_ref[:, cols])
        act_ref[:, cols] = (gate * _sigmoid(gate) * up).astype(BF16)
        yield
    k_main = wom_ref.shape[0]
    out = _dot(act_ref[:, :k_main], wom_ref[...])
    if act_ref.shape[1] > k_main:
        out = out + _dot(act_ref[:, k_main:], wot_ref[...])
    r = N_MOD * sub + 2
    return x + (0.5 * mod[r:r + 1]) * out


def _gelu_tanh(v):
    return 0.5 * v * (1.0 + jnp.tanh(math.sqrt(2.0 / math.pi) * (v + 0.044715 * (v * v * v))))


def _pool_fill(x_ref, xp_ref, xn_ref, g_ref, mod, hs_ref, seq_len):
    tm = x_ref.shape[0]
    tiles_per_seq = seq_len // tm
    j = pl.program_id(0) % tiles_per_seq
    hp = _modnorm(xp_ref[...], g_ref, mod, 1)
    hn = _modnorm(xn_ref[...], g_ref, mod, 1)
    hs_ref[0:POOL_HALO, :] = jnp.where(j > 0, hp, 0.0)
    hs_ref[POOL_HALO:POOL_HALO + tm, :] = _modnorm(x_ref[...], g_ref, mod, 1)
    hs_ref[POOL_HALO + tm:, :] = jnp.where(j < tiles_per_seq - 1, hn, 0.0)


def _pool_update(x, row0, tm, mod, pw_ref, pb_ref, ps_ref, hs_ref, seq_len):
    ts, d = x.shape
    cg = d // len(POOL_WINDOWS)
    j = pl.program_id(0) % (seq_len // tm)
    t = j * tm + row0 + lax.broadcasted_iota(jnp.int32, (ts, 1), 0)
    n_ext = ts + 2 * POOL_HALO
    ys = []
    for gi, window in enumerate(POOL_WINDOWS):
        left = window // 2
        right = window - 1 - left
        cols = slice(gi * cg, (gi + 1) * cg)
        assert window & (window - 1) == 0 and right < POOL_HALO
        ext = hs_ref[row0:row0 + n_ext, cols]
        run = ext
        span = 1
        while span < window:
            run = run + pltpu.roll(run, span, 0)
            span *= 2
        if right:
            run = pltpu.roll(run, n_ext - right, 0)
        total = run[POOL_HALO:POOL_HALO + ts]
        lo = jnp.maximum(t - left, 0)
        hi = jnp.minimum(t + right, seq_len - 1)
        count = (hi - lo + 1).astype(F32)
        p = total / count - ext[POOL_HALO:POOL_HALO + ts]
        ys.append(_dot(p.astype(BF16), pw_ref[gi]))
        yield
    y = (jnp.concatenate(ys, axis=1) + pb_ref[...]) * ps_ref[...]
    return x + mod[N_MOD + 2:N_MOD + 3] * y


def _s5_tail_update(x, ys, g_ref, mod, dskip_ref, gw_ref, gb_ref):
    d = x.shape[1]
    h = _modnorm(x, g_ref, mod, 1)
    y = _gelu_tanh(dskip_ref[...] * h + ys).astype(BF16)
    yield
    ms = []
    for k in range(d // GLU_CHUNK):
        cv = slice(GLU_CHUNK * k, GLU_CHUNK * (k + 1))
        cg = slice(d + GLU_CHUNK * k, d + GLU_CHUNK * (k + 1))
        val = _dot(y, gw_ref[:, cv]) + gb_ref[:, cv]
        gate = _dot(y, gw_ref[:, cg]) + gb_ref[:, cg]
        ms.append(val * _sigmoid(gate))
        yield
    return x + mod[N_MOD + 2:N_MOD + 3] * jnp.concatenate(ms, axis=1)


def _cast_weight_block(n_extra, win_ref, wout_ref, *refs):
    wg_ref, wu_ref, wom_ref = refs[n_extra:n_extra + 3]
    for src, dst in zip(refs[:n_extra], refs[n_extra + 3:]):
        dst[...] = src[...].astype(BF16)
    d_ff = win_ref.shape[1] // 2
    w = win_ref[...]
    for src, dst in ((w[:, :d_ff], wg_ref), (w[:, d_ff:], wu_ref)):
        dst[:, :d_ff] = src.astype(BF16)
        if dst.shape[1] > d_ff:
            dst[:, d_ff:] = jnp.zeros((dst.shape[0], dst.shape[1] - d_ff), BF16)
    wom_ref[...] = wout_ref[...].astype(BF16)


def _layer_kernel(mixer, emit_h, final_norm, cast_next, seq_len, n_sub, *refs):
    refs = list(refs)
    x_ref, mod_ref, g_ref = refs[:3]
    del refs[:3]
    mod = mod_ref[0]
    if mixer == "pool":
        xp_ref, xn_ref, pw_ref, pb_ref, ps_ref = refs[:5]
        del refs[:5]
    elif mixer == "s5":
        ys_ref, dskip_ref, gw_ref, gb_ref = refs[:4]
        del refs[:4]
    wg_ref, wu_ref, wom_ref, wot_ref = refs[:4]
    del refs[:4]
    if final_norm:
        fg_ref = refs.pop(0)
    if cast_next is not None:
        cast_in = refs[:2 + cast_next]
        del refs[:2 + cast_next]
    o_ref = refs.pop(0)
    if emit_h:
        h_ref = refs.pop(0)
    if cast_next is not None:
        _cast_weight_block(cast_next, *cast_in, *refs[:3 + cast_next])
        del refs[:3 + cast_next]
    act_ref = refs.pop(0)
    tm = x_ref.shape[0]
    ts = tm // n_sub
    chunks = ts // CHUNK
    if mixer == "pool":
        hs_ref = refs.pop(0)
        _pool_fill(x_ref, xp_ref, xn_ref, g_ref, mod, hs_ref, seq_len)
    elif mixer == "s5":
        rows_ref = refs.pop(0)
        n_slab = rows_ref.shape[0]
        for l in range(CHUNK):
            for s in range(n_slab):
                rows_ref[s, pl.ds(l, tm // CHUNK, stride=CHUNK_PITCH), :] = (
                    ys_ref[l, :, LANES * s:LANES * (s + 1)].astype(F32))
    if emit_h:
        hrows_ref = refs.pop(0)

    def sub_tile(si):
        row0 = si * ts
        x = x_ref[row0:row0 + ts, :]
        if mixer == "pool":
            x = yield from _pool_update(x, row0, tm, mod, pw_ref, pb_ref, ps_ref, hs_ref, seq_len)
        elif mixer == "s5":
            ys = jnp.concatenate(
                [jnp.concatenate([rows_ref[s, CHUNK_PITCH * c:CHUNK_PITCH * c + CHUNK, :]
                                  for c in range(si * chunks, (si + 1) * chunks)], axis=0)
                 for s in range(n_slab)], axis=1)
            x = yield from _s5_tail_update(x, ys, g_ref, mod, dskip_ref, gw_ref, gb_ref)
        sub = 0 if mixer is None else 2
        x = yield from _swiglu_step(x, g_ref, mod, sub, wg_ref, wu_ref, wom_ref, wot_ref, act_ref.at[si])
        if final_norm:
            x = _rms(x, fg_ref[...])
        o_ref[row0:row0 + ts, :] = x
        if emit_h:
            h = _modnorm(x, g_ref, mod, 1)
            for c in range(chunks):
                base = CHUNK_PITCH * (si * chunks + c)
                for s in range(hrows_ref.shape[0]):
                    hrows_ref[s, base:base + CHUNK, :] = h[CHUNK * c:CHUNK * (c + 1), LANES * s:LANES * (s + 1)]

    tiles = [sub_tile(si) for si in range(n_sub)]
    pending = [True] * n_sub
    for marker in tiles[0]:
        if marker == MATMUL_PHASE:
            pending[0] = False
            break
    for si, tile in enumerate(tiles):
        nxt = si + 1 if si + 1 < n_sub else None
        for _ in tile:
            if nxt is not None and pending[nxt]:
                pending[nxt] = next(tiles[nxt]) != MATMUL_PHASE
        while nxt is not None and pending[nxt]:
            pending[nxt] = next(tiles[nxt]) != MATMUL_PHASE
    if emit_h:
        for l in range(CHUNK):
            for s in range(hrows_ref.shape[0]):
                h_ref[l, :, LANES * s:LANES * (s + 1)] = (
                    hrows_ref[s, pl.ds(l, tm // CHUNK, stride=CHUNK_PITCH), :].astype(BF16))


def _const_spec(shape):
    nd = len(shape)
    return pl.BlockSpec(shape, lambda *_: (0,) * nd, pipeline_mode=pl.Buffered(1))


def _ffn_dims(ffn_w_in):
    d_ff = ffn_w_in.shape[-1] // 2
    return d_ff, d_ff + -d_ff % FF_CHUNK, d_ff // FF_CHUNK * FF_CHUNK


def _cast_io(ffn_w_in, ffn_w_out, layer, which, extras, steps):
    d = ffn_w_in.shape[2]
    d_ff, ffp, k_main = _ffn_dims(ffn_w_in)
    n = 1
    while n < CAST_BLOCKS and steps % (2 * n) == 0:
        n *= 2
    assert d % (n * 2 * SUBLANES) == 0 and k_main % (n * 2 * SUBLANES) == 0
    blk = lambda i: i * n // steps
    in_specs = [pl.BlockSpec((None, None, d // n, 2 * d_ff), lambda i: (layer, which, blk(i), 0)),
                pl.BlockSpec((None, None, k_main // n, d), lambda i: (layer, which, blk(i), 0))]
    out_specs = [pl.BlockSpec((d // n, ffp), lambda i: (blk(i), 0)),
                 pl.BlockSpec((d // n, ffp), lambda i: (blk(i), 0)),
                 pl.BlockSpec((k_main // n, d), lambda i: (blk(i), 0))]
    out_shape = [jax.ShapeDtypeStruct((d, ffp), BF16), jax.ShapeDtypeStruct((d, ffp), BF16),
                 jax.ShapeDtypeStruct((k_main, d), BF16)]
    for m in extras:
        rows, cols = m.shape
        assert rows % (n * 2 * SUBLANES) == 0
        in_specs.append(pl.BlockSpec((rows // n, cols), lambda i: (blk(i), 0)))
        out_specs.append(pl.BlockSpec((rows // n, cols), lambda i: (blk(i), 0)))
        out_shape.append(jax.ShapeDtypeStruct(m.shape, BF16))
    args = [ffn_w_in.astype(F32), ffn_w_out.astype(F32)] + [m.astype(F32) for m in extras]
    return args, in_specs, out_specs, out_shape


def _w_out_tail(ffn_w_out, layer, which, k_main, ffp):
    tail = lax.optimization_barrier(ffn_w_out[layer, which, k_main:, :]).astype(BF16)
    return jnp.pad(tail, ((0, ffp - k_main - tail.shape[0]), (0, 0)))


def _cast_first_weights(ffn_w_in, ffn_w_out, layer, which):
    args, in_specs, out_specs, out_shape = _cast_io(ffn_w_in, ffn_w_out, layer, which, (), CAST_BLOCKS)
    return pl.pallas_call(
        functools.partial(_cast_weight_block, 0),
        grid=(CAST_BLOCKS,),
        in_specs=in_specs,
        out_specs=out_specs,
        out_shape=out_shape,
        compiler_params=pltpu.CompilerParams(
            dimension_semantics=("arbitrary",), vmem_limit_bytes=VMEM_LIMIT),
        name="weight_cast",
    )(*args)


def _tile_plan(seq_len, mixer, emit_h):
    tm = min(512 if emit_h else 1024, seq_len)
    return tm, max(1, tm // (512 if mixer is None else 256))


def _layer_call(x, mod, norm_g, weights, seq_len, *, mixer=None, mixer_args=(),
                emit_h=False, final_g=None, next_ffn=None):
    n_tok, d = x.shape
    tm, n_sub = _tile_plan(seq_len, mixer, emit_h)
    assert seq_len % tm == 0 and tm % (n_sub * CHUNK * SUBLANES) == 0
    tiles_per_seq = seq_len // tm
    wg, wu, wom, wot = weights
    ffp = wg.shape[1]
    row_spec = pl.BlockSpec((tm, d), lambda i: (i, 0))
    chunk_spec = pl.BlockSpec((None, CHUNK, tm // CHUNK, d),
                              lambda i: (i // tiles_per_seq, 0, i % tiles_per_seq, 0))
    args = [x, mod, norm_g.astype(F32)]
    specs = [row_spec,
             pl.BlockSpec((1, N_SUB * N_MOD, d), lambda i: (i // tiles_per_seq, 0, 0)),
             _const_spec((N_SUB, d))]
    scratch = [pltpu.VMEM((n_sub, tm // n_sub, ffp), BF16)]
    if mixer == "pool":
        pool_w, pool_b, pool_scale = mixer_args
        hb = tm // POOL_HALO
        last = n_tok // POOL_HALO - 1
        args += [x, x, pool_w.astype(BF16), pool_b.astype(F32).reshape(1, d),
                 pool_scale.astype(F32).reshape(1, d)]
        specs += [pl.BlockSpec((POOL_HALO, d), lambda i: (jnp.maximum(i * hb - 1, 0), 0)),
                  pl.BlockSpec((POOL_HALO, d), lambda i: (jnp.minimum((i + 1) * hb, last), 0)),
                  _const_spec(pool_w.shape), _const_spec((1, d)), _const_spec((1, d))]
        scratch.append(pltpu.VMEM((tm + 2 * POOL_HALO, d), F32))
    elif mixer == "s5":
        ys, d_skip, glu_w, glu_b = mixer_args
        args += [ys, d_skip.astype(F32).reshape(1, d), glu_w.astype(BF16),
                 glu_b.astype(F32).reshape(1, 2 * d)]
        specs += [chunk_spec, _const_spec((1, d)), _const_spec((d, 2 * d)), _const_spec((1, 2 * d))]
        scratch.append(pltpu.VMEM((d // LANES, tm // CHUNK * CHUNK_PITCH, LANES), F32))
    args += [wg, wu, wom, wot]
    specs += [_const_spec(w.shape) for w in (wg, wu, wom, wot)]
    if final_g is not None:
        args.append(final_g.astype(F32).reshape(1, d))
        specs.append(_const_spec((1, d)))
    if next_ffn is not None:
        cast_args, cast_in, cast_out, cast_shape = _cast_io(*next_ffn, n_tok // tm)
        args += cast_args
        specs += cast_in
    out_shape = [jax.ShapeDtypeStruct((n_tok, d), F32)]
    out_specs = [row_spec]
    if emit_h:
        out_shape.append(jax.ShapeDtypeStruct((n_tok // seq_len, CHUNK, seq_len // CHUNK, d), BF16))
        out_specs.append(chunk_spec)
        scratch.append(pltpu.VMEM((d // LANES, tm // CHUNK * CHUNK_PITCH, LANES), F32))
    if next_ffn is not None:
        out_shape += cast_shape
        out_specs += cast_out
    outs = pl.pallas_call(
        functools.partial(_layer_kernel, mixer, emit_h, final_g is not None,
                          None if next_ffn is None else len(next_ffn[4]), seq_len, n_sub),
        grid=(n_tok // tm,),
        in_specs=specs,
        out_specs=out_specs,
        out_shape=out_shape,
        scratch_shapes=scratch,
        compiler_params=pltpu.CompilerParams(
            dimension_semantics=("arbitrary",), vmem_limit_bytes=VMEM_LIMIT),
        name="layer_" + (mixer or "ffn"),
    )(*args)
    n_act = 2 if emit_h else 1
    acts = tuple(outs[:n_act]) if emit_h else outs[0]
    if next_ffn is None:
        return acts
    _, ffp_next, k_main = _ffn_dims(next_ffn[0])
    tail = _w_out_tail(next_ffn[1], next_ffn[2], next_ffn[3], k_main, ffp_next)
    return acts, (*outs[n_act:n_act + 3], tail), tuple(outs[n_act + 3:])


def _cmul(ar, ai, br, bi):
    return ar * br - ai * bi, ar * bi + ai * br


def _s5_prep_kernel(n_levels, lr_ref, li_ref, ldt_ref, bre_ref, bim_ref, cre_ref, cim_ref,
                    t_ref, bz_ref, cx_ref, dec_ref):
    gb = lr_ref.shape[1]
    w = t_ref.shape[1]
    chunk = w // SSM_H
    n_dbl = chunk.bit_length() - 1
    assert chunk == 1 << n_dbl
    lane = lax.broadcasted_iota(jnp.int32, (1, LANES), 1)
    re_half = lane < SSM_P
    wlane = lax.broadcasted_iota(jnp.int32, (1, w), 1)
    step_e = (1 << lax.broadcasted_iota(jnp.int32, (SUBLANES, 1), 0)).astype(F32)
    level_e = (chunk << lax.broadcasted_iota(jnp.int32, (n_levels, 1), 0)).astype(F32)
    assert n_dbl < SUBLANES

    def flip(v):
        return jnp.concatenate([v[SSM_H * l:SSM_H * (l + 1)] for l in reversed(range(chunk))], axis=0)

    for g in range(gb):
        rs, zs, cs = [], [], []
        for direction in range(2):
            lr, li = lr_ref[direction, g], li_ref[direction, g]
            dt = jnp.exp(ldt_ref[direction, g])
            lrdt, lidt = lr * dt, li * dt

            def a_pow(e):
                mag = jnp.exp(lrdt * e)
                return mag * jnp.cos(lidt * e), mag * jnp.sin(lidt * e)

            step_r, step_i = a_pow(step_e)
            ar, ai = step_r[0:1], step_i[0:1]
            den = lr * lr + li * li
            fr = ((ar - 1.0) * lr + ai * li) / den
            fi = (ai * lr - (ar - 1.0) * li) / den
            bre, bim = bre_ref[direction, g], bim_ref[direction, g]
            bbr, bbi = _cmul(fr, fi, bre, bim)
            cre16, cim16 = cre_ref[direction, g], cim_ref[direction, g]

            b_r, b_i, c_r, c_i = bbr, bbi, cre16, cim16
            for k in range(n_dbl):
                sr, si = step_r[k:k + 1], step_i[k:k + 1]
                nr, ni = _cmul(b_r, b_i, sr, si)
                b_r, b_i = jnp.concatenate([b_r, nr], axis=0), jnp.concatenate([b_i, ni], axis=0)
                nr, ni = _cmul(c_r, c_i, sr, si)
                c_r, c_i = jnp.concatenate([c_r, nr], axis=0), jnp.concatenate([c_i, ni], axis=0)
            top_r, top_i = _cmul(cre16, cim16, step_r[n_dbl:n_dbl + 1], step_i[n_dbl:n_dbl + 1])
            c1_r = jnp.concatenate([c_r[SSM_H:], top_r], axis=0)
            c1_i = jnp.concatenate([c_i[SSM_H:], top_i], axis=0)

            def mine(v):
                return jnp.where(re_half if g % 2 == 0 else ~re_half, v, 0.0)

            if direction == 0:
                b_r, b_i = flip(b_r), flip(b_i)
            else:
                c1_r, c1_i, c_r, c_i = flip(c1_r), flip(c1_i), flip(c_r), flip(c_i)
            zs += [mine(b_r), mine(b_i)]
            cs += [mine(c1_r).T, mine(-c1_i).T]
            gmat = jnp.where(re_half, c_r, -c_i).T
            bbt = jnp.where(re_half, bbr, bbi)
            rs.append(_dot3(bbt, gmat))
            er, ei = a_pow(level_e)
            base = 2 * n_levels * direction
            dec_ref[g, base:base + n_levels, :] = er
            dec_ref[g, base + n_levels:base + 2 * n_levels, :] = ei
        for i in range(chunk):
            s = SSM_H * i
            fwd_rows = jnp.where(wlane >= s, pltpu.roll(rs[0], s, 1) if s else rs[0], 0.0)
            e = (s + SSM_H) % w
            bwd_rows = jnp.where(wlane < s + SSM_H, pltpu.roll(rs[1], e, 1) if e else rs[1], 0.0)
            t_ref[g, s:s + SSM_H, :] = (fwd_rows + bwd_rows).astype(BF16)
        bz_ref[g] = jnp.concatenate(zs, axis=1).astype(BF16)
        cx_ref[g] = jnp.concatenate(cs, axis=0).astype(BF16)


def _s5_prep(lam_re, lam_im, log_dt, b_re, b_im, c_re, c_im, n_levels):
    n_dir, n_grp, n_state = lam_re.shape
    assert n_dir == 2 and n_state == SSM_P and b_re.shape[-1] == SSM_H
    w = CHUNK * SSM_H
    gb = GROUPS_PER_SLAB
    assert n_grp % gb == 0 and n_levels % SUBLANES == 0

    def twice(v):
        return jnp.concatenate([v, v], axis=-1).astype(F32)

    lr = twice(lam_re)[:, :, None, :]
    li = twice(lam_im)[:, :, None, :]
    ldt = jnp.broadcast_to(log_dt.astype(F32)[:, :, None, None], (n_dir, n_grp, 1, LANES))
    bre, bim = twice(jnp.swapaxes(b_re, 2, 3)), twice(jnp.swapaxes(b_im, 2, 3))
    cre, cim = twice(c_re), twice(c_im)
    n_dec = 2 * 2 * n_levels
    row = lambda r: pl.BlockSpec((n_dir, gb, r, LANES), lambda i: (0, i, 0, 0))
    return pl.pallas_call(
        functools.partial(_s5_prep_kernel, n_levels),
        grid=(n_grp // gb,),
        in_specs=[row(1), row(1), row(1), row(SSM_H), row(SSM_H), row(SSM_H), row(SSM_H)],
        out_specs=[pl.BlockSpec((gb, w, w), lambda i: (i, 0, 0)),
                   pl.BlockSpec((gb, w, 4 * LANES), lambda i: (i, 0, 0)),
                   pl.BlockSpec((gb, 4 * LANES, w), lambda i: (i, 0, 0)),
                   pl.BlockSpec((gb, n_dec, LANES), lambda i: (i, 0, 0))],
        out_shape=[jax.ShapeDtypeStruct((n_grp, w, w), BF16),
                   jax.ShapeDtypeStruct((n_grp, w, 4 * LANES), BF16),
                   jax.ShapeDtypeStruct((n_grp, 4 * LANES, w), BF16),
                   jax.ShapeDtypeStruct((n_grp, n_dec, LANES), F32)],
        compiler_params=pltpu.CompilerParams(
            dimension_semantics=("arbitrary",), vmem_limit_bytes=VMEM_LIMIT),
        name="s5_prep",
    )(lr, li, ldt, bre, bim, cre, cim)


def _chunk_scan(zr, zi, a_re, a_im, reverse):
    n_rows = zr.shape[0]
    assert n_rows % SUBLANES == 0
    row = lax.broadcasted_iota(jnp.int32, (SUBLANES, 1), 0)
    edge = SUBLANES - 1 if reverse else 0

    def shifted(v, dist):
        if reverse:
            return jnp.where(row < SUBLANES - dist, pltpu.roll(v, SUBLANES - dist, 0), 0.0)
        return jnp.where(row >= dist, pltpu.roll(v, dist, 0), 0.0)

    def local_scan(vr, vi):
        for k in range(SUBLANES.bit_length() - 1):
            hr, hi = shifted(vr, 1 << k), shifted(vi, 1 << k)
            vr, vi = vr + (hr * a_re[k] - hi * a_im[k]), vi + (hr * a_im[k] + hi * a_re[k])
        return vr, vi

    def spread(v, r):
        return jnp.broadcast_to(v[r:r + 1, :], v.shape)

    unit = jnp.where(row == edge, 1.0, 0.0) + jnp.zeros((SUBLANES, zr.shape[1]), F32)
    pr, pi = local_scan(unit, jnp.zeros_like(unit))
    pr, pi = pr * a_re[0] - pi * a_im[0], pr * a_im[0] + pi * a_re[0]
    n_tiles = n_rows // SUBLANES
    order = range(n_tiles - 1, -1, -1) if reverse else range(n_tiles)
    cr = ci = jnp.zeros_like(unit)
    out_r, out_i = [None] * n_tiles, [None] * n_tiles
    for v in order:
        rows = slice(SUBLANES * v, SUBLANES * (v + 1))
        sr, si = local_scan(zr[rows], zi[rows])
        sr, si = sr + (pr * cr - pi * ci), si + (pr * ci + pi * cr)
        out_r[v] = jnp.where(row == edge, cr, shifted(sr, 1))
        out_i[v] = jnp.where(row == edge, ci, shifted(si, 1))
        cr, ci = spread(sr, SUBLANES - 1 - edge), spread(si, SUBLANES - 1 - edge)
    return jnp.concatenate(out_r, axis=0), jnp.concatenate(out_i, axis=0)


def _slab_permutation():
    n = GROUPS_PER_SLAB
    idx = np.arange(n * LANES)
    a, b, h = idx // LANES, (idx % LANES) // SSM_H, idx % SSM_H
    perm = np.zeros((n * LANES, n * LANES), np.float32)
    perm[idx, b * LANES + a * SSM_H + h] = 1.0
    return jnp.asarray(perm, BF16)


def _unit_transpose(vs):
    vs = list(vs)
    n = len(vs)
    assert n * SSM_H == LANES
    unit = lax.broadcasted_iota(jnp.int32, (1, LANES), 1) // SSM_H
    d = n // 2
    while d:
        upper = (unit & d) != 0
        for i in range(n):
            if i & d:
                continue
            a, b = vs[i], vs[i + d]
            vs[i] = jnp.where(upper, pltpu.roll(b, d * SSM_H, 1), a)
            vs[i + d] = jnp.where(upper, b, pltpu.roll(a, LANES - d * SSM_H, 1))
        d //= 2
    return vs


def _s5_core_kernel(n_levels, h_ref, perm_ref, t_ref, bz_ref, cx_ref, dec_ref, y_ref, u_ref, yg_ref):
    n_grp = GROUPS_PER_SLAB
    perm = perm_ref[...]
    for o in range(CHUNK // n_grp):
        v = jnp.concatenate([h_ref[n_grp * o + l8] for l8 in range(n_grp)], axis=1)
        uv = _dot(v, perm).astype(BF16)
        for g in range(n_grp):
            u_ref[g, :, LANES * o:LANES * (o + 1)] = uv[:, LANES * g:LANES * (g + 1)]
    first_half = lax.broadcasted_iota(jnp.int32, (1, LANES), 1) < SSM_P
    for g0 in range(0, n_grp, 2):
        pair = (g0, g0 + 1)
        z = _dot(u_ref[g0], bz_ref[g0]) + _dot(u_ref[g0 + 1], bz_ref[g0 + 1])
        xs = []
        for direction in range(2):
            base = 2 * n_levels * direction
            a_re, a_im = ([jnp.where(first_half, dec_ref[g0, r:r + 1, :], dec_ref[g0 + 1, r:r + 1, :])
                           for r in range(lo, lo + n_levels)] for lo in (base, base + n_levels))
            cols = 2 * LANES * direction
            xs += _chunk_scan(z[:, cols:cols + LANES], z[:, cols + LANES:cols + 2 * LANES],
                              a_re, a_im, direction == 1)
        x_all = jnp.concatenate(xs, axis=1).astype(BF16)
        for g in pair:
            yg_ref[g] = _dot(u_ref[g], t_ref[g]) + _dot(x_all, cx_ref[g])
    for o in range(CHUNK // n_grp):
        ys = _unit_transpose([yg_ref[g, :, LANES * o:LANES * (o + 1)] for g in range(n_grp)])
        for l8 in range(n_grp):
            y_ref[n_grp * o + l8] = ys[l8].astype(BF16)


def _s5_mix(h, lam_re, lam_im, log_dt, b_re, b_im, c_re, c_im):
    bsz, _, rows, d = h.shape
    w = CHUNK * SSM_H
    gb = GROUPS_PER_SLAB
    assert h.shape[1] == CHUNK and CHUNK % gb == 0 and d % LANES == 0
    n_levels = -(-max(1, (rows - 1).bit_length()) // SUBLANES) * SUBLANES
    t_mat, bz, cx, dec = _s5_prep(lam_re, lam_im, log_dt, b_re, b_im, c_re, c_im, n_levels)
    perm = _slab_permutation()
    slab = pl.BlockSpec((None, CHUNK, rows, LANES), lambda s, b: (b, 0, 0, s))
    grp = lambda r, c: pl.BlockSpec((gb, r, c), lambda s, b: (s, 0, 0))
    return pl.pallas_call(
        functools.partial(_s5_core_kernel, n_levels),
        grid=(d // LANES, bsz),
        in_specs=[slab, _const_spec(perm.shape), grp(w, w), grp(w, 4 * LANES), grp(4 * LANES, w),
                  grp(dec.shape[1], LANES)],
        out_specs=slab,
        out_shape=jax.ShapeDtypeStruct(h.shape, BF16),
        scratch_shapes=[pltpu.VMEM((gb, rows, w), BF16), pltpu.VMEM((gb, rows, w), F32)],
        compiler_params=pltpu.CompilerParams(
            dimension_semantics=("arbitrary", "arbitrary"), vmem_limit_bytes=VMEM_LIMIT),
        name="s5_core",
    )(h, perm, t_mat, bz, cx, dec)


def kernel(x, c, mod_w, mod_b, norm_g, ffn_w_in, ffn_w_out, pool_w, pool_b, pool_scale,
           ssm_lam_re, ssm_lam_im, ssm_log_dt, ssm_b_re, ssm_b_im, ssm_c_re, ssm_c_im,
           ssm_d, glu_w, glu_b, final_g):
    bsz, seq_len, d = x.shape
    depth = mod_w.shape[0]
    n_mixers = 2
    mods = _modulation(c, mod_w, mod_b)
    xf = x.astype(F32).reshape(bsz * seq_len, d)
    _, ffp, k_main = _ffn_dims(ffn_w_in)
    weights = (*_cast_first_weights(ffn_w_in, ffn_w_out, 0, 0), _w_out_tail(ffn_w_out, 0, 0, k_main, ffp))
    for i in range(depth):
        j = i // n_mixers
        is_s5 = i % n_mixers == 1
        last = i == depth - 1
        extras = (glu_w[j],) if is_s5 else ()
        acts, weights, extras = _layer_call(xf, mods[i], norm_g[i], weights, seq_len, emit_h=is_s5,
                                            next_ffn=(ffn_w_in, ffn_w_out, i, 1, extras))
        if is_s5:
            xf, h = acts
            ys = _s5_mix(h, ssm_lam_re[j], ssm_lam_im[j], ssm_log_dt[j], ssm_b_re[j],
                         ssm_b_im[j], ssm_c_re[j], ssm_c_im[j])
            mixer, mixer_args = "s5", (ys, ssm_d[j], extras[0], glu_b[j])
        else:
            xf = acts
            mixer, mixer_args = "pool", (pool_w[j], pool_b[j], pool_scale[j])
        res = _layer_call(xf, mods[i], norm_g[i], weights, seq_len, mixer=mixer, mixer_args=mixer_args,
                          final_g=final_g if last else None,
                          next_ffn=None if last else (ffn_w_in, ffn_w_out, i + 1, 0, ()))
        xf, weights = (res, None) if last else res[:2]
    return xf.reshape(bsz, seq_len, d).astype(x.dtype)
```

```python
import functools
import math

import numpy as np

import jax
import jax.numpy as jnp
from jax import lax
from jax.experimental import pallas as pl
from jax.experimental.pallas import tpu as pltpu

F32 = jnp.float32
BF16 = jnp.bfloat16

EPS = 1e-6
N_SUB = 3
N_MOD = 3
POOL_WINDOWS = (2, 4, 8, 16)
POOL_HALO = 8
SSM_H = 16
SSM_P = 64
CHUNK = 32
CHUNK_PITCH = CHUNK + 4
FF_CHUNK = 256
GLU_CHUNK = 512
MATMUL_PHASE = "matmul phase"
LANES = 128
CAST_BLOCKS = 16
SUBLANES = 8
GROUPS_PER_SLAB = LANES // SSM_H
VMEM_LIMIT = 56 * 1024 * 1024


def _sigmoid(v):
    return 1.0 / (1.0 + jnp.exp(-v))


def _split3(a):
    hi = a.astype(BF16)
    lo = (a - hi.astype(F32)).astype(BF16)
    return hi, lo


def _dot(a, b):
    return jnp.dot(a, b, preferred_element_type=F32)


def _dot3(a, b):
    ah, al = _split3(a)
    bh, bl = _split3(b)
    return _dot(ah, bh) + (_dot(ah, bl) + _dot(al, bh))


def _mod_kernel(c_ref, w_ref, b_ref, o_ref):
    c = c_ref[...]
    cond = c * _sigmoid(c)
    o_ref[0] = _dot3(cond, w_ref[0]) + b_ref[0]


def _modulation(c, mod_w, mod_b):
    depth, d, n = mod_w.shape
    bsz = c.shape[0]
    rows = SUBLANES
    tn = n // 4 if n % (4 * LANES) == 0 else n
    assert n % tn == 0 and bsz <= rows
    c_pad = jnp.zeros((rows, d), F32).at[:bsz].set(c.astype(F32))
    out = pl.pallas_call(
        _mod_kernel,
        grid=(depth, n // tn),
        in_specs=[
            pl.BlockSpec((rows, d), lambda i, j: (0, 0)),
            pl.BlockSpec((1, d, tn), lambda i, j: (i, 0, j)),
            pl.BlockSpec((1, 1, tn), lambda i, j: (i, 0, j)),
        ],
        out_specs=pl.BlockSpec((1, rows, tn), lambda i, j: (i, 0, j)),
        out_shape=jax.ShapeDtypeStruct((depth, rows, n), F32),
        compiler_params=pltpu.CompilerParams(
            dimension_semantics=("arbitrary", "arbitrary"), vmem_limit_bytes=VMEM_LIMIT),
        name="adaln_modulation",
    )(c_pad, mod_w.astype(F32), mod_b.astype(F32).reshape(depth, 1, n))
    return out[:, :bsz].reshape(depth, bsz, N_SUB * N_MOD, d)


def _rms(x, g):
    return x * lax.rsqrt(jnp.mean(x * x, axis=-1, keepdims=True) + EPS) * g


def _modnorm(x, g_ref, mod, sub):
    r = N_MOD * sub
    shift, scale = mod[r:r + 1], mod[r + 1:r + 2]
    return _rms(x, g_ref[sub:sub + 1, :]) * (1.0 + scale) + shift


def _swiglu_step(x, g_ref, mod, sub, wg_ref, wu_ref, wom_ref, wot_ref, act_ref):
    h = _modnorm(x, g_ref, mod, sub).astype(BF16)
    yield MATMUL_PHASE
    n_chunks = act_ref.shape[1] // FF_CHUNK
    for k in range(n_chunks):
        cols = slice(FF_CHUNK * k, FF_CHUNK * (k + 1))
        gate = _dot(h, wg_ref[:, cols])
        up = _dot(h, wu_ref[:, cols])
        act_ref[:, cols] = (gate * _sigmoid(gate) * up).astype(BF16)
        yield
    k_main = wom_ref.shape[0]
    out = _dot(act_ref[:, :k_main], wom_ref[...])
    if act_ref.shape[1] > k_main:
        out = out + _dot(act_ref[:, k_main:], wot_ref[...])
    r = N_MOD * sub + 2
    return x + (0.5 * mod[r:r + 1]) * out


def _gelu_tanh(v):
    return 0.5 * v * (1.0 + jnp.tanh(math.sqrt(2.0 / math.pi) * (v + 0.044715 * (v * v * v))))


def _pool_fill(x_ref, xp_ref, xn_ref, g_ref, mod, hs_ref, seq_len):
    tm = x_ref.shape[0]
    tiles_per_seq = seq_len // tm
    j = pl.program_id(0) % tiles_per_seq
    hp = _modnorm(xp_ref[...], g_ref, mod, 1)
    hn = _modnorm(xn_ref[...], g_ref, mod, 1)
    hs_ref[0:POOL_HALO, :] = jnp.where(j > 0, hp, 0.0)
    hs_ref[POOL_HALO:POOL_HALO + tm, :] = _modnorm(x_ref[...], g_ref, mod, 1)
    hs_ref[POOL_HALO + tm:, :] = jnp.where(j < tiles_per_seq - 1, hn, 0.0)


def _pool_update(x, row0, tm, mod, pw_ref, pb_ref, ps_ref, hs_ref, seq_len):
    ts, d = x.shape
    cg = d // len(POOL_WINDOWS)
    j = pl.program_id(0) % (seq_len // tm)
    t = j * tm + row0 + lax.broadcasted_iota(jnp.int32, (ts, 1), 0)
    n_ext = ts + 2 * POOL_HALO
    ys = []
    for gi, window in enumerate(POOL_WINDOWS):
        left = window // 2
        right = window - 1 - left
        cols = slice(gi * cg, (gi + 1) * cg)
        assert window & (window - 1) == 0 and right < POOL_HALO
        ext = hs_ref[row0:row0 + n_ext, cols]
        run = ext
        span = 1
        while span < window:
            run = run + pltpu.roll(run, span, 0)
            span *= 2
        if right:
            run = pltpu.roll(run, n_ext - right, 0)
        total = run[POOL_HALO:POOL_HALO + ts]
        lo = jnp.maximum(t - left, 0)
        hi = jnp.minimum(t + right, seq_len - 1)
        count = (hi - lo + 1).astype(F32)
        p = total / count - ext[POOL_HALO:POOL_HALO + ts]
        ys.append(_dot(p.astype(BF16), pw_ref[gi]))
        yield
    y = (jnp.concatenate(ys, axis=1) + pb_ref[...]) * ps_ref[...]
    return x + mod[N_MOD + 2:N_MOD + 3] * y


def _s5_tail_update(x, ys, g_ref, mod, dskip_ref, gw_ref, gb_ref):
    d = x.shape[1]
    h = _modnorm(x, g_ref, mod, 1)
    y = _gelu_tanh(dskip_ref[...] * h + ys).astype(BF16)
    yield
    ms = []
    for k in range(d // GLU_CHUNK):
        cv = slice(GLU_CHUNK * k, GLU_CHUNK * (k + 1))
        cg = slice(d + GLU_CHUNK * k, d + GLU_CHUNK * (k + 1))
        val = _dot(y, gw_ref[:, cv]) + gb_ref[:, cv]
        gate = _dot(y, gw_ref[:, cg]) + gb_ref[:, cg]
        ms.append(val * _sigmoid(gate))
        yield
    return x + mod[N_MOD + 2:N_MOD + 3] * jnp.concatenate(ms, axis=1)


def _cast_weight_block(n_extra, win_ref, wout_ref, *refs):
    wg_ref, wu_ref, wom_ref = refs[n_extra:n_extra + 3]
    for src, dst in zip(refs[:n_extra], refs[n_extra + 3:]):
        dst[...] = src[...].astype(BF16)
    d_ff = win_ref.shape[1] // 2
    w = win_ref[...]
    for src, dst in ((w[:, :d_ff], wg_ref), (w[:, d_ff:], wu_ref)):
        dst[:, :d_ff] = src.astype(BF16)
        if dst.shape[1] > d_ff:
            dst[:, d_ff:] = jnp.zeros((dst.shape[0], dst.shape[1] - d_ff), BF16)
    wom_ref[...] = wout_ref[...].astype(BF16)


def _layer_kernel(mixer, emit_h, final_norm, cast_next, seq_len, n_sub, *refs):
    refs = list(refs)
    x_ref, mod_ref, g_ref = refs[:3]
    del refs[:3]
    mod = mod_ref[0]
    if mixer == "pool":
        xp_ref, xn_ref, pw_ref, pb_ref, ps_ref = refs[:5]
        del refs[:5]
    elif mixer == "s5":
        ys_ref, dskip_ref, gw_ref, gb_ref = refs[:4]
        del refs[:4]
    wg_ref, wu_ref, wom_ref, wot_ref = refs[:4]
    del refs[:4]
    if final_norm:
        fg_ref = refs.pop(0)
    if cast_next is not None:
        cast_in = refs[:2 + cast_next]
        del refs[:2 + cast_next]
    o_ref = refs.pop(0)
    if emit_h:
        h_ref = refs.pop(0)
    if cast_next is not None:
        _cast_weight_block(cast_next, *cast_in, *refs[:3 + cast_next])
        del refs[:3 + cast_next]
    act_ref = refs.pop(0)
    tm = x_ref.shape[0]
    ts = tm // n_sub
    chunks = ts // CHUNK
    if mixer == "pool":
        hs_ref = refs.pop(0)
        _pool_fill(x_ref, xp_ref, xn_ref, g_ref, mod, hs_ref, seq_len)
    elif mixer == "s5":
        rows_ref = refs.pop(0)
        n_slab = rows_ref.shape[0]
        for l in range(CHUNK):
            for s in range(n_slab):
                rows_ref[s, pl.ds(l, tm // CHUNK, stride=CHUNK_PITCH), :] = (
                    ys_ref[l, :, LANES * s:LANES * (s + 1)].astype(F32))
    if emit_h:
        hrows_ref = refs.pop(0)

    def sub_tile(si):
        row0 = si * ts
        x = x_ref[row0:row0 + ts, :]
        if mixer == "pool":
            x = yield from _pool_update(x, row0, tm, mod, pw_ref, pb_ref, ps_ref, hs_ref, seq_len)
        elif mixer == "s5":
            ys = jnp.concatenate(
                [jnp.concatenate([rows_ref[s, CHUNK_PITCH * c:CHUNK_PITCH * c + CHUNK, :]
                                  for c in range(si * chunks, (si + 1) * chunks)], axis=0)
                 for s in range(n_slab)], axis=1)
            x = yield from _s5_tail_update(x, ys, g_ref, mod, dskip_ref, gw_ref, gb_ref)
        sub = 0 if mixer is None else 2
        x = yield from _swiglu_step(x, g_ref, mod, sub, wg_ref, wu_ref, wom_ref, wot_ref, act_ref.at[si])
        if final_norm:
            x = _rms(x, fg_ref[...])
        o_ref[row0:row0 + ts, :] = x
        if emit_h:
            h = _modnorm(x, g_ref, mod, 1)
            for c in range(chunks):
                base = CHUNK_PITCH * (si * chunks + c)
                for s in range(hrows_ref.shape[0]):
                    hrows_ref[s, base:base + CHUNK, :] = h[CHUNK * c:CHUNK * (c + 1), LANES * s:LANES * (s + 1)]

    tiles = [sub_tile(si) for si in range(n_sub)]
    pending = [True] * n_sub
    for marker in tiles[0]:
        if marker == MATMUL_PHASE:
            pending[0] = False
            break
    for si, tile in enumerate(tiles):
        nxt = si + 1 if si + 1 < n_sub else None
        for _ in tile:
            if nxt is not None and pending[nxt]:
                pending[nxt] = next(tiles[nxt]) != MATMUL_PHASE
        while nxt is not None and pending[nxt]:
            pending[nxt] = next(tiles[nxt]) != MATMUL_PHASE
    if emit_h:
        for l in range(CHUNK):
            for s in range(hrows_ref.shape[0]):
                h_ref[l, :, LANES * s:LANES * (s + 1)] = (
                    hrows_ref[s, pl.ds(l, tm // CHUNK, stride=CHUNK_PITCH), :].astype(BF16))


def _const_spec(shape):
    nd = len(shape)
    return pl.BlockSpec(shape, lambda *_: (0,) * nd, pipeline_mode=pl.Buffered(1))


def _ffn_dims(ffn_w_in):
    d_ff = ffn_w_in.shape[-1] // 2
    return d_ff, d_ff + -d_ff % FF_CHUNK, d_ff // FF_CHUNK * FF_CHUNK


def _cast_io(ffn_w_in, ffn_w_out, layer, which, extras, steps):
    d = ffn_w_in.shape[2]
    d_ff, ffp, k_main = _ffn_dims(ffn_w_in)
    n = 1
    while n < CAST_BLOCKS and steps % (2 * n) == 0:
        n *= 2
    assert d % (n * 2 * SUBLANES) == 0 and k_main % (n * 2 * SUBLANES) == 0
    blk = lambda i: i * n // steps
    in_specs = [pl.BlockSpec((None, None, d // n, 2 * d_ff), lambda i: (layer, which, blk(i), 0)),
                pl.BlockSpec((None, None, k_main // n, d), lambda i: (layer, which, blk(i), 0))]
    out_specs = [pl.BlockSpec((d // n, ffp), lambda i: (blk(i), 0)),
                 pl.BlockSpec((d // n, ffp), lambda i: (blk(i), 0)),
                 pl.BlockSpec((k_main // n, d), lambda i: (blk(i), 0))]
    out_shape = [jax.ShapeDtypeStruct((d, ffp), BF16), jax.ShapeDtypeStruct((d, ffp), BF16),
                 jax.ShapeDtypeStruct((k_main, d), BF16)]
    for m in extras:
        rows, cols = m.shape
        assert rows % (n * 2 * SUBLANES) == 0
        in_specs.append(pl.BlockSpec((rows // n, cols), lambda i: (blk(i), 0)))
        out_specs.append(pl.BlockSpec((rows // n, cols), lambda i: (blk(i), 0)))
        out_shape.append(jax.ShapeDtypeStruct(m.shape, BF16))
    args = [ffn_w_in.astype(F32), ffn_w_out.astype(F32)] + [m.astype(F32) for m in extras]
    return args, in_specs, out_specs, out_shape


def _w_out_tail(ffn_w_out, layer, which, k_main, ffp):
    tail = lax.optimization_barrier(ffn_w_out[layer, which, k_main:, :]).astype(BF16)
    return jnp.pad(tail, ((0, ffp - k_main - tail.shape[0]), (0, 0)))


def _cast_first_weights(ffn_w_in, ffn_w_out, layer, which):
    args, in_specs, out_specs, out_shape = _cast_io(ffn_w_in, ffn_w_out, layer, which, (), CAST_BLOCKS)
    return pl.pallas_call(
        functools.partial(_cast_weight_block, 0),
        grid=(CAST_BLOCKS,),
        in_specs=in_specs,
        out_specs=out_specs,
        out_shape=out_shape,
        compiler_params=pltpu.CompilerParams(
            dimension_semantics=("arbitrary",), vmem_limit_bytes=VMEM_LIMIT),
        name="weight_cast",
    )(*args)


def _tile_plan(seq_len, mixer, emit_h):
    tm = min(512 if emit_h else 1024, seq_len)
    return tm, max(1, tm // {None: 1024, "pool": 256, "s5": 512}[mixer])


def _layer_call(x, mod, norm_g, weights, seq_len, *, mixer=None, mixer_args=(),
                emit_h=False, final_g=None, next_ffn=None):
    n_tok, d = x.shape
    tm, n_sub = _tile_plan(seq_len, mixer, emit_h)
    assert seq_len % tm == 0 and tm % (n_sub * CHUNK * SUBLANES) == 0
    tiles_per_seq = seq_len // tm
    wg, wu, wom, wot = weights
    ffp = wg.shape[1]
    row_spec = pl.BlockSpec((tm, d), lambda i: (i, 0))
    chunk_spec = pl.BlockSpec((None, CHUNK, tm // CHUNK, d),
                              lambda i: (i // tiles_per_seq, 0, i % tiles_per_seq, 0))
    args = [x, mod, norm_g.astype(F32)]
    specs = [row_spec,
             pl.BlockSpec((1, N_SUB * N_MOD, d), lambda i: (i // tiles_per_seq, 0, 0)),
             _const_spec((N_SUB, d))]
    scratch = [pltpu.VMEM((n_sub, tm // n_sub, ffp), BF16)]
    if mixer == "pool":
        pool_w, pool_b, pool_scale = mixer_args
        hb = tm // POOL_HALO
        last = n_tok // POOL_HALO - 1
        args += [x, x, pool_w.astype(BF16), pool_b.astype(F32).reshape(1, d),
                 pool_scale.astype(F32).reshape(1, d)]
        specs += [pl.BlockSpec((POOL_HALO, d), lambda i: (jnp.maximum(i * hb - 1, 0), 0)),
                  pl.BlockSpec((POOL_HALO, d), lambda i: (jnp.minimum((i + 1) * hb, last), 0)),
                  _const_spec(pool_w.shape), _const_spec((1, d)), _const_spec((1, d))]
        scratch.append(pltpu.VMEM((tm + 2 * POOL_HALO, d), F32))
    elif mixer == "s5":
        ys, d_skip, glu_w, glu_b = mixer_args
        args += [ys, d_skip.astype(F32).reshape(1, d), glu_w.astype(BF16),
                 glu_b.astype(F32).reshape(1, 2 * d)]
        specs += [chunk_spec, _const_spec((1, d)), _const_spec((d, 2 * d)), _const_spec((1, 2 * d))]
        scratch.append(pltpu.VMEM((d // LANES, tm // CHUNK * CHUNK_PITCH, LANES), F32))
    args += [wg, wu, wom, wot]
    specs += [_const_spec(w.shape) for w in (wg, wu, wom, wot)]
    if final_g is not None:
        args.append(final_g.astype(F32).reshape(1, d))
        specs.append(_const_spec((1, d)))
    if next_ffn is not None:
        cast_args, cast_in, cast_out, cast_shape = _cast_io(*next_ffn, n_tok // tm)
        args += cast_args
        specs += cast_in
    out_shape = [jax.ShapeDtypeStruct((n_tok, d), F32)]
    out_specs = [row_spec]
    if emit_h:
        out_shape.append(jax.ShapeDtypeStruct((n_tok // seq_len, CHUNK, seq_len // CHUNK, d), BF16))
        out_specs.append(chunk_spec)
        scratch.append(pltpu.VMEM((d // LANES, tm // CHUNK * CHUNK_PITCH, LANES), F32))
    if next_ffn is not None:
        out_shape += cast_shape
        out_specs += cast_out
    outs = pl.pallas_call(
        functools.partial(_layer_kernel, mixer, emit_h, final_g is not None,
                          None if next_ffn is None else len(next_ffn[4]), seq_len, n_sub),
        grid=(n_tok // tm,),
        in_specs=specs,
        out_specs=out_specs,
        out_shape=out_shape,
        scratch_shapes=scratch,
        compiler_params=pltpu.CompilerParams(
            dimension_semantics=("arbitrary",), vmem_limit_bytes=VMEM_LIMIT),
        name="layer_" + (mixer or "ffn"),
    )(*args)
    n_act = 2 if emit_h else 1
    acts = tuple(outs[:n_act]) if emit_h else outs[0]
    if next_ffn is None:
        return acts
    _, ffp_next, k_main = _ffn_dims(next_ffn[0])
    tail = _w_out_tail(next_ffn[1], next_ffn[2], next_ffn[3], k_main, ffp_next)
    return acts, (*outs[n_act:n_act + 3], tail), tuple(outs[n_act + 3:])


def _cmul(ar, ai, br, bi):
    return ar * br - ai * bi, ar * bi + ai * br


def _s5_prep_kernel(n_levels, lr_ref, li_ref, ldt_ref, bre_ref, bim_ref, cre_ref, cim_ref,
                    t_ref, bz_ref, cx_ref, dec_ref):
    gb = lr_ref.shape[1]
    w = t_ref.shape[1]
    chunk = w // SSM_H
    n_dbl = chunk.bit_length() - 1
    assert chunk == 1 << n_dbl
    lane = lax.broadcasted_iota(jnp.int32, (1, LANES), 1)
    re_half = lane < SSM_P
    wlane = lax.broadcasted_iota(jnp.int32, (1, w), 1)
    step_e = (1 << lax.broadcasted_iota(jnp.int32, (SUBLANES, 1), 0)).astype(F32)
    level_e = (chunk << lax.broadcasted_iota(jnp.int32, (n_levels, 1), 0)).astype(F32)
    assert n_dbl < SUBLANES

    def flip(v):
        return jnp.concatenate([v[SSM_H * l:SSM_H * (l + 1)] for l in reversed(range(chunk))], axis=0)

    for g in range(gb):
        rs, zs, cs = [], [], []
        for direction in range(2):
            lr, li = lr_ref[direction, g], li_ref[direction, g]
            dt = jnp.exp(ldt_ref[direction, g])
            lrdt, lidt = lr * dt, li * dt

            def a_pow(e):
                mag = jnp.exp(lrdt * e)
                return mag * jnp.cos(lidt * e), mag * jnp.sin(lidt * e)

            step_r, step_i = a_pow(step_e)
            ar, ai = step_r[0:1], step_i[0:1]
            den = lr * lr + li * li
            fr = ((ar - 1.0) * lr + ai * li) / den
            fi = (ai * lr - (ar - 1.0) * li) / den
            bre, bim = bre_ref[direction, g], bim_ref[direction, g]
            bbr, bbi = _cmul(fr, fi, bre, bim)
            cre16, cim16 = cre_ref[direction, g], cim_ref[direction, g]

            b_r, b_i, c_r, c_i = bbr, bbi, cre16, cim16
            for k in range(n_dbl):
                sr, si = step_r[k:k + 1], step_i[k:k + 1]
                nr, ni = _cmul(b_r, b_i, sr, si)
                b_r, b_i = jnp.concatenate([b_r, nr], axis=0), jnp.concatenate([b_i, ni], axis=0)
                nr, ni = _cmul(c_r, c_i, sr, si)
                c_r, c_i = jnp.concatenate([c_r, nr], axis=0), jnp.concatenate([c_i, ni], axis=0)
            top_r, top_i = _cmul(cre16, cim16, step_r[n_dbl:n_dbl + 1], step_i[n_dbl:n_dbl + 1])
            c1_r = jnp.concatenate([c_r[SSM_H:], top_r], axis=0)
            c1_i = jnp.concatenate([c_i[SSM_H:], top_i], axis=0)

            def mine(v):
                return jnp.where(re_half if g % 2 == 0 else ~re_half, v, 0.0)

            if direction == 0:
                b_r, b_i = flip(b_r), flip(b_i)
            else:
                c1_r, c1_i, c_r, c_i = flip(c1_r), flip(c1_i), flip(c_r), flip(c_i)
            zs += [mine(b_r), mine(b_i)]
            cs += [mine(c1_r).T, mine(-c1_i).T]
            gmat = jnp.where(re_half, c_r, -c_i).T
            bbt = jnp.where(re_half, bbr, bbi)
            rs.append(_dot3(bbt, gmat))
            er, ei = a_pow(level_e)
            base = 2 * n_levels * direction
            dec_ref[g, base:base + n_levels, :] = er
            dec_ref[g, base + n_levels:base + 2 * n_levels, :] = ei
        for i in range(chunk):
            s = SSM_H * i
            fwd_rows = jnp.where(wlane >= s, pltpu.roll(rs[0], s, 1) if s else rs[0], 0.0)
            e = (s + SSM_H) % w
            bwd_rows = jnp.where(wlane < s + SSM_H, pltpu.roll(rs[1], e, 1) if e else rs[1], 0.0)
            t_ref[g, s:s + SSM_H, :] = (fwd_rows + bwd_rows).astype(BF16)
        bz_ref[g] = jnp.concatenate(zs, axis=1).astype(BF16)
        cx_ref[g] = jnp.concatenate(cs, axis=0).astype(BF16)


def _s5_prep(lam_re, lam_im, log_dt, b_re, b_im, c_re, c_im, n_levels):
    n_dir, n_grp, n_state = lam_re.shape
    assert n_dir == 2 and n_state == SSM_P and b_re.shape[-1] == SSM_H
    w = CHUNK * SSM_H
    gb = GROUPS_PER_SLAB
    assert n_grp % gb == 0 and n_levels % SUBLANES == 0

    def twice(v):
        return jnp.concatenate([v, v], axis=-1).astype(F32)

    lr = twice(lam_re)[:, :, None, :]
    li = twice(lam_im)[:, :, None, :]
    ldt = jnp.broadcast_to(log_dt.astype(F32)[:, :, None, None], (n_dir, n_grp, 1, LANES))
    bre, bim = twice(jnp.swapaxes(b_re, 2, 3)), twice(jnp.swapaxes(b_im, 2, 3))
    cre, cim = twice(c_re), twice(c_im)
    n_dec = 2 * 2 * n_levels
    row = lambda r: pl.BlockSpec((n_dir, gb, r, LANES), lambda i: (0, i, 0, 0))
    return pl.pallas_call(
        functools.partial(_s5_prep_kernel, n_levels),
        grid=(n_grp // gb,),
        in_specs=[row(1), row(1), row(1), row(SSM_H), row(SSM_H), row(SSM_H), row(SSM_H)],
        out_specs=[pl.BlockSpec((gb, w, w), lambda i: (i, 0, 0)),
                   pl.BlockSpec((gb, w, 4 * LANES), lambda i: (i, 0, 0)),
                   pl.BlockSpec((gb, 4 * LANES, w), lambda i: (i, 0, 0)),
                   pl.BlockSpec((gb, n_dec, LANES), lambda i: (i, 0, 0))],
        out_shape=[jax.ShapeDtypeStruct((n_grp, w, w), BF16),
                   jax.ShapeDtypeStruct((n_grp, w, 4 * LANES), BF16),
                   jax.ShapeDtypeStruct((n_grp, 4 * LANES, w), BF16),
                   jax.ShapeDtypeStruct((n_grp, n_dec, LANES), F32)],
        compiler_params=pltpu.CompilerParams(
            dimension_semantics=("arbitrary",), vmem_limit_bytes=VMEM_LIMIT),
        name="s5_prep",
    )(lr, li, ldt, bre, bim, cre, cim)


def _chunk_scan(zr, zi, a_re, a_im, reverse):
    n_rows = zr.shape[0]
    assert n_rows % SUBLANES == 0
    row = lax.broadcasted_iota(jnp.int32, (SUBLANES, 1), 0)
    edge = SUBLANES - 1 if reverse else 0

    def shifted(v, dist):
        if reverse:
            return jnp.where(row < SUBLANES - dist, pltpu.roll(v, SUBLANES - dist, 0), 0.0)
        return jnp.where(row >= dist, pltpu.roll(v, dist, 0), 0.0)

    def local_scan(vr, vi):
        for k in range(SUBLANES.bit_length() - 1):
            hr, hi = shifted(vr, 1 << k), shifted(vi, 1 << k)
            vr, vi = vr + (hr * a_re[k] - hi * a_im[k]), vi + (hr * a_im[k] + hi * a_re[k])
        return vr, vi

    def spread(v, r):
        return jnp.broadcast_to(v[r:r + 1, :], v.shape)

    unit = jnp.where(row == edge, 1.0, 0.0) + jnp.zeros((SUBLANES, zr.shape[1]), F32)
    pr, pi = local_scan(unit, jnp.zeros_like(unit))
    pr, pi = pr * a_re[0] - pi * a_im[0], pr * a_im[0] + pi * a_re[0]
    n_tiles = n_rows // SUBLANES
    order = range(n_tiles - 1, -1, -1) if reverse else range(n_tiles)
    cr = ci = jnp.zeros_like(unit)
    out_r, out_i = [None] * n_tiles, [None] * n_tiles
    for v in order:
        rows = slice(SUBLANES * v, SUBLANES * (v + 1))
        sr, si = local_scan(zr[rows], zi[rows])
        sr, si = sr + (pr * cr - pi * ci), si + (pr * ci + pi * cr)
        out_r[v] = jnp.where(row == edge, cr, shifted(sr, 1))
        out_i[v] = jnp.where(row == edge, ci, shifted(si, 1))
        cr, ci = spread(sr, SUBLANES - 1 - edge), spread(si, SUBLANES - 1 - edge)
    return jnp.concatenate(out_r, axis=0), jnp.concatenate(out_i, axis=0)


def _slab_permutation():
    n = GROUPS_PER_SLAB
    idx = np.arange(n * LANES)
    a, b, h = idx // LANES, (idx % LANES) // SSM_H, idx % SSM_H
    perm = np.zeros((n * LANES, n * LANES), np.float32)
    perm[idx, b * LANES + a * SSM_H + h] = 1.0
    return jnp.asarray(perm, BF16)


def _unit_transpose(vs):
    vs = list(vs)
    n = len(vs)
    assert n * SSM_H == LANES
    unit = lax.broadcasted_iota(jnp.int32, (1, LANES), 1) // SSM_H
    d = n // 2
    while d:
        upper = (unit & d) != 0
        for i in range(n):
            if i & d:
                continue
            a, b = vs[i], vs[i + d]
            vs[i] = jnp.where(upper, pltpu.roll(b, d * SSM_H, 1), a)
            vs[i + d] = jnp.where(upper, b, pltpu.roll(a, LANES - d * SSM_H, 1))
        d //= 2
    return vs


def _s5_core_kernel(n_levels, h_ref, perm_ref, t_ref, bz_ref, cx_ref, dec_ref, y_ref, u_ref, yg_ref):
    n_grp = GROUPS_PER_SLAB
    perm = perm_ref[...]
    for o in range(CHUNK // n_grp):
        v = jnp.concatenate([h_ref[n_grp * o + l8] for l8 in range(n_grp)], axis=1)
        uv = _dot(v, perm).astype(BF16)
        for g in range(n_grp):
            u_ref[g, :, LANES * o:LANES * (o + 1)] = uv[:, LANES * g:LANES * (g + 1)]
    first_half = lax.broadcasted_iota(jnp.int32, (1, LANES), 1) < SSM_P
    for g0 in range(0, n_grp, 2):
        pair = (g0, g0 + 1)
        z = _dot(u_ref[g0], bz_ref[g0]) + _dot(u_ref[g0 + 1], bz_ref[g0 + 1])
        xs = []
        for direction in range(2):
            base = 2 * n_levels * direction
            a_re, a_im = ([jnp.where(first_half, dec_ref[g0, r:r + 1, :], dec_ref[g0 + 1, r:r + 1, :])
                           for r in range(lo, lo + n_levels)] for lo in (base, base + n_levels))
            cols = 2 * LANES * direction
            xs += _chunk_scan(z[:, cols:cols + LANES], z[:, cols + LANES:cols + 2 * LANES],
                              a_re, a_im, direction == 1)
        x_all = jnp.concatenate(xs, axis=1).astype(BF16)
        for g in pair:
            yg_ref[g] = _dot(u_ref[g], t_ref[g]) + _dot(x_all, cx_ref[g])
    for o in range(CHUNK // n_grp):
        ys = _unit_transpose([yg_ref[g, :, LANES * o:LANES * (o + 1)] for g in range(n_grp)])
        for l8 in range(n_grp):
            y_ref[n_grp * o + l8] = ys[l8].astype(BF16)


def _s5_mix(h, lam_re, lam_im, log_dt, b_re, b_im, c_re, c_im):
    bsz, _, rows, d = h.shape
    w = CHUNK * SSM_H
    gb = GROUPS_PER_SLAB
    assert h.shape[1] == CHUNK and CHUNK % gb == 0 and d % LANES == 0
    n_levels = -(-max(1, (rows - 1).bit_length()) // SUBLANES) * SUBLANES
    t_mat, bz, cx, dec = _s5_prep(lam_re, lam_im, log_dt, b_re, b_im, c_re, c_im, n_levels)
    perm = _slab_permutation()
    slab = pl.BlockSpec((None, CHUNK, rows, LANES), lambda s, b: (b, 0, 0, s))
    grp = lambda r, c: pl.BlockSpec((gb, r, c), lambda s, b: (s, 0, 0))
    return pl.pallas_call(
        functools.partial(_s5_core_kernel, n_levels),
        grid=(d // LANES, bsz),
        in_specs=[slab, _const_spec(perm.shape), grp(w, w), grp(w, 4 * LANES), grp(4 * LANES, w),
                  grp(dec.shape[1], LANES)],
        out_specs=slab,
        out_shape=jax.ShapeDtypeStruct(h.shape, BF16),
        scratch_shapes=[pltpu.VMEM((gb, rows, w), BF16), pltpu.VMEM((gb, rows, w), F32)],
        compiler_params=pltpu.CompilerParams(
            dimension_semantics=("arbitrary", "arbitrary"), vmem_limit_bytes=VMEM_LIMIT),
        name="s5_core",
    )(h, perm, t_mat, bz, cx, dec)


def kernel(x, c, mod_w, mod_b, norm_g, ffn_w_in, ffn_w_out, pool_w, pool_b, pool_scale,
           ssm_lam_re, ssm_lam_im, ssm_log_dt, ssm_b_re, ssm_b_im, ssm_c_re, ssm_c_im,
           ssm_d, glu_w, glu_b, final_g):
    bsz, seq_len, d = x.shape
    depth = mod_w.shape[0]
    n_mixers = 2
    mods = _modulation(c, mod_w, mod_b)
    xf = x.astype(F32).reshape(bsz * seq_len, d)
    _, ffp, k_main = _ffn_dims(ffn_w_in)
    weights = (*_cast_first_weights(ffn_w_in, ffn_w_out, 0, 0), _w_out_tail(ffn_w_out, 0, 0, k_main, ffp))
    for i in range(depth):
        j = i // n_mixers
        is_s5 = i % n_mixers == 1
        last = i == depth - 1
        extras = (glu_w[j],) if is_s5 else ()
        acts, weights, extras = _layer_call(xf, mods[i], norm_g[i], weights, seq_len, emit_h=is_s5,
                                            next_ffn=(ffn_w_in, ffn_w_out, i, 1, extras))
        if is_s5:
            xf, h = acts
            ys = _s5_mix(h, ssm_lam_re[j], ssm_lam_im[j], ssm_log_dt[j], ssm_b_re[j],
                         ssm_b_im[j], ssm_c_re[j], ssm_c_im[j])
            mixer, mixer_args = "s5", (ys, ssm_d[j], extras[0], glu_b[j])
        else:
            xf = acts
            mixer, mixer_args = "pool", (pool_w[j], pool_b[j], pool_scale[j])
        res = _layer_call(xf, mods[i], norm_g[i], weights, seq_len, mixer=mixer, mixer_args=mixer_args,
                          final_g=final_g if last else None,
                          next_ffn=None if last else (ffn_w_in, ffn_w_out, i + 1, 0, ()))
        xf, weights = (res, None) if last else res[:2]
    return xf.reshape(bsz, seq_len, d).astype(x.dtype)
```

```python
import functools
import math

import numpy as np

import jax
import jax.numpy as jnp
from jax import lax
from jax.experimental import pallas as pl
from jax.experimental.pallas import tpu as pltpu

F32 = jnp.float32
BF16 = jnp.bfloat16

EPS = 1e-6
N_SUB = 3
N_MOD = 3
POOL_WINDOWS = (2, 4, 8, 16)
POOL_HALO = 8
SSM_H = 16
SSM_P = 64
CHUNK = 32
CHUNK_PITCH = CHUNK + 4
FF_CHUNK = 256
GLU_CHUNK = 512
MATMUL_PHASE = "matmul phase"
SCAN_YIELD_TILES = 8
LANES = 128
CAST_BLOCKS = 16
SUBLANES = 8
GROUPS_PER_SLAB = LANES // SSM_H
VMEM_LIMIT = 56 * 1024 * 1024


def _sigmoid(v):
    return 1.0 / (1.0 + jnp.exp(-v))


def _split3(a):
    hi = a.astype(BF16)
    lo = (a - hi.astype(F32)).astype(BF16)
    return hi, lo


def _dot(a, b):
    return jnp.dot(a, b, preferred_element_type=F32)


def _dot3(a, b):
    ah, al = _split3(a)
    bh, bl = _split3(b)
    return _dot(ah, bh) + (_dot(ah, bl) + _dot(al, bh))


def _mod_kernel(c_ref, w_ref, b_ref, o_ref):
    c = c_ref[...]
    cond = c * _sigmoid(c)
    o_ref[0] = _dot3(cond, w_ref[0]) + b_ref[0]


def _modulation(c, mod_w, mod_b):
    depth, d, n = mod_w.shape
    bsz = c.shape[0]
    rows = SUBLANES
    tn = n // 4 if n % (4 * LANES) == 0 else n
    assert n % tn == 0 and bsz <= rows
    c_pad = jnp.zeros((rows, d), F32).at[:bsz].set(c.astype(F32))
    out = pl.pallas_call(
        _mod_kernel,
        grid=(depth, n // tn),
        in_specs=[
            pl.BlockSpec((rows, d), lambda i, j: (0, 0)),
            pl.BlockSpec((1, d, tn), lambda i, j: (i, 0, j)),
            pl.BlockSpec((1, 1, tn), lambda i, j: (i, 0, j)),
        ],
        out_specs=pl.BlockSpec((1, rows, tn), lambda i, j: (i, 0, j)),
        out_shape=jax.ShapeDtypeStruct((depth, rows, n), F32),
        compiler_params=pltpu.CompilerParams(
            dimension_semantics=("arbitrary", "arbitrary"), vmem_limit_bytes=VMEM_LIMIT),
        name="adaln_modulation",
    )(c_pad, mod_w.astype(F32), mod_b.astype(F32).reshape(depth, 1, n))
    return out[:, :bsz].reshape(depth, bsz, N_SUB * N_MOD, d)


def _rms(x, g):
    return x * lax.rsqrt(jnp.mean(x * x, axis=-1, keepdims=True) + EPS) * g


def _modnorm(x, g_ref, mod, sub):
    r = N_MOD * sub
    shift, scale = mod[r:r + 1], mod[r + 1:r + 2]
    return _rms(x, g_ref[sub:sub + 1, :]) * (1.0 + scale) + shift


def _swiglu_step(x, g_ref, mod, sub, wg_ref, wu_ref, wom_ref, wot_ref, act_ref):
    h = _modnorm(x, g_ref, mod, sub).astype(BF16)
    yield MATMUL_PHASE
    n_chunks = act_ref.shape[1] // FF_CHUNK
    for k in range(n_chunks):
        cols = slice(FF_CHUNK * k, FF_CHUNK * (k + 1))
        gate = _dot(h, wg_ref[:, cols])
        up = _dot(h, wu_ref[:, cols])
        act_ref[:, cols] = (gate * _sigmoid(gate) * up).astype(BF16)
        yield
    k_main = wom_ref.shape[0]
    out = _dot(act_ref[:, :k_main], wom_ref[...])
    if act_ref.shape[1] > k_main:
        out = out + _dot(act_ref[:, k_main:], wot_ref[...])
    r = N_MOD * sub + 2
    return x + (0.5 * mod[r:r + 1]) * out


def _gelu_tanh(v):
    return 0.5 * v * (1.0 + jnp.tanh(math.sqrt(2.0 / math.pi) * (v + 0.044715 * (v * v * v))))


def _pool_fill(x_ref, xp_ref, xn_ref, g_ref, mod, hs_ref, seq_len):
    tm = x_ref.shape[0]
    tiles_per_seq = seq_len // tm
    j = pl.program_id(0) % tiles_per_seq
    hp = _modnorm(xp_ref[...], g_ref, mod, 1)
    hn = _modnorm(xn_ref[...], g_ref, mod, 1)
    hs_ref[0:POOL_HALO, :] = jnp.where(j > 0, hp, 0.0)
    hs_ref[POOL_HALO:POOL_HALO + tm, :] = _modnorm(x_ref[...], g_ref, mod, 1)
    hs_ref[POOL_HALO + tm:, :] = jnp.where(j < tiles_per_seq - 1, hn, 0.0)


def _pool_update(x, row0, tm, mod, pw_ref, pb_ref, ps_ref, hs_ref, seq_len):
    ts, d = x.shape
    cg = d // len(POOL_WINDOWS)
    j = pl.program_id(0) % (seq_len // tm)
    t = j * tm + row0 + lax.broadcasted_iota(jnp.int32, (ts, 1), 0)
    n_ext = ts + 2 * POOL_HALO
    ys = []
    for gi, window in enumerate(POOL_WINDOWS):
        left = window // 2
        right = window - 1 - left
        cols = slice(gi * cg, (gi + 1) * cg)
        assert window & (window - 1) == 0 and right < POOL_HALO
        ext = hs_ref[row0:row0 + n_ext, cols]
        run = ext
        span = 1
        while span < window:
            run = run + pltpu.roll(run, span, 0)
            span *= 2
        if right:
            run = pltpu.roll(run, n_ext - right, 0)
        total = run[POOL_HALO:POOL_HALO + ts]
        lo = jnp.maximum(t - left, 0)
        hi = jnp.minimum(t + right, seq_len - 1)
        count = (hi - lo + 1).astype(F32)
        p = total / count - ext[POOL_HALO:POOL_HALO + ts]
        ys.append(_dot(p.astype(BF16), pw_ref[gi]))
        yield
    y = (jnp.concatenate(ys, axis=1) + pb_ref[...]) * ps_ref[...]
    return x + mod[N_MOD + 2:N_MOD + 3] * y


def _s5_tail_update(x, ys, g_ref, mod, dskip_ref, gw_ref, gb_ref):
    d = x.shape[1]
    h = _modnorm(x, g_ref, mod, 1)
    y = _gelu_tanh(dskip_ref[...] * h + ys).astype(BF16)
    yield
    ms = []
    for k in range(d // GLU_CHUNK):
        cv = slice(GLU_CHUNK * k, GLU_CHUNK * (k + 1))
        cg = slice(d + GLU_CHUNK * k, d + GLU_CHUNK * (k + 1))
        val = _dot(y, gw_ref[:, cv]) + gb_ref[:, cv]
        gate = _dot(y, gw_ref[:, cg]) + gb_ref[:, cg]
        ms.append(val * _sigmoid(gate))
        yield
    return x + mod[N_MOD + 2:N_MOD + 3] * jnp.concatenate(ms, axis=1)


def _cast_weight_block(n_extra, win_ref, wout_ref, *refs):
    wg_ref, wu_ref, wom_ref = refs[n_extra:n_extra + 3]
    for src, dst in zip(refs[:n_extra], refs[n_extra + 3:]):
        dst[...] = src[...].astype(BF16)
    d_ff = win_ref.shape[1] // 2
    w = win_ref[...]
    for src, dst in ((w[:, :d_ff], wg_ref), (w[:, d_ff:], wu_ref)):
        dst[:, :d_ff] = src.astype(BF16)
        if dst.shape[1] > d_ff:
            dst[:, d_ff:] = jnp.zeros((dst.shape[0], dst.shape[1] - d_ff), BF16)
    wom_ref[...] = wout_ref[...].astype(BF16)


def _layer_kernel(mixer, emit_h, final_norm, cast_next, seq_len, n_sub, *refs):
    refs = list(refs)
    x_ref, mod_ref, g_ref = refs[:3]
    del refs[:3]
    mod = mod_ref[0]
    if mixer == "pool":
        xp_ref, xn_ref, pw_ref, pb_ref, ps_ref = refs[:5]
        del refs[:5]
    elif mixer == "s5":
        ys_ref, dskip_ref, gw_ref, gb_ref = refs[:4]
        del refs[:4]
    wg_ref, wu_ref, wom_ref, wot_ref = refs[:4]
    del refs[:4]
    if final_norm:
        fg_ref = refs.pop(0)
    if cast_next is not None:
        cast_in = refs[:2 + cast_next]
        del refs[:2 + cast_next]
    o_ref = refs.pop(0)
    if emit_h:
        h_ref = refs.pop(0)
    if cast_next is not None:
        _cast_weight_block(cast_next, *cast_in, *refs[:3 + cast_next])
        del refs[:3 + cast_next]
    act_ref = refs.pop(0)
    tm = x_ref.shape[0]
    ts = tm // n_sub
    chunks = ts // CHUNK
    if mixer == "pool":
        hs_ref = refs.pop(0)
        _pool_fill(x_ref, xp_ref, xn_ref, g_ref, mod, hs_ref, seq_len)
    elif mixer == "s5":
        rows_ref = refs.pop(0)
        n_slab = rows_ref.shape[0]
        for l in range(CHUNK):
            for s in range(n_slab):
                rows_ref[s, pl.ds(l, tm // CHUNK, stride=CHUNK_PITCH), :] = (
                    ys_ref[l, :, LANES * s:LANES * (s + 1)].astype(F32))
    if emit_h:
        hrows_ref = refs.pop(0)

    def sub_tile(si):
        row0 = si * ts
        x = x_ref[row0:row0 + ts, :]
        if mixer == "pool":
            x = yield from _pool_update(x, row0, tm, mod, pw_ref, pb_ref, ps_ref, hs_ref, seq_len)
        elif mixer == "s5":
            ys = jnp.concatenate(
                [jnp.concatenate([rows_ref[s, CHUNK_PITCH * c:CHUNK_PITCH * c + CHUNK, :]
                                  for c in range(si * chunks, (si + 1) * chunks)], axis=0)
                 for s in range(n_slab)], axis=1)
            x = yield from _s5_tail_update(x, ys, g_ref, mod, dskip_ref, gw_ref, gb_ref)
        sub = 0 if mixer is None else 2
        x = yield from _swiglu_step(x, g_ref, mod, sub, wg_ref, wu_ref, wom_ref, wot_ref, act_ref.at[si])
        if final_norm:
            x = _rms(x, fg_ref[...])
        o_ref[row0:row0 + ts, :] = x
        if emit_h:
            h = _modnorm(x, g_ref, mod, 1)
            for c in range(chunks):
                base = CHUNK_PITCH * (si * chunks + c)
                for s in range(hrows_ref.shape[0]):
                    hrows_ref[s, base:base + CHUNK, :] = h[CHUNK * c:CHUNK * (c + 1), LANES * s:LANES * (s + 1)]

    tiles = [sub_tile(si) for si in range(n_sub)]
    pending = [True] * n_sub
    for marker in tiles[0]:
        if marker == MATMUL_PHASE:
            pending[0] = False
            break
    for si, tile in enumerate(tiles):
        nxt = si + 1 if si + 1 < n_sub else None
        for _ in tile:
            if nxt is not None and pending[nxt]:
                pending[nxt] = next(tiles[nxt]) != MATMUL_PHASE
        while nxt is not None and pending[nxt]:
            pending[nxt] = next(tiles[nxt]) != MATMUL_PHASE
    if emit_h:
        for l in range(CHUNK):
            for s in range(hrows_ref.shape[0]):
                h_ref[l, :, LANES * s:LANES * (s + 1)] = (
                    hrows_ref[s, pl.ds(l, tm // CHUNK, stride=CHUNK_PITCH), :].astype(BF16))


def _const_spec(shape):
    nd = len(shape)
    return pl.BlockSpec(shape, lambda *_: (0,) * nd, pipeline_mode=pl.Buffered(1))


def _ffn_dims(ffn_w_in):
    d_ff = ffn_w_in.shape[-1] // 2
    return d_ff, d_ff + -d_ff % FF_CHUNK, d_ff // FF_CHUNK * FF_CHUNK


def _cast_io(ffn_w_in, ffn_w_out, layer, which, extras, steps):
    d = ffn_w_in.shape[2]
    d_ff, ffp, k_main = _ffn_dims(ffn_w_in)
    n = 1
    while n < CAST_BLOCKS and steps % (2 * n) == 0:
        n *= 2
    assert d % (n * 2 * SUBLANES) == 0 and k_main % (n * 2 * SUBLANES) == 0
    blk = lambda i: i * n // steps
    in_specs = [pl.BlockSpec((None, None, d // n, 2 * d_ff), lambda i: (layer, which, blk(i), 0)),
                pl.BlockSpec((None, None, k_main // n, d), lambda i: (layer, which, blk(i), 0))]
    out_specs = [pl.BlockSpec((d // n, ffp), lambda i: (blk(i), 0)),
                 pl.BlockSpec((d // n, ffp), lambda i: (blk(i), 0)),
                 pl.BlockSpec((k_main // n, d), lambda i: (blk(i), 0))]
    out_shape = [jax.ShapeDtypeStruct((d, ffp), BF16), jax.ShapeDtypeStruct((d, ffp), BF16),
                 jax.ShapeDtypeStruct((k_main, d), BF16)]
    for m in extras:
        rows, cols = m.shape
        assert rows % (n * 2 * SUBLANES) == 0
        in_specs.append(pl.BlockSpec((rows // n, cols), lambda i: (blk(i), 0)))
        out_specs.append(pl.BlockSpec((rows // n, cols), lambda i: (blk(i), 0)))
        out_shape.append(jax.ShapeDtypeStruct(m.shape, BF16))
    args = [ffn_w_in.astype(F32), ffn_w_out.astype(F32)] + [m.astype(F32) for m in extras]
    return args, in_specs, out_specs, out_shape


def _w_out_tail(ffn_w_out, layer, which, k_main, ffp):
    tail = lax.optimization_barrier(ffn_w_out[layer, which, k_main:, :]).astype(BF16)
    return jnp.pad(tail, ((0, ffp - k_main - tail.shape[0]), (0, 0)))


def _cast_first_weights(ffn_w_in, ffn_w_out, layer, which):
    args, in_specs, out_specs, out_shape = _cast_io(ffn_w_in, ffn_w_out, layer, which, (), CAST_BLOCKS)
    return pl.pallas_call(
        functools.partial(_cast_weight_block, 0),
        grid=(CAST_BLOCKS,),
        in_specs=in_specs,
        out_specs=out_specs,
        out_shape=out_shape,
        compiler_params=pltpu.CompilerParams(
            dimension_semantics=("arbitrary",), vmem_limit_bytes=VMEM_LIMIT),
        name="weight_cast",
    )(*args)


def _tile_plan(seq_len, mixer, emit_h):
    tm = min(512 if emit_h else 1024, seq_len)
    return tm, max(1, tm // {None: 1024, "pool": 256, "s5": 512}[mixer])


def _layer_call(x, mod, norm_g, weights, seq_len, *, mixer=None, mixer_args=(),
                emit_h=False, final_g=None, next_ffn=None):
    n_tok, d = x.shape
    tm, n_sub = _tile_plan(seq_len, mixer, emit_h)
    assert seq_len % tm == 0 and tm % (n_sub * CHUNK * SUBLANES) == 0
    tiles_per_seq = seq_len // tm
    wg, wu, wom, wot = weights
    ffp = wg.shape[1]
    row_spec = pl.BlockSpec((tm, d), lambda i: (i, 0))
    chunk_spec = pl.BlockSpec((None, CHUNK, tm // CHUNK, d),
                              lambda i: (i // tiles_per_seq, 0, i % tiles_per_seq, 0))
    args = [x, mod, norm_g.astype(F32)]
    specs = [row_spec,
             pl.BlockSpec((1, N_SUB * N_MOD, d), lambda i: (i // tiles_per_seq, 0, 0)),
             _const_spec((N_SUB, d))]
    scratch = [pltpu.VMEM((n_sub, tm // n_sub, ffp), BF16)]
    if mixer == "pool":
        pool_w, pool_b, pool_scale = mixer_args
        hb = tm // POOL_HALO
        last = n_tok // POOL_HALO - 1
        args += [x, x, pool_w.astype(BF16), pool_b.astype(F32).reshape(1, d),
                 pool_scale.astype(F32).reshape(1, d)]
        specs += [pl.BlockSpec((POOL_HALO, d), lambda i: (jnp.maximum(i * hb - 1, 0), 0)),
                  pl.BlockSpec((POOL_HALO, d), lambda i: (jnp.minimum((i + 1) * hb, last), 0)),
                  _const_spec(pool_w.shape), _const_spec((1, d)), _const_spec((1, d))]
        scratch.append(pltpu.VMEM((tm + 2 * POOL_HALO, d), F32))
    elif mixer == "s5":
        ys, d_skip, glu_w, glu_b = mixer_args
        args += [ys, d_skip.astype(F32).reshape(1, d), glu_w.astype(BF16),
                 glu_b.astype(F32).reshape(1, 2 * d)]
        specs += [chunk_spec, _const_spec((1, d)), _const_spec((d, 2 * d)), _const_spec((1, 2 * d))]
        scratch.append(pltpu.VMEM((d // LANES, tm // CHUNK * CHUNK_PITCH, LANES), F32))
    args += [wg, wu, wom, wot]
    specs += [_const_spec(w.shape) for w in (wg, wu, wom, wot)]
    if final_g is not None:
        args.append(final_g.astype(F32).reshape(1, d))
        specs.append(_const_spec((1, d)))
    if next_ffn is not None:
        cast_args, cast_in, cast_out, cast_shape = _cast_io(*next_ffn, n_tok // tm)
        args += cast_args
        specs += cast_in
    out_shape = [jax.ShapeDtypeStruct((n_tok, d), F32)]
    out_specs = [row_spec]
    if emit_h:
        out_shape.append(jax.ShapeDtypeStruct((n_tok // seq_len, CHUNK, seq_len // CHUNK, d), BF16))
        out_specs.append(chunk_spec)
        scratch.append(pltpu.VMEM((d // LANES, tm // CHUNK * CHUNK_PITCH, LANES), F32))
    if next_ffn is not None:
        out_shape += cast_shape
        out_specs += cast_out
    outs = pl.pallas_call(
        functools.partial(_layer_kernel, mixer, emit_h, final_g is not None,
                          None if next_ffn is None else len(next_ffn[4]), seq_len, n_sub),
        grid=(n_tok // tm,),
        in_specs=specs,
        out_specs=out_specs,
        out_shape=out_shape,
        scratch_shapes=scratch,
        compiler_params=pltpu.CompilerParams(
            dimension_semantics=("arbitrary",), vmem_limit_bytes=VMEM_LIMIT),
        name="layer_" + (mixer or "ffn"),
    )(*args)
    n_act = 2 if emit_h else 1
    acts = tuple(outs[:n_act]) if emit_h else outs[0]
    if next_ffn is None:
        return acts
    _, ffp_next, k_main = _ffn_dims(next_ffn[0])
    tail = _w_out_tail(next_ffn[1], next_ffn[2], next_ffn[3], k_main, ffp_next)
    return acts, (*outs[n_act:n_act + 3], tail), tuple(outs[n_act + 3:])


def _cmul(ar, ai, br, bi):
    return ar * br - ai * bi, ar * bi + ai * br


def _s5_prep_kernel(n_levels, lr_ref, li_ref, ldt_ref, bre_ref, bim_ref, cre_ref, cim_ref,
                    t_ref, bz_ref, cx_ref, dec_ref):
    gb = lr_ref.shape[1]
    w = t_ref.shape[1]
    chunk = w // SSM_H
    n_dbl = chunk.bit_length() - 1
    assert chunk == 1 << n_dbl
    lane = lax.broadcasted_iota(jnp.int32, (1, LANES), 1)
    re_half = lane < SSM_P
    wlane = lax.broadcasted_iota(jnp.int32, (1, w), 1)
    step_e = (1 << lax.broadcasted_iota(jnp.int32, (SUBLANES, 1), 0)).astype(F32)
    level_e = (chunk << lax.broadcasted_iota(jnp.int32, (n_levels, 1), 0)).astype(F32)
    assert n_dbl < SUBLANES

    def flip(v):
        return jnp.concatenate([v[SSM_H * l:SSM_H * (l + 1)] for l in reversed(range(chunk))], axis=0)

    for g in range(gb):
        rs, zs, cs = [], [], []
        for direction in range(2):
            lr, li = lr_ref[direction, g], li_ref[direction, g]
            dt = jnp.exp(ldt_ref[direction, g])
            lrdt, lidt = lr * dt, li * dt

            def a_pow(e):
                mag = jnp.exp(lrdt * e)
                return mag * jnp.cos(lidt * e), mag * jnp.sin(lidt * e)

            step_r, step_i = a_pow(step_e)
            ar, ai = step_r[0:1], step_i[0:1]
            den = lr * lr + li * li
            fr = ((ar - 1.0) * lr + ai * li) / den
            fi = (ai * lr - (ar - 1.0) * li) / den
            bre, bim = bre_ref[direction, g], bim_ref[direction, g]
            bbr, bbi = _cmul(fr, fi, bre, bim)
            cre16, cim16 = cre_ref[direction, g], cim_ref[direction, g]

            b_r, b_i, c_r, c_i = bbr, bbi, cre16, cim16
            for k in range(n_dbl):
                sr, si = step_r[k:k + 1], step_i[k:k + 1]
                nr, ni = _cmul(b_r, b_i, sr, si)
                b_r, b_i = jnp.concatenate([b_r, nr], axis=0), jnp.concatenate([b_i, ni], axis=0)
                nr, ni = _cmul(c_r, c_i, sr, si)
                c_r, c_i = jnp.concatenate([c_r, nr], axis=0), jnp.concatenate([c_i, ni], axis=0)
            top_r, top_i = _cmul(cre16, cim16, step_r[n_dbl:n_dbl + 1], step_i[n_dbl:n_dbl + 1])
            c1_r = jnp.concatenate([c_r[SSM_H:], top_r], axis=0)
            c1_i = jnp.concatenate([c_i[SSM_H:], top_i], axis=0)

            def mine(v):
                return jnp.where(re_half if g % 2 == 0 else ~re_half, v, 0.0)

            if direction == 0:
                b_r, b_i = flip(b_r), flip(b_i)
            else:
                c1_r, c1_i, c_r, c_i = flip(c1_r), flip(c1_i), flip(c_r), flip(c_i)
            zs += [mine(b_r), mine(b_i)]
            cs += [mine(c1_r).T, mine(-c1_i).T]
            gmat = jnp.where(re_half, c_r, -c_i).T
            bbt = jnp.where(re_half, bbr, bbi)
            rs.append(_dot3(bbt, gmat))
            er, ei = a_pow(level_e)
            base = 2 * n_levels * direction
            dec_ref[g, base:base + n_levels, :] = er
            dec_ref[g, base + n_levels:base + 2 * n_levels, :] = ei
        for i in range(chunk):
            s = SSM_H * i
            fwd_rows = jnp.where(wlane >= s, pltpu.roll(rs[0], s, 1) if s else rs[0], 0.0)
            e = (s + SSM_H) % w
            bwd_rows = jnp.where(wlane < s + SSM_H, pltpu.roll(rs[1], e, 1) if e else rs[1], 0.0)
            t_ref[g, s:s + SSM_H, :] = (fwd_rows + bwd_rows).astype(BF16)
        bz_ref[g] = jnp.concatenate(zs, axis=1).astype(BF16)
        cx_ref[g] = jnp.concatenate(cs, axis=0).astype(BF16)


def _s5_prep(lam_re, lam_im, log_dt, b_re, b_im, c_re, c_im, n_levels):
    n_dir, n_grp, n_state = lam_re.shape
    assert n_dir == 2 and n_state == SSM_P and b_re.shape[-1] == SSM_H
    w = CHUNK * SSM_H
    gb = GROUPS_PER_SLAB
    assert n_grp % gb == 0 and n_levels % SUBLANES == 0

    def twice(v):
        return jnp.concatenate([v, v], axis=-1).astype(F32)

    lr = twice(lam_re)[:, :, None, :]
    li = twice(lam_im)[:, :, None, :]
    ldt = jnp.broadcast_to(log_dt.astype(F32)[:, :, None, None], (n_dir, n_grp, 1, LANES))
    bre, bim = twice(jnp.swapaxes(b_re, 2, 3)), twice(jnp.swapaxes(b_im, 2, 3))
    cre, cim = twice(c_re), twice(c_im)
    n_dec = 2 * 2 * n_levels
    row = lambda r: pl.BlockSpec((n_dir, gb, r, LANES), lambda i: (0, i, 0, 0))
    return pl.pallas_call(
        functools.partial(_s5_prep_kernel, n_levels),
        grid=(n_grp // gb,),
        in_specs=[row(1), row(1), row(1), row(SSM_H), row(SSM_H), row(SSM_H), row(SSM_H)],
        out_specs=[pl.BlockSpec((gb, w, w), lambda i: (i, 0, 0)),
                   pl.BlockSpec((gb, w, 4 * LANES), lambda i: (i, 0, 0)),
                   pl.BlockSpec((gb, 4 * LANES, w), lambda i: (i, 0, 0)),
                   pl.BlockSpec((gb, n_dec, LANES), lambda i: (i, 0, 0))],
        out_shape=[jax.ShapeDtypeStruct((n_grp, w, w), BF16),
                   jax.ShapeDtypeStruct((n_grp, w, 4 * LANES), BF16),
                   jax.ShapeDtypeStruct((n_grp, 4 * LANES, w), BF16),
                   jax.ShapeDtypeStruct((n_grp, n_dec, LANES), F32)],
        compiler_params=pltpu.CompilerParams(
            dimension_semantics=("arbitrary",), vmem_limit_bytes=VMEM_LIMIT),
        name="s5_prep",
    )(lr, li, ldt, bre, bim, cre, cim)


def _chunk_scan(zr, zi, a_re, a_im, reverse):
    n_rows = zr.shape[0]
    assert n_rows % SUBLANES == 0
    row = lax.broadcasted_iota(jnp.int32, (SUBLANES, 1), 0)
    edge = SUBLANES - 1 if reverse else 0

    def shifted(v, dist):
        if reverse:
            return jnp.where(row < SUBLANES - dist, pltpu.roll(v, SUBLANES - dist, 0), 0.0)
        return jnp.where(row >= dist, pltpu.roll(v, dist, 0), 0.0)

    def local_scan(vr, vi):
        for k in range(SUBLANES.bit_length() - 1):
            hr, hi = shifted(vr, 1 << k), shifted(vi, 1 << k)
            vr, vi = vr + (hr * a_re[k] - hi * a_im[k]), vi + (hr * a_im[k] + hi * a_re[k])
        return vr, vi

    def spread(v, r):
        return jnp.broadcast_to(v[r:r + 1, :], v.shape)

    unit = jnp.where(row == edge, 1.0, 0.0) + jnp.zeros((SUBLANES, zr.shape[1]), F32)
    pr, pi = local_scan(unit, jnp.zeros_like(unit))
    pr, pi = pr * a_re[0] - pi * a_im[0], pr * a_im[0] + pi * a_re[0]
    n_tiles = n_rows // SUBLANES
    order = range(n_tiles - 1, -1, -1) if reverse else range(n_tiles)
    cr = ci = jnp.zeros_like(unit)
    out_r, out_i = [None] * n_tiles, [None] * n_tiles
    for n, v in enumerate(order):
        rows = slice(SUBLANES * v, SUBLANES * (v + 1))
        sr, si = local_scan(zr[rows], zi[rows])
        sr, si = sr + (pr * cr - pi * ci), si + (pr * ci + pi * cr)
        out_r[v] = jnp.where(row == edge, cr, shifted(sr, 1))
        out_i[v] = jnp.where(row == edge, ci, shifted(si, 1))
        cr, ci = spread(sr, SUBLANES - 1 - edge), spread(si, SUBLANES - 1 - edge)
        if n % SCAN_YIELD_TILES == SCAN_YIELD_TILES - 1:
            yield
    return jnp.concatenate(out_r, axis=0), jnp.concatenate(out_i, axis=0)


def _slab_permutation():
    n = GROUPS_PER_SLAB
    idx = np.arange(n * LANES)
    a, b, h = idx // LANES, (idx % LANES) // SSM_H, idx % SSM_H
    perm = np.zeros((n * LANES, n * LANES), np.float32)
    perm[idx, b * LANES + a * SSM_H + h] = 1.0
    return jnp.asarray(perm, BF16)


def _unit_transpose(vs):
    vs = list(vs)
    n = len(vs)
    assert n * SSM_H == LANES
    unit = lax.broadcasted_iota(jnp.int32, (1, LANES), 1) // SSM_H
    d = n // 2
    while d:
        upper = (unit & d) != 0
        for i in range(n):
            if i & d:
                continue
            a, b = vs[i], vs[i + d]
            vs[i] = jnp.where(upper, pltpu.roll(b, d * SSM_H, 1), a)
            vs[i + d] = jnp.where(upper, b, pltpu.roll(a, LANES - d * SSM_H, 1))
        d //= 2
    return vs


def _s5_core_kernel(n_levels, h_ref, perm_ref, t_ref, bz_ref, cx_ref, dec_ref, y_ref, u_ref, yg_ref):
    n_grp = GROUPS_PER_SLAB
    first_half = lax.broadcasted_iota(jnp.int32, (1, LANES), 1) < SSM_P
    pairs = [(g0, g0 + 1) for g0 in range(0, n_grp, 2)]

    def gather_rows(p, o):
        v = jnp.concatenate([h_ref[n_grp * o + l8] for l8 in range(n_grp)], axis=1)
        uv = _dot(v, perm_ref[:, 2 * LANES * p:2 * LANES * (p + 1)]).astype(BF16)
        for k, g in enumerate(pairs[p]):
            u_ref[g, :, LANES * o:LANES * (o + 1)] = uv[:, LANES * k:LANES * (k + 1)]

    def pair_scan(p):
        g0, g1 = pairs[p]
        z = _dot(u_ref[g0], bz_ref[g0]) + _dot(u_ref[g1], bz_ref[g1])
        xs = []
        for direction in range(2):
            base = 2 * n_levels * direction
            a_re, a_im = ([jnp.where(first_half, dec_ref[g0, r:r + 1, :], dec_ref[g1, r:r + 1, :])
                           for r in range(lo, lo + n_levels)] for lo in (base, base + n_levels))
            cols = 2 * LANES * direction
            xs += yield from _chunk_scan(z[:, cols:cols + LANES], z[:, cols + LANES:cols + 2 * LANES],
                                         a_re, a_im, direction == 1)
        return jnp.concatenate(xs, axis=1).astype(BF16)

    lag = {}
    matmuls = []

    def lag_product(g):
        lag[g] = _dot(u_ref[g], t_ref[g])

    def state_product(g, x_all):
        yg_ref[g] = lag[g] + _dot(x_all, cx_ref[g])

    n_oct = CHUNK // n_grp
    for o in range(n_oct):
        gather_rows(0, o)
    for p in range(len(pairs)):
        if p + 1 < len(pairs):
            matmuls += [functools.partial(gather_rows, p + 1, o) for o in range(n_oct)]
        matmuls += [functools.partial(lag_product, g) for g in pairs[p]]
        scan = pair_scan(p)
        while True:
            try:
                next(scan)
            except StopIteration as done:
                x_all = done.value
                break
            if matmuls:
                matmuls.pop(0)()
        while matmuls:
            matmuls.pop(0)()
        matmuls += [functools.partial(state_product, g, x_all) for g in pairs[p]]
    while matmuls:
        matmuls.pop(0)()
    for o in range(CHUNK // n_grp):
        ys = _unit_transpose([yg_ref[g, :, LANES * o:LANES * (o + 1)] for g in range(n_grp)])
        for l8 in range(n_grp):
            y_ref[n_grp * o + l8] = ys[l8].astype(BF16)


def _s5_mix(h, lam_re, lam_im, log_dt, b_re, b_im, c_re, c_im):
    bsz, _, rows, d = h.shape
    w = CHUNK * SSM_H
    gb = GROUPS_PER_SLAB
    assert h.shape[1] == CHUNK and CHUNK % gb == 0 and d % LANES == 0
    n_levels = -(-max(1, (rows - 1).bit_length()) // SUBLANES) * SUBLANES
    t_mat, bz, cx, dec = _s5_prep(lam_re, lam_im, log_dt, b_re, b_im, c_re, c_im, n_levels)
    perm = _slab_permutation()
    slab = pl.BlockSpec((None, CHUNK, rows, LANES), lambda s, b: (b, 0, 0, s))
    grp = lambda r, c: pl.BlockSpec((gb, r, c), lambda s, b: (s, 0, 0))
    return pl.pallas_call(
        functools.partial(_s5_core_kernel, n_levels),
        grid=(d // LANES, bsz),
        in_specs=[slab, _const_spec(perm.shape), grp(w, w), grp(w, 4 * LANES), grp(4 * LANES, w),
                  grp(dec.shape[1], LANES)],
        out_specs=slab,
        out_shape=jax.ShapeDtypeStruct(h.shape, BF16),
        scratch_shapes=[pltpu.VMEM((gb, rows, w), BF16), pltpu.VMEM((gb, rows, w), F32)],
        compiler_params=pltpu.CompilerParams(
            dimension_semantics=("arbitrary", "arbitrary"), vmem_limit_bytes=VMEM_LIMIT),
        name="s5_core",
    )(h, perm, t_mat, bz, cx, dec)


def kernel(x, c, mod_w, mod_b, norm_g, ffn_w_in, ffn_w_out, pool_w, pool_b, pool_scale,
           ssm_lam_re, ssm_lam_im, ssm_log_dt, ssm_b_re, ssm_b_im, ssm_c_re, ssm_c_im,
           ssm_d, glu_w, glu_b, final_g):
    bsz, seq_len, d = x.shape
    depth = mod_w.shape[0]
    n_mixers = 2
    mods = _modulation(c, mod_w, mod_b)
    xf = x.astype(F32).reshape(bsz * seq_len, d)
    _, ffp, k_main = _ffn_dims(ffn_w_in)
    weights = (*_cast_first_weights(ffn_w_in, ffn_w_out, 0, 0), _w_out_tail(ffn_w_out, 0, 0, k_main, ffp))
    for i in range(depth):
        j = i // n_mixers
        is_s5 = i % n_mixers == 1
        last = i == depth - 1
        extras = (glu_w[j],) if is_s5 else ()
        acts, weights, extras = _layer_call(xf, mods[i], norm_g[i], weights, seq_len, emit_h=is_s5,
                                            next_ffn=(ffn_w_in, ffn_w_out, i, 1, extras))
        if is_s5:
            xf, h = acts
            ys = _s5_mix(h, ssm_lam_re[j], ssm_lam_im[j], ssm_log_dt[j], ssm_b_re[j],
                         ssm_b_im[j], ssm_c_re[j], ssm_c_im[j])
            mixer, mixer_args = "s5", (ys, ssm_d[j], extras[0], glu_b[j])
        else:
            xf = acts
            mixer, mixer_args = "pool", (pool_w[j], pool_b[j], pool_scale[j])
        res = _layer_call(xf, mods[i], norm_g[i], weights, seq_len, mixer=mixer, mixer_args=mixer_args,
                          final_g=final_g if last else None,
                          next_ffn=None if last else (ffn_w_in, ffn_w_out, i + 1, 0, ()))
        xf, weights = (res, None) if last else res[:2]
    return xf.reshape(bsz, seq_len, d).astype(x.dtype)
```

```python
import functools
import math

import numpy as np

import jax
import jax.numpy as jnp
from jax import lax
from jax.experimental import pallas as pl
from jax.experimental.pallas import tpu as pltpu

F32 = jnp.float32
BF16 = jnp.bfloat16

EPS = 1e-6
N_SUB = 3
N_MOD = 3
POOL_WINDOWS = (2, 4, 8, 16)
POOL_HALO = 8
SSM_H = 16
SSM_P = 64
CHUNK = 32
CHUNK_PITCH = CHUNK + 4
FF_CHUNK = 256
GLU_CHUNK = 512
MATMUL_PHASE = "matmul phase"
SCAN_YIELD_TILES = 8
LANES = 128
CAST_BLOCKS = 16
SUBLANES = 8
GROUPS_PER_SLAB = LANES // SSM_H
VMEM_LIMIT = 60 * 1024 * 1024


def _sigmoid(v):
    return 1.0 / (1.0 + jnp.exp(-v))


def _split3(a):
    hi = a.astype(BF16)
    lo = (a - hi.astype(F32)).astype(BF16)
    return hi, lo


def _dot(a, b):
    return jnp.dot(a, b, preferred_element_type=F32)


def _dot3(a, b):
    ah, al = _split3(a)
    bh, bl = _split3(b)
    return _dot(ah, bh) + (_dot(ah, bl) + _dot(al, bh))


def _mod_kernel(c_ref, w_ref, b_ref, o_ref):
    c = c_ref[...]
    cond = c * _sigmoid(c)
    o_ref[0] = _dot3(cond, w_ref[0]) + b_ref[0]


def _modulation(c, mod_w, mod_b):
    depth, d, n = mod_w.shape
    bsz = c.shape[0]
    rows = SUBLANES
    tn = n // 4 if n % (4 * LANES) == 0 else n
    assert n % tn == 0 and bsz <= rows
    c_pad = jnp.zeros((rows, d), F32).at[:bsz].set(c.astype(F32))
    out = pl.pallas_call(
        _mod_kernel,
        grid=(depth, n // tn),
        in_specs=[
            pl.BlockSpec((rows, d), lambda i, j: (0, 0)),
            pl.BlockSpec((1, d, tn), lambda i, j: (i, 0, j)),
            pl.BlockSpec((1, 1, tn), lambda i, j: (i, 0, j)),
        ],
        out_specs=pl.BlockSpec((1, rows, tn), lambda i, j: (i, 0, j)),
        out_shape=jax.ShapeDtypeStruct((depth, rows, n), F32),
        compiler_params=pltpu.CompilerParams(
            dimension_semantics=("arbitrary", "arbitrary"), vmem_limit_bytes=VMEM_LIMIT),
        name="adaln_modulation",
    )(c_pad, mod_w.astype(F32), mod_b.astype(F32).reshape(depth, 1, n))
    return out[:, :bsz].reshape(depth, bsz, N_SUB * N_MOD, d)


def _rms(x, g):
    return x * lax.rsqrt(jnp.mean(x * x, axis=-1, keepdims=True) + EPS) * g


def _modnorm(x, g_ref, mod, sub):
    r = N_MOD * sub
    shift, scale = mod[r:r + 1], mod[r + 1:r + 2]
    return _rms(x, g_ref[sub:sub + 1, :]) * (1.0 + scale) + shift


def _swiglu_step(x, g_ref, mod, sub, wg_ref, wu_ref, wom_ref, wot_ref, act_ref):
    h = _modnorm(x, g_ref, mod, sub).astype(BF16)
    yield MATMUL_PHASE
    n_chunks = act_ref.shape[1] // FF_CHUNK
    for k in range(n_chunks):
        cols = slice(FF_CHUNK * k, FF_CHUNK * (k + 1))
        gate = _dot(h, wg_ref[:, cols])
        up = _dot(h, wu_ref[:, cols])
        act_ref[:, cols] = (gate * _sigmoid(gate) * up).astype(BF16)
        yield
    k_main = wom_ref.shape[0]
    out = _dot(act_ref[:, :k_main], wom_ref[...])
    if act_ref.shape[1] > k_main:
        out = out + _dot(act_ref[:, k_main:], wot_ref[...])
    r = N_MOD * sub + 2
    return x + (0.5 * mod[r:r + 1]) * out


def _gelu_tanh(v):
    return 0.5 * v * (1.0 + jnp.tanh(math.sqrt(2.0 / math.pi) * (v + 0.044715 * (v * v * v))))


def _pool_fill(x_ref, xp_ref, xn_ref, g_ref, mod, hs_ref, seq_len):
    tm = x_ref.shape[0]
    tiles_per_seq = seq_len // tm
    j = pl.program_id(0) % tiles_per_seq
    hp = _modnorm(xp_ref[...], g_ref, mod, 1)
    hn = _modnorm(xn_ref[...], g_ref, mod, 1)
    hs_ref[0:POOL_HALO, :] = jnp.where(j > 0, hp, 0.0)
    hs_ref[POOL_HALO:POOL_HALO + tm, :] = _modnorm(x_ref[...], g_ref, mod, 1)
    hs_ref[POOL_HALO + tm:, :] = jnp.where(j < tiles_per_seq - 1, hn, 0.0)


def _pool_update(x, row0, tm, mod, pw_ref, pb_ref, ps_ref, hs_ref, seq_len):
    ts, d = x.shape
    cg = d // len(POOL_WINDOWS)
    j = pl.program_id(0) % (seq_len // tm)
    t = j * tm + row0 + lax.broadcasted_iota(jnp.int32, (ts, 1), 0)
    n_ext = ts + 2 * POOL_HALO
    ys = []
    for gi, window in enumerate(POOL_WINDOWS):
        left = window // 2
        right = window - 1 - left
        cols = slice(gi * cg, (gi + 1) * cg)
        assert window & (window - 1) == 0 and right < POOL_HALO
        ext = hs_ref[row0:row0 + n_ext, cols]
        run = ext
        span = 1
        while span < window:
            run = run + pltpu.roll(run, span, 0)
            span *= 2
        if right:
            run = pltpu.roll(run, n_ext - right, 0)
        total = run[POOL_HALO:POOL_HALO + ts]
        lo = jnp.maximum(t - left, 0)
        hi = jnp.minimum(t + right, seq_len - 1)
        count = (hi - lo + 1).astype(F32)
        p = total / count - ext[POOL_HALO:POOL_HALO + ts]
        ys.append(_dot(p.astype(BF16), pw_ref[gi]))
        yield
    y = (jnp.concatenate(ys, axis=1) + pb_ref[...]) * ps_ref[...]
    return x + mod[N_MOD + 2:N_MOD + 3] * y


def _s5_tail_update(x, ys, g_ref, mod, dskip_ref, gw_ref, gb_ref):
    d = x.shape[1]
    h = _modnorm(x, g_ref, mod, 1)
    y = _gelu_tanh(dskip_ref[...] * h + ys).astype(BF16)
    yield
    ms = []
    for k in range(d // GLU_CHUNK):
        cv = slice(GLU_CHUNK * k, GLU_CHUNK * (k + 1))
        cg = slice(d + GLU_CHUNK * k, d + GLU_CHUNK * (k + 1))
        val = _dot(y, gw_ref[:, cv]) + gb_ref[:, cv]
        gate = _dot(y, gw_ref[:, cg]) + gb_ref[:, cg]
        ms.append(val * _sigmoid(gate))
        yield
    return x + mod[N_MOD + 2:N_MOD + 3] * jnp.concatenate(ms, axis=1)


def _cast_weight_block(n_extra, win_ref, wout_ref, *refs):
    wg_ref, wu_ref, wom_ref = refs[n_extra:n_extra + 3]
    for src, dst in zip(refs[:n_extra], refs[n_extra + 3:]):
        dst[...] = src[...].astype(BF16)
    d_ff = win_ref.shape[1] // 2
    w = win_ref[...]
    for src, dst in ((w[:, :d_ff], wg_ref), (w[:, d_ff:], wu_ref)):
        dst[:, :d_ff] = src.astype(BF16)
        if dst.shape[1] > d_ff:
            dst[:, d_ff:] = jnp.zeros((dst.shape[0], dst.shape[1] - d_ff), BF16)
    wom_ref[...] = wout_ref[...].astype(BF16)


def _layer_kernel(mixer, emit_h, final_norm, cast_next, seq_len, n_sub, *refs):
    refs = list(refs)
    x_ref, mod_ref, g_ref = refs[:3]
    del refs[:3]
    mod = mod_ref[0]
    if mixer == "pool":
        xp_ref, xn_ref, pw_ref, pb_ref, ps_ref = refs[:5]
        del refs[:5]
    elif mixer == "s5":
        ys_ref, dskip_ref, gw_ref, gb_ref = refs[:4]
        del refs[:4]
    wg_ref, wu_ref, wom_ref, wot_ref = refs[:4]
    del refs[:4]
    if final_norm:
        fg_ref = refs.pop(0)
    if cast_next is not None:
        cast_in = refs[:2 + cast_next]
        del refs[:2 + cast_next]
    o_ref = refs.pop(0)
    if emit_h:
        h_ref = refs.pop(0)
    if cast_next is not None:
        _cast_weight_block(cast_next, *cast_in, *refs[:3 + cast_next])
        del refs[:3 + cast_next]
    act_ref = refs.pop(0)
    tm = x_ref.shape[0]
    ts = tm // n_sub
    chunks = ts // CHUNK
    if mixer == "pool":
        hs_ref = refs.pop(0)
        _pool_fill(x_ref, xp_ref, xn_ref, g_ref, mod, hs_ref, seq_len)
    elif mixer == "s5":
        rows_ref = refs.pop(0)
        n_slab = rows_ref.shape[0]
        for l in range(CHUNK):
            for s in range(n_slab):
                rows_ref[s, pl.ds(l, tm // CHUNK, stride=CHUNK_PITCH), :] = (
                    ys_ref[l, :, LANES * s:LANES * (s + 1)].astype(F32))
    if emit_h:
        hrows_ref = refs.pop(0)

    def sub_tile(si):
        row0 = si * ts
        x = x_ref[row0:row0 + ts, :]
        if mixer == "pool":
            x = yield from _pool_update(x, row0, tm, mod, pw_ref, pb_ref, ps_ref, hs_ref, seq_len)
        elif mixer == "s5":
            ys = jnp.concatenate(
                [jnp.concatenate([rows_ref[s, CHUNK_PITCH * c:CHUNK_PITCH * c + CHUNK, :]
                                  for c in range(si * chunks, (si + 1) * chunks)], axis=0)
                 for s in range(n_slab)], axis=1)
            x = yield from _s5_tail_update(x, ys, g_ref, mod, dskip_ref, gw_ref, gb_ref)
        sub = 0 if mixer is None else 2
        x = yield from _swiglu_step(x, g_ref, mod, sub, wg_ref, wu_ref, wom_ref, wot_ref, act_ref.at[si])
        if final_norm:
            x = _rms(x, fg_ref[...])
        o_ref[row0:row0 + ts, :] = x
        if emit_h:
            h = _modnorm(x, g_ref, mod, 1)
            for c in range(chunks):
                base = CHUNK_PITCH * (si * chunks + c)
                for s in range(hrows_ref.shape[0]):
                    hrows_ref[s, base:base + CHUNK, :] = h[CHUNK * c:CHUNK * (c + 1), LANES * s:LANES * (s + 1)]

    tiles = [sub_tile(si) for si in range(n_sub)]
    pending = [True] * n_sub
    for marker in tiles[0]:
        if marker == MATMUL_PHASE:
            pending[0] = False
            break
    for si, tile in enumerate(tiles):
        nxt = si + 1 if si + 1 < n_sub else None
        for _ in tile:
            if nxt is not None and pending[nxt]:
                pending[nxt] = next(tiles[nxt]) != MATMUL_PHASE
        while nxt is not None and pending[nxt]:
            pending[nxt] = next(tiles[nxt]) != MATMUL_PHASE
    if emit_h:
        for l in range(CHUNK):
            for s in range(hrows_ref.shape[0]):
                h_ref[l, :, LANES * s:LANES * (s + 1)] = (
                    hrows_ref[s, pl.ds(l, tm // CHUNK, stride=CHUNK_PITCH), :].astype(BF16))


def _const_spec(shape):
    nd = len(shape)
    return pl.BlockSpec(shape, lambda *_: (0,) * nd, pipeline_mode=pl.Buffered(1))


def _ffn_dims(ffn_w_in):
    d_ff = ffn_w_in.shape[-1] // 2
    return d_ff, d_ff + -d_ff % FF_CHUNK, d_ff // FF_CHUNK * FF_CHUNK


def _cast_io(ffn_w_in, ffn_w_out, layer, which, extras, steps):
    d = ffn_w_in.shape[2]
    d_ff, ffp, k_main = _ffn_dims(ffn_w_in)
    n = 1
    while n < CAST_BLOCKS and steps % (2 * n) == 0:
        n *= 2
    assert d % (n * 2 * SUBLANES) == 0 and k_main % (n * 2 * SUBLANES) == 0
    blk = lambda i: i * n // steps
    in_specs = [pl.BlockSpec((None, None, d // n, 2 * d_ff), lambda i: (layer, which, blk(i), 0)),
                pl.BlockSpec((None, None, k_main // n, d), lambda i: (layer, which, blk(i), 0))]
    out_specs = [pl.BlockSpec((d // n, ffp), lambda i: (blk(i), 0)),
                 pl.BlockSpec((d // n, ffp), lambda i: (blk(i), 0)),
                 pl.BlockSpec((k_main // n, d), lambda i: (blk(i), 0))]
    out_shape = [jax.ShapeDtypeStruct((d, ffp), BF16), jax.ShapeDtypeStruct((d, ffp), BF16),
                 jax.ShapeDtypeStruct((k_main, d), BF16)]
    for m in extras:
        rows, cols = m.shape
        assert rows % (n * 2 * SUBLANES) == 0
        in_specs.append(pl.BlockSpec((rows // n, cols), lambda i: (blk(i), 0)))
        out_specs.append(pl.BlockSpec((rows // n, cols), lambda i: (blk(i), 0)))
        out_shape.append(jax.ShapeDtypeStruct(m.shape, BF16))
    args = [ffn_w_in.astype(F32), ffn_w_out.astype(F32)] + [m.astype(F32) for m in extras]
    return args, in_specs, out_specs, out_shape


def _w_out_tail(ffn_w_out, layer, which, k_main, ffp):
    tail = lax.optimization_barrier(ffn_w_out[layer, which, k_main:, :]).astype(BF16)
    return jnp.pad(tail, ((0, ffp - k_main - tail.shape[0]), (0, 0)))


def _cast_first_weights(ffn_w_in, ffn_w_out, layer, which):
    args, in_specs, out_specs, out_shape = _cast_io(ffn_w_in, ffn_w_out, layer, which, (), CAST_BLOCKS)
    return pl.pallas_call(
        functools.partial(_cast_weight_block, 0),
        grid=(CAST_BLOCKS,),
        in_specs=in_specs,
        out_specs=out_specs,
        out_shape=out_shape,
        compiler_params=pltpu.CompilerParams(
            dimension_semantics=("arbitrary",), vmem_limit_bytes=VMEM_LIMIT),
        name="weight_cast",
    )(*args)


def _tile_plan(seq_len, mixer, emit_h):
    tm = min(1024, seq_len)
    ts = {None: 512 if emit_h else 1024, "pool": 256, "s5": 512}[mixer]
    return tm, max(1, tm // ts)


def _layer_call(x, mod, norm_g, weights, seq_len, *, mixer=None, mixer_args=(),
                emit_h=False, final_g=None, next_ffn=None):
    n_tok, d = x.shape
    tm, n_sub = _tile_plan(seq_len, mixer, emit_h)
    assert seq_len % tm == 0 and tm % (n_sub * CHUNK * SUBLANES) == 0
    tiles_per_seq = seq_len // tm
    wg, wu, wom, wot = weights
    ffp = wg.shape[1]
    row_spec = pl.BlockSpec((tm, d), lambda i: (i, 0))
    chunk_spec = pl.BlockSpec((None, CHUNK, tm // CHUNK, d),
                              lambda i: (i // tiles_per_seq, 0, i % tiles_per_seq, 0))
    args = [x, mod, norm_g.astype(F32)]
    specs = [row_spec,
             pl.BlockSpec((1, N_SUB * N_MOD, d), lambda i: (i // tiles_per_seq, 0, 0)),
             _const_spec((N_SUB, d))]
    scratch = [pltpu.VMEM((n_sub, tm // n_sub, ffp), BF16)]
    if mixer == "pool":
        pool_w, pool_b, pool_scale = mixer_args
        hb = tm // POOL_HALO
        last = n_tok // POOL_HALO - 1
        args += [x, x, pool_w.astype(BF16), pool_b.astype(F32).reshape(1, d),
                 pool_scale.astype(F32).reshape(1, d)]
        specs += [pl.BlockSpec((POOL_HALO, d), lambda i: (jnp.maximum(i * hb - 1, 0), 0)),
                  pl.BlockSpec((POOL_HALO, d), lambda i: (jnp.minimum((i + 1) * hb, last), 0)),
                  _const_spec(pool_w.shape), _const_spec((1, d)), _const_spec((1, d))]
        scratch.append(pltpu.VMEM((tm + 2 * POOL_HALO, d), F32))
    elif mixer == "s5":
        ys, d_skip, glu_w, glu_b = mixer_args
        args += [ys, d_skip.astype(F32).reshape(1, d), glu_w.astype(BF16),
                 glu_b.astype(F32).reshape(1, 2 * d)]
        specs += [chunk_spec, _const_spec((1, d)), _const_spec((d, 2 * d)), _const_spec((1, 2 * d))]
        scratch.append(pltpu.VMEM((d // LANES, tm // CHUNK * CHUNK_PITCH, LANES), F32))
    args += [wg, wu, wom, wot]
    specs += [_const_spec(w.shape) for w in (wg, wu, wom, wot)]
    if final_g is not None:
        args.append(final_g.astype(F32).reshape(1, d))
        specs.append(_const_spec((1, d)))
    if next_ffn is not None:
        cast_args, cast_in, cast_out, cast_shape = _cast_io(*next_ffn, n_tok // tm)
        args += cast_args
        specs += cast_in
    out_shape = [jax.ShapeDtypeStruct((n_tok, d), F32)]
    out_specs = [row_spec]
    if emit_h:
        out_shape.append(jax.ShapeDtypeStruct((n_tok // seq_len, CHUNK, seq_len // CHUNK, d), BF16))
        out_specs.append(chunk_spec)
        scratch.append(pltpu.VMEM((d // LANES, tm // CHUNK * CHUNK_PITCH, LANES), F32))
    if next_ffn is not None:
        out_shape += cast_shape
        out_specs += cast_out
    outs = pl.pallas_call(
        functools.partial(_layer_kernel, mixer, emit_h, final_g is not None,
                          None if next_ffn is None else len(next_ffn[4]), seq_len, n_sub),
        grid=(n_tok // tm,),
        in_specs=specs,
        out_specs=out_specs,
        out_shape=out_shape,
        scratch_shapes=scratch,
        compiler_params=pltpu.CompilerParams(
            dimension_semantics=("arbitrary",), vmem_limit_bytes=VMEM_LIMIT),
        name="layer_" + (mixer or "ffn"),
    )(*args)
    n_act = 2 if emit_h else 1
    acts = tuple(outs[:n_act]) if emit_h else outs[0]
    if next_ffn is None:
        return acts
    _, ffp_next, k_main = _ffn_dims(next_ffn[0])
    tail = _w_out_tail(next_ffn[1], next_ffn[2], next_ffn[3], k_main, ffp_next)
    return acts, (*outs[n_act:n_act + 3], tail), tuple(outs[n_act + 3:])


def _cmul(ar, ai, br, bi):
    return ar * br - ai * bi, ar * bi + ai * br


def _s5_prep_kernel(n_levels, lr_ref, li_ref, ldt_ref, bre_ref, bim_ref, cre_ref, cim_ref,
                    t_ref, bz_ref, cx_ref, dec_ref):
    gb = lr_ref.shape[1]
    w = t_ref.shape[1]
    chunk = w // SSM_H
    n_dbl = chunk.bit_length() - 1
    assert chunk == 1 << n_dbl
    lane = lax.broadcasted_iota(jnp.int32, (1, LANES), 1)
    re_half = lane < SSM_P
    wlane = lax.broadcasted_iota(jnp.int32, (1, w), 1)
    step_e = (1 << lax.broadcasted_iota(jnp.int32, (SUBLANES, 1), 0)).astype(F32)
    level_e = (chunk << lax.broadcasted_iota(jnp.int32, (n_levels, 1), 0)).astype(F32)
    assert n_dbl < SUBLANES

    def flip(v):
        return jnp.concatenate([v[SSM_H * l:SSM_H * (l + 1)] for l in reversed(range(chunk))], axis=0)

    for g in range(gb):
        rs, zs, cs = [], [], []
        for direction in range(2):
            lr, li = lr_ref[direction, g], li_ref[direction, g]
            dt = jnp.exp(ldt_ref[direction, g])
            lrdt, lidt = lr * dt, li * dt

            def a_pow(e):
                mag = jnp.exp(lrdt * e)
                return mag * jnp.cos(lidt * e), mag * jnp.sin(lidt * e)

            step_r, step_i = a_pow(step_e)
            ar, ai = step_r[0:1], step_i[0:1]
            den = lr * lr + li * li
            fr = ((ar - 1.0) * lr + ai * li) / den
            fi = (ai * lr - (ar - 1.0) * li) / den
            bre, bim = bre_ref[direction, g], bim_ref[direction, g]
            bbr, bbi = _cmul(fr, fi, bre, bim)
            cre16, cim16 = cre_ref[direction, g], cim_ref[direction, g]

            b_r, b_i, c_r, c_i = bbr, bbi, cre16, cim16
            for k in range(n_dbl):
                sr, si = step_r[k:k + 1], step_i[k:k + 1]
                nr, ni = _cmul(b_r, b_i, sr, si)
                b_r, b_i = jnp.concatenate([b_r, nr], axis=0), jnp.concatenate([b_i, ni], axis=0)
                nr, ni = _cmul(c_r, c_i, sr, si)
                c_r, c_i = jnp.concatenate([c_r, nr], axis=0), jnp.concatenate([c_i, ni], axis=0)
            top_r, top_i = _cmul(cre16, cim16, step_r[n_dbl:n_dbl + 1], step_i[n_dbl:n_dbl + 1])
            c1_r = jnp.concatenate([c_r[SSM_H:], top_r], axis=0)
            c1_i = jnp.concatenate([c_i[SSM_H:], top_i], axis=0)

            def mine(v):
                return jnp.where(re_half if g % 2 == 0 else ~re_half, v, 0.0)

            if direction == 0:
                b_r, b_i = flip(b_r), flip(b_i)
            else:
                c1_r, c1_i, c_r, c_i = flip(c1_r), flip(c1_i), flip(c_r), flip(c_i)
            zs += [mine(b_r), mine(b_i)]
            cs += [mine(c1_r).T, mine(-c1_i).T]
            gmat = jnp.where(re_half, c_r, -c_i).T
            bbt = jnp.where(re_half, bbr, bbi)
            rs.append(_dot3(bbt, gmat))
            er, ei = a_pow(level_e)
            base = 2 * n_levels * direction
            dec_ref[g, base:base + n_levels, :] = er
            dec_ref[g, base + n_levels:base + 2 * n_levels, :] = ei
        for i in range(chunk):
            s = SSM_H * i
            fwd_rows = jnp.where(wlane >= s, pltpu.roll(rs[0], s, 1) if s else rs[0], 0.0)
            e = (s + SSM_H) % w
            bwd_rows = jnp.where(wlane < s + SSM_H, pltpu.roll(rs[1], e, 1) if e else rs[1], 0.0)
            t_ref[g, s:s + SSM_H, :] = (fwd_rows + bwd_rows).astype(BF16)
        bz_ref[g] = jnp.concatenate(zs, axis=1).astype(BF16)
        cx_ref[g] = jnp.concatenate(cs, axis=0).astype(BF16)


def _s5_prep(lam_re, lam_im, log_dt, b_re, b_im, c_re, c_im, n_levels):
    n_dir, n_grp, n_state = lam_re.shape
    assert n_dir == 2 and n_state == SSM_P and b_re.shape[-1] == SSM_H
    w = CHUNK * SSM_H
    gb = GROUPS_PER_SLAB
    assert n_grp % gb == 0 and n_levels % SUBLANES == 0

    def twice(v):
        return jnp.concatenate([v, v], axis=-1).astype(F32)

    lr = twice(lam_re)[:, :, None, :]
    li = twice(lam_im)[:, :, None, :]
    ldt = jnp.broadcast_to(log_dt.astype(F32)[:, :, None, None], (n_dir, n_grp, 1, LANES))
    bre, bim = twice(jnp.swapaxes(b_re, 2, 3)), twice(jnp.swapaxes(b_im, 2, 3))
    cre, cim = twice(c_re), twice(c_im)
    n_dec = 2 * 2 * n_levels
    row = lambda r: pl.BlockSpec((n_dir, gb, r, LANES), lambda i: (0, i, 0, 0))
    return pl.pallas_call(
        functools.partial(_s5_prep_kernel, n_levels),
        grid=(n_grp // gb,),
        in_specs=[row(1), row(1), row(1), row(SSM_H), row(SSM_H), row(SSM_H), row(SSM_H)],
        out_specs=[pl.BlockSpec((gb, w, w), lambda i: (i, 0, 0)),
                   pl.BlockSpec((gb, w, 4 * LANES), lambda i: (i, 0, 0)),
                   pl.BlockSpec((gb, 4 * LANES, w), lambda i: (i, 0, 0)),
                   pl.BlockSpec((gb, n_dec, LANES), lambda i: (i, 0, 0))],
        out_shape=[jax.ShapeDtypeStruct((n_grp, w, w), BF16),
                   jax.ShapeDtypeStruct((n_grp, w, 4 * LANES), BF16),
                   jax.ShapeDtypeStruct((n_grp, 4 * LANES, w), BF16),
                   jax.ShapeDtypeStruct((n_grp, n_dec, LANES), F32)],
        compiler_params=pltpu.CompilerParams(
            dimension_semantics=("arbitrary",), vmem_limit_bytes=VMEM_LIMIT),
        name="s5_prep",
    )(lr, li, ldt, bre, bim, cre, cim)


def _chunk_scan(zr, zi, a_re, a_im, reverse):
    n_rows = zr.shape[0]
    assert n_rows % SUBLANES == 0
    row = lax.broadcasted_iota(jnp.int32, (SUBLANES, 1), 0)
    edge = SUBLANES - 1 if reverse else 0

    def shifted(v, dist):
        if reverse:
            return jnp.where(row < SUBLANES - dist, pltpu.roll(v, SUBLANES - dist, 0), 0.0)
        return jnp.where(row >= dist, pltpu.roll(v, dist, 0), 0.0)

    def local_scan(vr, vi):
        for k in range(SUBLANES.bit_length() - 1):
            hr, hi = shifted(vr, 1 << k), shifted(vi, 1 << k)
            vr, vi = vr + (hr * a_re[k] - hi * a_im[k]), vi + (hr * a_im[k] + hi * a_re[k])
        return vr, vi

    def spread(v, r):
        return jnp.broadcast_to(v[r:r + 1, :], v.shape)

    unit = jnp.where(row == edge, 1.0, 0.0) + jnp.zeros((SUBLANES, zr.shape[1]), F32)
    pr, pi = local_scan(unit, jnp.zeros_like(unit))
    pr, pi = pr * a_re[0] - pi * a_im[0], pr * a_im[0] + pi * a_re[0]
    n_tiles = n_rows // SUBLANES
    order = range(n_tiles - 1, -1, -1) if reverse else range(n_tiles)
    cr = ci = jnp.zeros_like(unit)
    out_r, out_i = [None] * n_tiles, [None] * n_tiles
    for n, v in enumerate(order):
        rows = slice(SUBLANES * v, SUBLANES * (v + 1))
        sr, si = local_scan(zr[rows], zi[rows])
        sr, si = sr + (pr * cr - pi * ci), si + (pr * ci + pi * cr)
        out_r[v] = jnp.where(row == edge, cr, shifted(sr, 1))
        out_i[v] = jnp.where(row == edge, ci, shifted(si, 1))
        cr, ci = spread(sr, SUBLANES - 1 - edge), spread(si, SUBLANES - 1 - edge)
        if n % SCAN_YIELD_TILES == SCAN_YIELD_TILES - 1:
            yield
    return jnp.concatenate(out_r, axis=0), jnp.concatenate(out_i, axis=0)


def _slab_permutation():
    n = GROUPS_PER_SLAB
    idx = np.arange(n * LANES)
    a, b, h = idx // LANES, (idx % LANES) // SSM_H, idx % SSM_H
    perm = np.zeros((n * LANES, n * LANES), np.float32)
    perm[idx, b * LANES + a * SSM_H + h] = 1.0
    return jnp.asarray(perm, BF16)


def _unit_transpose(vs):
    vs = list(vs)
    n = len(vs)
    assert n * SSM_H == LANES
    unit = lax.broadcasted_iota(jnp.int32, (1, LANES), 1) // SSM_H
    d = n // 2
    while d:
        upper = (unit & d) != 0
        for i in range(n):
            if i & d:
                continue
            a, b = vs[i], vs[i + d]
            vs[i] = jnp.where(upper, pltpu.roll(b, d * SSM_H, 1), a)
            vs[i + d] = jnp.where(upper, b, pltpu.roll(a, LANES - d * SSM_H, 1))
        d //= 2
    return vs


def _s5_core_kernel(n_levels, h_ref, perm_ref, t_ref, bz_ref, cx_ref, dec_ref, y_ref, u_ref, yg_ref):
    n_grp = GROUPS_PER_SLAB
    first_half = lax.broadcasted_iota(jnp.int32, (1, LANES), 1) < SSM_P
    pairs = [(g0, g0 + 1) for g0 in range(0, n_grp, 2)]

    def gather_rows(p, o):
        v = jnp.concatenate([h_ref[n_grp * o + l8] for l8 in range(n_grp)], axis=1)
        uv = _dot(v, perm_ref[:, 2 * LANES * p:2 * LANES * (p + 1)]).astype(BF16)
        for k, g in enumerate(pairs[p]):
            u_ref[g, :, LANES * o:LANES * (o + 1)] = uv[:, LANES * k:LANES * (k + 1)]

    def pair_scan(p):
        g0, g1 = pairs[p]
        z = _dot(u_ref[g0], bz_ref[g0]) + _dot(u_ref[g1], bz_ref[g1])
        xs = []
        for direction in range(2):
            base = 2 * n_levels * direction
            a_re, a_im = ([jnp.where(first_half, dec_ref[g0, r:r + 1, :], dec_ref[g1, r:r + 1, :])
                           for r in range(lo, lo + n_levels)] for lo in (base, base + n_levels))
            cols = 2 * LANES * direction
            xs += yield from _chunk_scan(z[:, cols:cols + LANES], z[:, cols + LANES:cols + 2 * LANES],
                                         a_re, a_im, direction == 1)
        return jnp.concatenate(xs, axis=1).astype(BF16)

    lag = {}
    matmuls = []

    def lag_product(g):
        lag[g] = _dot(u_ref[g], t_ref[g])

    def state_product(g, x_all):
        yg_ref[g] = lag[g] + _dot(x_all, cx_ref[g])

    n_oct = CHUNK // n_grp
    for o in range(n_oct):
        gather_rows(0, o)
    for p in range(len(pairs)):
        if p + 1 < len(pairs):
            matmuls += [functools.partial(gather_rows, p + 1, o) for o in range(n_oct)]
        matmuls += [functools.partial(lag_product, g) for g in pairs[p]]
        scan = pair_scan(p)
        while True:
            try:
                next(scan)
            except StopIteration as done:
                x_all = done.value
                break
            if matmuls:
                matmuls.pop(0)()
        while matmuls:
            matmuls.pop(0)()
        matmuls += [functools.partial(state_product, g, x_all) for g in pairs[p]]
    while matmuls:
        matmuls.pop(0)()
    for o in range(CHUNK // n_grp):
        ys = _unit_transpose([yg_ref[g, :, LANES * o:LANES * (o + 1)] for g in range(n_grp)])
        for l8 in range(n_grp):
            y_ref[n_grp * o + l8] = ys[l8].astype(BF16)


def _s5_mix(h, lam_re, lam_im, log_dt, b_re, b_im, c_re, c_im):
    bsz, _, rows, d = h.shape
    w = CHUNK * SSM_H
    gb = GROUPS_PER_SLAB
    assert h.shape[1] == CHUNK and CHUNK % gb == 0 and d % LANES == 0
    n_levels = -(-max(1, (rows - 1).bit_length()) // SUBLANES) * SUBLANES
    t_mat, bz, cx, dec = _s5_prep(lam_re, lam_im, log_dt, b_re, b_im, c_re, c_im, n_levels)
    perm = _slab_permutation()
    slab = pl.BlockSpec((None, CHUNK, rows, LANES), lambda s, b: (b, 0, 0, s))
    grp = lambda r, c: pl.BlockSpec((gb, r, c), lambda s, b: (s, 0, 0))
    return pl.pallas_call(
        functools.partial(_s5_core_kernel, n_levels),
        grid=(d // LANES, bsz),
        in_specs=[slab, _const_spec(perm.shape), grp(w, w), grp(w, 4 * LANES), grp(4 * LANES, w),
                  grp(dec.shape[1], LANES)],
        out_specs=slab,
        out_shape=jax.ShapeDtypeStruct(h.shape, BF16),
        scratch_shapes=[pltpu.VMEM((gb, rows, w), BF16), pltpu.VMEM((gb, rows, w), F32)],
        compiler_params=pltpu.CompilerParams(
            dimension_semantics=("arbitrary", "arbitrary"), vmem_limit_bytes=VMEM_LIMIT),
        name="s5_core",
    )(h, perm, t_mat, bz, cx, dec)


def kernel(x, c, mod_w, mod_b, norm_g, ffn_w_in, ffn_w_out, pool_w, pool_b, pool_scale,
           ssm_lam_re, ssm_lam_im, ssm_log_dt, ssm_b_re, ssm_b_im, ssm_c_re, ssm_c_im,
           ssm_d, glu_w, glu_b, final_g):
    bsz, seq_len, d = x.shape
    depth = mod_w.shape[0]
    n_mixers = 2
    mods = _modulation(c, mod_w, mod_b)
    xf = x.astype(F32).reshape(bsz * seq_len, d)
    _, ffp, k_main = _ffn_dims(ffn_w_in)
    weights = (*_cast_first_weights(ffn_w_in, ffn_w_out, 0, 0), _w_out_tail(ffn_w_out, 0, 0, k_main, ffp))
    for i in range(depth):
        j = i // n_mixers
        is_s5 = i % n_mixers == 1
        last = i == depth - 1
        extras = (glu_w[j],) if is_s5 else ()
        acts, weights, extras = _layer_call(xf, mods[i], norm_g[i], weights, seq_len, emit_h=is_s5,
                                            next_ffn=(ffn_w_in, ffn_w_out, i, 1, extras))
        if is_s5:
            xf, h = acts
            ys = _s5_mix(h, ssm_lam_re[j], ssm_lam_im[j], ssm_log_dt[j], ssm_b_re[j],
                         ssm_b_im[j], ssm_c_re[j], ssm_c_im[j])
            mixer, mixer_args = "s5", (ys, ssm_d[j], extras[0], glu_b[j])
        else:
            xf = acts
            mixer, mixer_args = "pool", (pool_w[j], pool_b[j], pool_scale[j])
        res = _layer_call(xf, mods[i], norm_g[i], weights, seq_len, mixer=mixer, mixer_args=mixer_args,
                          final_g=final_g if last else None,
                          next_ffn=None if last else (ffn_w_in, ffn_w_out, i + 1, 0, ()))
        xf, weights = (res, None) if last else res[:2]
    return xf.reshape(bsz, seq_len, d).astype(x.dtype)
```

```python
import functools
import math

import numpy as np

import jax
import jax.numpy as jnp
from jax import lax
from jax.experimental import pallas as pl
from jax.experimental.pallas import tpu as pltpu

F32 = jnp.float32
BF16 = jnp.bfloat16

EPS = 1e-6
N_SUB = 3
N_MOD = 3
POOL_WINDOWS = (2, 4, 8, 16)
POOL_HALO = 8
SSM_H = 16
SSM_P = 64
CHUNK = 32
CHUNK_PITCH = CHUNK + 4
FF_CHUNK = 256
GLU_CHUNK = 512
MATMUL_PHASE = "matmul phase"
SCAN_YIELD_TILES = 8
LANES = 128
CAST_BLOCKS = 16
SUBLANES = 8
GROUPS_PER_SLAB = LANES // SSM_H
VMEM_LIMIT = 60 * 1024 * 1024


def _sigmoid(v):
    return 1.0 / (1.0 + jnp.exp(-v))


def _split3(a):
    hi = a.astype(BF16)
    lo = (a - hi.astype(F32)).astype(BF16)
    return hi, lo


def _dot(a, b):
    return jnp.dot(a, b, preferred_element_type=F32)


def _dot3(a, b):
    ah, al = _split3(a)
    bh, bl = _split3(b)
    return _dot(ah, bh) + (_dot(ah, bl) + _dot(al, bh))


def _mod_kernel(c_ref, w_ref, b_ref, o_ref):
    c = c_ref[...]
    cond = c * _sigmoid(c)
    o_ref[0] = _dot3(cond, w_ref[0]) + b_ref[0]


def _modulation(c, mod_w, mod_b):
    depth, d, n = mod_w.shape
    bsz = c.shape[0]
    rows = SUBLANES
    tn = n // 4 if n % (4 * LANES) == 0 else n
    assert n % tn == 0 and bsz <= rows
    c_pad = jnp.zeros((rows, d), F32).at[:bsz].set(c.astype(F32))
    out = pl.pallas_call(
        _mod_kernel,
        grid=(depth, n // tn),
        in_specs=[
            pl.BlockSpec((rows, d), lambda i, j: (0, 0)),
            pl.BlockSpec((1, d, tn), lambda i, j: (i, 0, j)),
            pl.BlockSpec((1, 1, tn), lambda i, j: (i, 0, j)),
        ],
        out_specs=pl.BlockSpec((1, rows, tn), lambda i, j: (i, 0, j)),
        out_shape=jax.ShapeDtypeStruct((depth, rows, n), F32),
        compiler_params=pltpu.CompilerParams(
            dimension_semantics=("arbitrary", "arbitrary"), vmem_limit_bytes=VMEM_LIMIT),
        name="adaln_modulation",
    )(c_pad, mod_w.astype(F32), mod_b.astype(F32).reshape(depth, 1, n))
    return out[:, :bsz].reshape(depth, bsz, N_SUB * N_MOD, d)


def _rms(x, g):
    return x * lax.rsqrt(jnp.mean(x * x, axis=-1, keepdims=True) + EPS) * g


def _modnorm(x, g_ref, mod, sub):
    r = N_MOD * sub
    shift, scale = mod[r:r + 1], mod[r + 1:r + 2]
    return _rms(x, g_ref[sub:sub + 1, :]) * (1.0 + scale) + shift


def _swiglu_step(x, g_ref, mod, sub, wg_ref, wu_ref, wom_ref, wot_ref, act_ref):
    h = _modnorm(x, g_ref, mod, sub).astype(BF16)
    yield MATMUL_PHASE
    n_chunks = act_ref.shape[1] // FF_CHUNK
    for k in range(n_chunks):
        cols = slice(FF_CHUNK * k, FF_CHUNK * (k + 1))
        gate = _dot(h, wg_ref[:, cols])
        up = _dot(h, wu_ref[:, cols])
        act_ref[:, cols] = (gate * _sigmoid(gate) * up).astype(BF16)
        yield
    k_main = wom_ref.shape[0]
    out = _dot(act_ref[:, :k_main], wom_ref[...])
    if act_ref.shape[1] > k_main:
        out = out + _dot(act_ref[:, k_main:], wot_ref[...])
    r = N_MOD * sub + 2
    return x + (0.5 * mod[r:r + 1]) * out


def _gelu_tanh(v):
    return 0.5 * v * (1.0 + jnp.tanh(math.sqrt(2.0 / math.pi) * (v + 0.044715 * (v * v * v))))


def _pool_update(x, row0, x_ref, xp_ref, xn_ref, g_ref, mod, pw_ref, pb_ref, ps_ref, seq_len):
    ts, d = x.shape
    tm = x_ref.shape[0]
    cg = d // len(POOL_WINDOWS)
    tiles_per_seq = seq_len // tm
    j = pl.program_id(0) % tiles_per_seq
    t = j * tm + row0 + lax.broadcasted_iota(jnp.int32, (ts, 1), 0)
    n_ext = ts + 2 * POOL_HALO
    if row0 == 0:
        before = jnp.where(j > 0, _modnorm(xp_ref[...], g_ref, mod, 1), 0.0)
    else:
        before = _modnorm(x_ref[row0 - POOL_HALO:row0, :], g_ref, mod, 1)
    if row0 + ts == tm:
        after = jnp.where(j < tiles_per_seq - 1, _modnorm(xn_ref[...], g_ref, mod, 1), 0.0)
    else:
        after = _modnorm(x_ref[row0 + ts:row0 + ts + POOL_HALO, :], g_ref, mod, 1)
    h_ext = jnp.concatenate([before, _modnorm(x, g_ref, mod, 1), after], axis=0)
    ys = []
    for gi, window in enumerate(POOL_WINDOWS):
        left = window // 2
        right = window - 1 - left
        cols = slice(gi * cg, (gi + 1) * cg)
        assert window & (window - 1) == 0 and right < POOL_HALO
        ext = h_ext[:, cols]
        run = ext
        span = 1
        while span < window:
            run = run + pltpu.roll(run, span, 0)
            span *= 2
        if right:
            run = pltpu.roll(run, n_ext - right, 0)
        total = run[POOL_HALO:POOL_HALO + ts]
        lo = jnp.maximum(t - left, 0)
        hi = jnp.minimum(t + right, seq_len - 1)
        count = (hi - lo + 1).astype(F32)
        p = total / count - ext[POOL_HALO:POOL_HALO + ts]
        ys.append(_dot(p.astype(BF16), pw_ref[gi]))
        yield
    y = (jnp.concatenate(ys, axis=1) + pb_ref[...]) * ps_ref[...]
    return x + mod[N_MOD + 2:N_MOD + 3] * y


def _s5_tail_update(x, ys, g_ref, mod, dskip_ref, gw_ref, gb_ref):
    d = x.shape[1]
    h = _modnorm(x, g_ref, mod, 1)
    y = _gelu_tanh(dskip_ref[...] * h + ys).astype(BF16)
    yield
    ms = []
    for k in range(d // GLU_CHUNK):
        cv = slice(GLU_CHUNK * k, GLU_CHUNK * (k + 1))
        cg = slice(d + GLU_CHUNK * k, d + GLU_CHUNK * (k + 1))
        val = _dot(y, gw_ref[:, cv]) + gb_ref[:, cv]
        gate = _dot(y, gw_ref[:, cg]) + gb_ref[:, cg]
        ms.append(val * _sigmoid(gate))
        yield
    return x + mod[N_MOD + 2:N_MOD + 3] * jnp.concatenate(ms, axis=1)


def _cast_weight_block(n_extra, win_ref, wout_ref, *refs):
    wg_ref, wu_ref, wom_ref = refs[n_extra:n_extra + 3]
    for src, dst in zip(refs[:n_extra], refs[n_extra + 3:]):
        dst[...] = src[...].astype(BF16)
    d_ff = win_ref.shape[1] // 2
    w = win_ref[...]
    for src, dst in ((w[:, :d_ff], wg_ref), (w[:, d_ff:], wu_ref)):
        dst[:, :d_ff] = src.astype(BF16)
        if dst.shape[1] > d_ff:
            dst[:, d_ff:] = jnp.zeros((dst.shape[0], dst.shape[1] - d_ff), BF16)
    wom_ref[...] = wout_ref[...].astype(BF16)


def _layer_kernel(mixer, emit_h, final_norm, cast_next, seq_len, n_sub, *refs):
    refs = list(refs)
    x_ref, mod_ref, g_ref = refs[:3]
    del refs[:3]
    mod = mod_ref[0]
    if mixer == "pool":
        xp_ref, xn_ref, pw_ref, pb_ref, ps_ref = refs[:5]
        del refs[:5]
    elif mixer == "s5":
        ys_ref, dskip_ref, gw_ref, gb_ref = refs[:4]
        del refs[:4]
    wg_ref, wu_ref, wom_ref, wot_ref = refs[:4]
    del refs[:4]
    if final_norm:
        fg_ref = refs.pop(0)
    if cast_next is not None:
        cast_in = refs[:2 + cast_next]
        del refs[:2 + cast_next]
    o_ref = refs.pop(0)
    if emit_h:
        h_ref = refs.pop(0)
    if cast_next is not None:
        _cast_weight_block(cast_next, *cast_in, *refs[:3 + cast_next])
        del refs[:3 + cast_next]
    act_ref = refs.pop(0)
    tm = x_ref.shape[0]
    ts = tm // n_sub
    chunks = ts // CHUNK
    if mixer == "s5":
        rows_ref = refs.pop(0)
        n_slab = rows_ref.shape[0]
        for l in range(CHUNK):
            for s in range(n_slab):
                rows_ref[s, pl.ds(l, tm // CHUNK, stride=CHUNK_PITCH), :] = (
                    ys_ref[l, :, LANES * s:LANES * (s + 1)].astype(F32))
    if emit_h:
        hrows_ref = refs.pop(0)

    def sub_tile(si):
        row0 = si * ts
        x = x_ref[row0:row0 + ts, :]
        if mixer == "pool":
            x = yield from _pool_update(x, row0, x_ref, xp_ref, xn_ref, g_ref, mod, pw_ref, pb_ref, ps_ref,
                                        seq_len)
        elif mixer == "s5":
            ys = jnp.concatenate(
                [jnp.concatenate([rows_ref[s, CHUNK_PITCH * c:CHUNK_PITCH * c + CHUNK, :]
                                  for c in range(si * chunks, (si + 1) * chunks)], axis=0)
                 for s in range(n_slab)], axis=1)
            x = yield from _s5_tail_update(x, ys, g_ref, mod, dskip_ref, gw_ref, gb_ref)
        sub = 0 if mixer is None else 2
        x = yield from _swiglu_step(x, g_ref, mod, sub, wg_ref, wu_ref, wom_ref, wot_ref, act_ref.at[si])
        if final_norm:
            x = _rms(x, fg_ref[...])
        o_ref[row0:row0 + ts, :] = x
        if emit_h:
            h = _modnorm(x, g_ref, mod, 1)
            for c in range(chunks):
                base = CHUNK_PITCH * (si * chunks + c)
                for s in range(hrows_ref.shape[0]):
                    hrows_ref[s, base:base + CHUNK, :] = h[CHUNK * c:CHUNK * (c + 1), LANES * s:LANES * (s + 1)]

    tiles = [sub_tile(si) for si in range(n_sub)]
    pending = [True] * n_sub
    for marker in tiles[0]:
        if marker == MATMUL_PHASE:
            pending[0] = False
            break
    for si, tile in enumerate(tiles):
        nxt = si + 1 if si + 1 < n_sub else None
        for _ in tile:
            if nxt is not None and pending[nxt]:
                pending[nxt] = next(tiles[nxt]) != MATMUL_PHASE
        while nxt is not None and pending[nxt]:
            pending[nxt] = next(tiles[nxt]) != MATMUL_PHASE
    if emit_h:
        for l in range(CHUNK):
            for s in range(hrows_ref.shape[0]):
                h_ref[l, :, LANES * s:LANES * (s + 1)] = (
                    hrows_ref[s, pl.ds(l, tm // CHUNK, stride=CHUNK_PITCH), :].astype(BF16))


def _const_spec(shape):
    nd = len(shape)
    return pl.BlockSpec(shape, lambda *_: (0,) * nd, pipeline_mode=pl.Buffered(1))


def _ffn_dims(ffn_w_in):
    d_ff = ffn_w_in.shape[-1] // 2
    return d_ff, d_ff + -d_ff % FF_CHUNK, d_ff // FF_CHUNK * FF_CHUNK


def _cast_io(ffn_w_in, ffn_w_out, layer, which, extras, steps):
    d = ffn_w_in.shape[2]
    d_ff, ffp, k_main = _ffn_dims(ffn_w_in)
    n = 1
    while n < CAST_BLOCKS and steps % (2 * n) == 0:
        n *= 2
    assert d % (n * 2 * SUBLANES) == 0 and k_main % (n * 2 * SUBLANES) == 0
    blk = lambda i: i * n // steps
    in_specs = [pl.BlockSpec((None, None, d // n, 2 * d_ff), lambda i: (layer, which, blk(i), 0)),
                pl.BlockSpec((None, None, k_main // n, d), lambda i: (layer, which, blk(i), 0))]
    out_specs = [pl.BlockSpec((d // n, ffp), lambda i: (blk(i), 0)),
                 pl.BlockSpec((d // n, ffp), lambda i: (blk(i), 0)),
                 pl.BlockSpec((k_main // n, d), lambda i: (blk(i), 0))]
    out_shape = [jax.ShapeDtypeStruct((d, ffp), BF16), jax.ShapeDtypeStruct((d, ffp), BF16),
                 jax.ShapeDtypeStruct((k_main, d), BF16)]
    for m in extras:
        rows, cols = m.shape
        assert rows % (n * 2 * SUBLANES) == 0
        in_specs.append(pl.BlockSpec((rows // n, cols), lambda i: (blk(i), 0)))
        out_specs.append(pl.BlockSpec((rows // n, cols), lambda i: (blk(i), 0)))
        out_shape.append(jax.ShapeDtypeStruct(m.shape, BF16))
    args = [ffn_w_in.astype(F32), ffn_w_out.astype(F32)] + [m.astype(F32) for m in extras]
    return args, in_specs, out_specs, out_shape


def _w_out_tail(ffn_w_out, layer, which, k_main, ffp):
    tail = lax.optimization_barrier(ffn_w_out[layer, which, k_main:, :]).astype(BF16)
    return jnp.pad(tail, ((0, ffp - k_main - tail.shape[0]), (0, 0)))


def _cast_first_weights(ffn_w_in, ffn_w_out, layer, which):
    steps = CAST_BLOCKS // 2
    args, in_specs, out_specs, out_shape = _cast_io(ffn_w_in, ffn_w_out, layer, which, (), steps)
    return pl.pallas_call(
        functools.partial(_cast_weight_block, 0),
        grid=(steps,),
        in_specs=in_specs,
        out_specs=out_specs,
        out_shape=out_shape,
        compiler_params=pltpu.CompilerParams(
            dimension_semantics=("arbitrary",), vmem_limit_bytes=VMEM_LIMIT),
        name="weight_cast",
    )(*args)


def _tile_plan(seq_len, mixer, emit_h):
    tm = min(1024, seq_len)
    ts = {None: 512 if emit_h else 1024, "pool": 256, "s5": 512}[mixer]
    return tm, max(1, tm // ts)


def _layer_call(x, mod, norm_g, weights, seq_len, *, mixer=None, mixer_args=(),
                emit_h=False, final_g=None, next_ffn=None):
    n_tok, d = x.shape
    tm, n_sub = _tile_plan(seq_len, mixer, emit_h)
    assert seq_len % tm == 0 and tm % (n_sub * CHUNK * SUBLANES) == 0
    tiles_per_seq = seq_len // tm
    wg, wu, wom, wot = weights
    ffp = wg.shape[1]
    row_spec = pl.BlockSpec((tm, d), lambda i: (i, 0))
    chunk_spec = pl.BlockSpec((None, CHUNK, tm // CHUNK, d),
                              lambda i: (i // tiles_per_seq, 0, i % tiles_per_seq, 0))
    args = [x, mod, norm_g.astype(F32)]
    specs = [row_spec,
             pl.BlockSpec((1, N_SUB * N_MOD, d), lambda i: (i // tiles_per_seq, 0, 0)),
             _const_spec((N_SUB, d))]
    scratch = [pltpu.VMEM((n_sub, tm // n_sub, ffp), BF16)]
    if mixer == "pool":
        pool_w, pool_b, pool_scale = mixer_args
        hb = tm // POOL_HALO
        last = n_tok // POOL_HALO - 1
        args += [x, x, pool_w.astype(BF16), pool_b.astype(F32).reshape(1, d),
                 pool_scale.astype(F32).reshape(1, d)]
        specs += [pl.BlockSpec((POOL_HALO, d), lambda i: (jnp.maximum(i * hb - 1, 0), 0)),
                  pl.BlockSpec((POOL_HALO, d), lambda i: (jnp.minimum((i + 1) * hb, last), 0)),
                  _const_spec(pool_w.shape), _const_spec((1, d)), _const_spec((1, d))]
    elif mixer == "s5":
        ys, d_skip, glu_w, glu_b = mixer_args
        args += [ys, d_skip.astype(F32).reshape(1, d), glu_w.astype(BF16),
                 glu_b.astype(F32).reshape(1, 2 * d)]
        specs += [chunk_spec, _const_spec((1, d)), _const_spec((d, 2 * d)), _const_spec((1, 2 * d))]
        scratch.append(pltpu.VMEM((d // LANES, tm // CHUNK * CHUNK_PITCH, LANES), F32))
    args += [wg, wu, wom, wot]
    specs += [_const_spec(w.shape) for w in (wg, wu, wom, wot)]
    if final_g is not None:
        args.append(final_g.astype(F32).reshape(1, d))
        specs.append(_const_spec((1, d)))
    if next_ffn is not None:
        cast_args, cast_in, cast_out, cast_shape = _cast_io(*next_ffn, n_tok // tm)
        args += cast_args
        specs += cast_in
    out_shape = [jax.ShapeDtypeStruct((n_tok, d), F32)]
    out_specs = [row_spec]
    if emit_h:
        out_shape.append(jax.ShapeDtypeStruct((n_tok // seq_len, CHUNK, seq_len // CHUNK, d), BF16))
        out_specs.append(chunk_spec)
        scratch.append(pltpu.VMEM((d // LANES, tm // CHUNK * CHUNK_PITCH, LANES), F32))
    if next_ffn is not None:
        out_shape += cast_shape
        out_specs += cast_out
    outs = pl.pallas_call(
        functools.partial(_layer_kernel, mixer, emit_h, final_g is not None,
                          None if next_ffn is None else len(next_ffn[4]), seq_len, n_sub),
        grid=(n_tok // tm,),
        in_specs=specs,
        out_specs=out_specs,
        out_shape=out_shape,
        scratch_shapes=scratch,
        compiler_params=pltpu.CompilerParams(
            dimension_semantics=("arbitrary",), vmem_limit_bytes=VMEM_LIMIT),
        name="layer_" + (mixer or "ffn"),
    )(*args)
    n_act = 2 if emit_h else 1
    acts = tuple(outs[:n_act]) if emit_h else outs[0]
    if next_ffn is None:
        return acts
    _, ffp_next, k_main = _ffn_dims(next_ffn[0])
    tail = _w_out_tail(next_ffn[1], next_ffn[2], next_ffn[3], k_main, ffp_next)
    return acts, (*outs[n_act:n_act + 3], tail), tuple(outs[n_act + 3:])


def _cmul(ar, ai, br, bi):
    return ar * br - ai * bi, ar * bi + ai * br


def _s5_prep_kernel(n_levels, lr_ref, li_ref, ldt_ref, bre_ref, bim_ref, cre_ref, cim_ref,
                    t_ref, bz_ref, cx_ref, dec_ref):
    gb = lr_ref.shape[1]
    w = t_ref.shape[1]
    chunk = w // SSM_H
    n_dbl = chunk.bit_length() - 1
    assert chunk == 1 << n_dbl
    lane = lax.broadcasted_iota(jnp.int32, (1, LANES), 1)
    re_half = lane < SSM_P
    wlane = lax.broadcasted_iota(jnp.int32, (1, w), 1)
    step_e = (1 << lax.broadcasted_iota(jnp.int32, (SUBLANES, 1), 0)).astype(F32)
    level_e = (chunk << lax.broadcasted_iota(jnp.int32, (n_levels, 1), 0)).astype(F32)
    assert n_dbl < SUBLANES

    def flip(v):
        return jnp.concatenate([v[SSM_H * l:SSM_H * (l + 1)] for l in reversed(range(chunk))], axis=0)

    for g in range(gb):
        rs, zs, cs = [], [], []
        for direction in range(2):
            lr, li = lr_ref[direction, g], li_ref[direction, g]
            dt = jnp.exp(ldt_ref[direction, g])
            lrdt, lidt = lr * dt, li * dt

            def a_pow(e):
                mag = jnp.exp(lrdt * e)
                return mag * jnp.cos(lidt * e), mag * jnp.sin(lidt * e)

            step_r, step_i = a_pow(step_e)
            ar, ai = step_r[0:1], step_i[0:1]
            den = lr * lr + li * li
            fr = ((ar - 1.0) * lr + ai * li) / den
            fi = (ai * lr - (ar - 1.0) * li) / den
            bre, bim = bre_ref[direction, g], bim_ref[direction, g]
            bbr, bbi = _cmul(fr, fi, bre, bim)
            cre16, cim16 = cre_ref[direction, g], cim_ref[direction, g]

            b_r, b_i, c_r, c_i = bbr, bbi, cre16, cim16
            for k in range(n_dbl):
                sr, si = step_r[k:k + 1], step_i[k:k + 1]
                nr, ni = _cmul(b_r, b_i, sr, si)
                b_r, b_i = jnp.concatenate([b_r, nr], axis=0), jnp.concatenate([b_i, ni], axis=0)
                nr, ni = _cmul(c_r, c_i, sr, si)
                c_r, c_i = jnp.concatenate([c_r, nr], axis=0), jnp.concatenate([c_i, ni], axis=0)
            top_r, top_i = _cmul(cre16, cim16, step_r[n_dbl:n_dbl + 1], step_i[n_dbl:n_dbl + 1])
            c1_r = jnp.concatenate([c_r[SSM_H:], top_r], axis=0)
            c1_i = jnp.concatenate([c_i[SSM_H:], top_i], axis=0)

            def mine(v):
                return jnp.where(re_half if g % 2 == 0 else ~re_half, v, 0.0)

            if direction == 0:
                b_r, b_i = flip(b_r), flip(b_i)
            else:
                c1_r, c1_i, c_r, c_i = flip(c1_r), flip(c1_i), flip(c_r), flip(c_i)
            zs += [mine(b_r), mine(b_i)]
            cs += [mine(c1_r).T, mine(-c1_i).T]
            gmat = jnp.where(re_half, c_r, -c_i).T
            bbt = jnp.where(re_half, bbr, bbi)
            rs.append(_dot3(bbt, gmat))
            er, ei = a_pow(level_e)
            base = 2 * n_levels * direction
            dec_ref[g, base:base + n_levels, :] = er
            dec_ref[g, base + n_levels:base + 2 * n_levels, :] = ei
        for i in range(chunk):
            s = SSM_H * i
            fwd_rows = jnp.where(wlane >= s, pltpu.roll(rs[0], s, 1) if s else rs[0], 0.0)
            e = (s + SSM_H) % w
            bwd_rows = jnp.where(wlane < s + SSM_H, pltpu.roll(rs[1], e, 1) if e else rs[1], 0.0)
            t_ref[g, s:s + SSM_H, :] = (fwd_rows + bwd_rows).astype(BF16)
        bz_ref[g] = jnp.concatenate(zs, axis=1).astype(BF16)
        cx_ref[g] = jnp.concatenate(cs, axis=0).astype(BF16)


def _s5_prep(lam_re, lam_im, log_dt, b_re, b_im, c_re, c_im, n_levels):
    n_dir, n_grp, n_state = lam_re.shape
    assert n_dir == 2 and n_state == SSM_P and b_re.shape[-1] == SSM_H
    w = CHUNK * SSM_H
    gb = GROUPS_PER_SLAB
    assert n_grp % gb == 0 and n_levels % SUBLANES == 0

    def twice(v):
        return jnp.concatenate([v, v], axis=-1).astype(F32)

    lr = twice(lam_re)[:, :, None, :]
    li = twice(lam_im)[:, :, None, :]
    ldt = jnp.broadcast_to(log_dt.astype(F32)[:, :, None, None], (n_dir, n_grp, 1, LANES))
    bre, bim = twice(jnp.swapaxes(b_re, 2, 3)), twice(jnp.swapaxes(b_im, 2, 3))
    cre, cim = twice(c_re), twice(c_im)
    n_dec = 2 * 2 * n_levels
    row = lambda r: pl.BlockSpec((n_dir, gb, r, LANES), lambda i: (0, i, 0, 0))
    return pl.pallas_call(
        functools.partial(_s5_prep_kernel, n_levels),
        grid=(n_grp // gb,),
        in_specs=[row(1), row(1), row(1), row(SSM_H), row(SSM_H), row(SSM_H), row(SSM_H)],
        out_specs=[pl.BlockSpec((gb, w, w), lambda i: (i, 0, 0)),
                   pl.BlockSpec((gb, w, 4 * LANES), lambda i: (i, 0, 0)),
                   pl.BlockSpec((gb, 4 * LANES, w), lambda i: (i, 0, 0)),
                   pl.BlockSpec((gb, n_dec, LANES), lambda i: (i, 0, 0))],
        out_shape=[jax.ShapeDtypeStruct((n_grp, w, w), BF16),
                   jax.ShapeDtypeStruct((n_grp, w, 4 * LANES), BF16),
                   jax.ShapeDtypeStruct((n_grp, 4 * LANES, w), BF16),
                   jax.ShapeDtypeStruct((n_grp, n_dec, LANES), F32)],
        compiler_params=pltpu.CompilerParams(
            dimension_semantics=("arbitrary",), vmem_limit_bytes=VMEM_LIMIT),
        name="s5_prep",
    )(lr, li, ldt, bre, bim, cre, cim)


def _chunk_scan(zr, zi, a_re, a_im, reverse):
    n_rows = zr.shape[0]
    assert n_rows % SUBLANES == 0
    row = lax.broadcasted_iota(jnp.int32, (SUBLANES, 1), 0)
    edge = SUBLANES - 1 if reverse else 0

    def shifted(v, dist):
        if reverse:
            return jnp.where(row < SUBLANES - dist, pltpu.roll(v, SUBLANES - dist, 0), 0.0)
        return jnp.where(row >= dist, pltpu.roll(v, dist, 0), 0.0)

    def local_scan(vr, vi):
        for k in range(SUBLANES.bit_length() - 1):
            hr, hi = shifted(vr, 1 << k), shifted(vi, 1 << k)
            vr, vi = vr + (hr * a_re[k] - hi * a_im[k]), vi + (hr * a_im[k] + hi * a_re[k])
        return vr, vi

    def spread(v, r):
        return jnp.broadcast_to(v[r:r + 1, :], v.shape)

    unit = jnp.where(row == edge, 1.0, 0.0) + jnp.zeros((SUBLANES, zr.shape[1]), F32)
    pr, pi = local_scan(unit, jnp.zeros_like(unit))
    pr, pi = pr * a_re[0] - pi * a_im[0], pr * a_im[0] + pi * a_re[0]
    n_tiles = n_rows // SUBLANES
    order = range(n_tiles - 1, -1, -1) if reverse else range(n_tiles)
    cr = ci = jnp.zeros_like(unit)
    out_r, out_i = [None] * n_tiles, [None] * n_tiles
    for n, v in enumerate(order):
        rows = slice(SUBLANES * v, SUBLANES * (v + 1))
        sr, si = local_scan(zr[rows], zi[rows])
        sr, si = sr + (pr * cr - pi * ci), si + (pr * ci + pi * cr)
        out_r[v] = jnp.where(row == edge, cr, shifted(sr, 1))
        out_i[v] = jnp.where(row == edge, ci, shifted(si, 1))
        cr, ci = spread(sr, SUBLANES - 1 - edge), spread(si, SUBLANES - 1 - edge)
        if n % SCAN_YIELD_TILES == SCAN_YIELD_TILES - 1:
            yield
    return jnp.concatenate(out_r, axis=0), jnp.concatenate(out_i, axis=0)


def _slab_permutation():
    n = GROUPS_PER_SLAB
    idx = np.arange(n * LANES)
    a, b, h = idx // LANES, (idx % LANES) // SSM_H, idx % SSM_H
    perm = np.zeros((n * LANES, n * LANES), np.float32)
    perm[idx, b * LANES + a * SSM_H + h] = 1.0
    return jnp.asarray(perm, BF16)


def _unit_transpose(vs):
    vs = list(vs)
    n = len(vs)
    assert n * SSM_H == LANES
    unit = lax.broadcasted_iota(jnp.int32, (1, LANES), 1) // SSM_H
    d = n // 2
    while d:
        upper = (unit & d) != 0
        for i in range(n):
            if i & d:
                continue
            a, b = vs[i], vs[i + d]
            vs[i] = jnp.where(upper, pltpu.roll(b, d * SSM_H, 1), a)
            vs[i + d] = jnp.where(upper, b, pltpu.roll(a, LANES - d * SSM_H, 1))
        d //= 2
    return vs


def _s5_core_kernel(n_levels, h_ref, perm_ref, t_ref, bz_ref, cx_ref, dec_ref, y_ref, u_ref, yg_ref):
    n_grp = GROUPS_PER_SLAB
    first_half = lax.broadcasted_iota(jnp.int32, (1, LANES), 1) < SSM_P
    pairs = [(g0, g0 + 1) for g0 in range(0, n_grp, 2)]

    def gather_rows(p, o):
        v = jnp.concatenate([h_ref[n_grp * o + l8] for l8 in range(n_grp)], axis=1)
        uv = _dot(v, perm_ref[:, 2 * LANES * p:2 * LANES * (p + 1)]).astype(BF16)
        for k, g in enumerate(pairs[p]):
            u_ref[g, :, LANES * o:LANES * (o + 1)] = uv[:, LANES * k:LANES * (k + 1)]

    def pair_scan(p):
        g0, g1 = pairs[p]
        z = _dot(u_ref[g0], bz_ref[g0]) + _dot(u_ref[g1], bz_ref[g1])
        xs = []
        for direction in range(2):
            base = 2 * n_levels * direction
            a_re, a_im = ([jnp.where(first_half, dec_ref[g0, r:r + 1, :], dec_ref[g1, r:r + 1, :])
                           for r in range(lo, lo + n_levels)] for lo in (base, base + n_levels))
            cols = 2 * LANES * direction
            xs += yield from _chunk_scan(z[:, cols:cols + LANES], z[:, cols + LANES:cols + 2 * LANES],
                                         a_re, a_im, direction == 1)
        return jnp.concatenate(xs, axis=1).astype(BF16)

    lag = {}
    matmuls = []

    def lag_product(g):
        lag[g] = _dot(u_ref[g], t_ref[g])

    def state_product(g, x_all):
        yg_ref[g] = lag[g] + _dot(x_all, cx_ref[g])

    n_oct = CHUNK // n_grp
    for o in range(n_oct):
        gather_rows(0, o)
    for p in range(len(pairs)):
        if p + 1 < len(pairs):
            matmuls += [functools.partial(gather_rows, p + 1, o) for o in range(n_oct)]
        matmuls += [functools.partial(lag_product, g) for g in pairs[p]]
        scan = pair_scan(p)
        while True:
            try:
                next(scan)
            except StopIteration as done:
                x_all = done.value
                break
            if matmuls:
                matmuls.pop(0)()
        while matmuls:
            matmuls.pop(0)()
        matmuls += [functools.partial(state_product, g, x_all) for g in pairs[p]]
    while matmuls:
        matmuls.pop(0)()
    for o in range(CHUNK // n_grp):
        ys = _unit_transpose([yg_ref[g, :, LANES * o:LANES * (o + 1)] for g in range(n_grp)])
        for l8 in range(n_grp):
            y_ref[n_grp * o + l8] = ys[l8].astype(BF16)


def _s5_mix(h, lam_re, lam_im, log_dt, b_re, b_im, c_re, c_im):
    bsz, _, rows, d = h.shape
    w = CHUNK * SSM_H
    gb = GROUPS_PER_SLAB
    assert h.shape[1] == CHUNK and CHUNK % gb == 0 and d % LANES == 0
    n_levels = -(-max(1, (rows - 1).bit_length()) // SUBLANES) * SUBLANES
    t_mat, bz, cx, dec = _s5_prep(lam_re, lam_im, log_dt, b_re, b_im, c_re, c_im, n_levels)
    perm = _slab_permutation()
    slab = pl.BlockSpec((None, CHUNK, rows, LANES), lambda s, b: (b, 0, 0, s))
    grp = lambda r, c: pl.BlockSpec((gb, r, c), lambda s, b: (s, 0, 0))
    return pl.pallas_call(
        functools.partial(_s5_core_kernel, n_levels),
        grid=(d // LANES, bsz),
        in_specs=[slab, _const_spec(perm.shape), grp(w, w), grp(w, 4 * LANES), grp(4 * LANES, w),
                  grp(dec.shape[1], LANES)],
        out_specs=slab,
        out_shape=jax.ShapeDtypeStruct(h.shape, BF16),
        scratch_shapes=[pltpu.VMEM((gb, rows, w), BF16), pltpu.VMEM((gb, rows, w), F32)],
        compiler_params=pltpu.CompilerParams(
            dimension_semantics=("arbitrary", "arbitrary"), vmem_limit_bytes=VMEM_LIMIT),
        name="s5_core",
    )(h, perm, t_mat, bz, cx, dec)


def kernel(x, c, mod_w, mod_b, norm_g, ffn_w_in, ffn_w_out, pool_w, pool_b, pool_scale,
           ssm_lam_re, ssm_lam_im, ssm_log_dt, ssm_b_re, ssm_b_im, ssm_c_re, ssm_c_im,
           ssm_d, glu_w, glu_b, final_g):
    bsz, seq_len, d = x.shape
    depth = mod_w.shape[0]
    n_mixers = 2
    mods = _modulation(c, mod_w, mod_b)
    xf = x.astype(F32).reshape(bsz * seq_len, d)
    _, ffp, k_main = _ffn_dims(ffn_w_in)
    weights = (*_cast_first_weights(ffn_w_in, ffn_w_out, 0, 0), _w_out_tail(ffn_w_out, 0, 0, k_main, ffp))
    for i in range(depth):
        j = i // n_mixers
        is_s5 = i % n_mixers == 1
        last = i == depth - 1
        extras = (glu_w[j],) if is_s5 else ()
        acts, weights, extras = _layer_call(xf, mods[i], norm_g[i], weights, seq_len, emit_h=is_s5,
                                            next_ffn=(ffn_w_in, ffn_w_out, i, 1, extras))
        if is_s5:
            xf, h = acts
            ys = _s5_mix(h, ssm_lam_re[j], ssm_lam_im[j], ssm_log_dt[j], ssm_b_re[j],
                         ssm_b_im[j], ssm_c_re[j], ssm_c_im[j])
            mixer, mixer_args = "s5", (ys, ssm_d[j], extras[0], glu_b[j])
        else:
            xf = acts
            mixer, mixer_args = "pool", (pool_w[j], pool_b[j], pool_scale[j])
        res = _layer_call(xf, mods[i], norm_g[i], weights, seq_len, mixer=mixer, mixer_args=mixer_args,
                          final_g=final_g if last else None,
                          next_ffn=None if last else (ffn_w_in, ffn_w_out, i + 1, 0, ()))
        xf, weights = (res, None) if last else res[:2]
    return xf.reshape(bsz, seq_len, d).astype(x.dtype)
```

```python
import functools
import math

import numpy as np

import jax
import jax.numpy as jnp
from jax import lax
from jax.experimental import pallas as pl
from jax.experimental.pallas import tpu as pltpu

F32 = jnp.float32
BF16 = jnp.bfloat16

EPS = 1e-6
N_SUB = 3
N_MOD = 3
POOL_WINDOWS = (2, 4, 8, 16)
POOL_HALO = 8
SSM_H = 16
SSM_P = 64
CHUNK = 32
CHUNK_PITCH = CHUNK + 4
FF_CHUNK = 256
GLU_CHUNK = 512
MATMUL_PHASE = "matmul phase"
SEQUENCE_LEAD_STEPS = 12
SCAN_YIELD_TILES = 8
LANES = 128
CAST_BLOCKS = 16
SUBLANES = 8
GROUPS_PER_SLAB = LANES // SSM_H
VMEM_LIMIT = 60 * 1024 * 1024


def _sigmoid(v):
    return 1.0 / (1.0 + jnp.exp(-v))


def _split3(a):
    hi = a.astype(BF16)
    lo = (a - hi.astype(F32)).astype(BF16)
    return hi, lo


def _dot(a, b):
    return jnp.dot(a, b, preferred_element_type=F32)


def _dot3(a, b):
    ah, al = _split3(a)
    bh, bl = _split3(b)
    return _dot(ah, bh) + (_dot(ah, bl) + _dot(al, bh))


def _mod_kernel(c_ref, w_ref, b_ref, o_ref):
    c = c_ref[...]
    cond = c * _sigmoid(c)
    o_ref[0] = _dot3(cond, w_ref[0]) + b_ref[0]


def _modulation(c, mod_w, mod_b):
    depth, d, n = mod_w.shape
    bsz = c.shape[0]
    rows = SUBLANES
    tn = n // 4 if n % (4 * LANES) == 0 else n
    assert n % tn == 0 and bsz <= rows
    c_pad = jnp.zeros((rows, d), F32).at[:bsz].set(c.astype(F32))
    out = pl.pallas_call(
        _mod_kernel,
        grid=(depth, n // tn),
        in_specs=[
            pl.BlockSpec((rows, d), lambda i, j: (0, 0)),
            pl.BlockSpec((1, d, tn), lambda i, j: (i, 0, j)),
            pl.BlockSpec((1, 1, tn), lambda i, j: (i, 0, j)),
        ],
        out_specs=pl.BlockSpec((1, rows, tn), lambda i, j: (i, 0, j)),
        out_shape=jax.ShapeDtypeStruct((depth, rows, n), F32),
        compiler_params=pltpu.CompilerParams(
            dimension_semantics=("arbitrary", "arbitrary"), vmem_limit_bytes=VMEM_LIMIT),
        name="adaln_modulation",
    )(c_pad, mod_w.astype(F32), mod_b.astype(F32).reshape(depth, 1, n))
    return out[:, :bsz].reshape(depth, bsz, N_SUB * N_MOD, d)


def _rms(x, g):
    return x * lax.rsqrt(jnp.mean(x * x, axis=-1, keepdims=True) + EPS) * g


def _modnorm(x, g_ref, mod, sub):
    r = N_MOD * sub
    shift, scale = mod[r:r + 1], mod[r + 1:r + 2]
    return _rms(x, g_ref[sub:sub + 1, :]) * (1.0 + scale) + shift


def _swiglu_step(x, g_ref, mod, sub, wg_ref, wu_ref, wom_ref, wot_ref, act_ref):
    h = _modnorm(x, g_ref, mod, sub).astype(BF16)
    yield MATMUL_PHASE
    n_chunks = act_ref.shape[1] // FF_CHUNK
    for k in range(n_chunks):
        cols = slice(FF_CHUNK * k, FF_CHUNK * (k + 1))
        gate = _dot(h, wg_ref[:, cols])
        up = _dot(h, wu_ref[:, cols])
        act_ref[:, cols] = (gate * _sigmoid(gate) * up).astype(BF16)
        yield
    k_main = wom_ref.shape[0]
    out = _dot(act_ref[:, :k_main], wom_ref[...])
    if act_ref.shape[1] > k_main:
        out = out + _dot(act_ref[:, k_main:], wot_ref[...])
    r = N_MOD * sub + 2
    return x + (0.5 * mod[r:r + 1]) * out


def _gelu_tanh(v):
    return 0.5 * v * (1.0 + jnp.tanh(math.sqrt(2.0 / math.pi) * (v + 0.044715 * (v * v * v))))


def _pool_update(x, row0, x_ref, xp_ref, xn_ref, g_ref, mod, pw_ref, pb_ref, ps_ref, seq_len):
    ts, d = x.shape
    tm = x_ref.shape[0]
    cg = d // len(POOL_WINDOWS)
    tiles_per_seq = seq_len // tm
    j = pl.program_id(0) % tiles_per_seq
    t = j * tm + row0 + lax.broadcasted_iota(jnp.int32, (ts, 1), 0)
    n_ext = ts + 2 * POOL_HALO
    if row0 == 0:
        before = jnp.where(j > 0, _modnorm(xp_ref[...], g_ref, mod, 1), 0.0)
    else:
        before = _modnorm(x_ref[row0 - POOL_HALO:row0, :], g_ref, mod, 1)
    if row0 + ts == tm:
        after = jnp.where(j < tiles_per_seq - 1, _modnorm(xn_ref[...], g_ref, mod, 1), 0.0)
    else:
        after = _modnorm(x_ref[row0 + ts:row0 + ts + POOL_HALO, :], g_ref, mod, 1)
    h_ext = jnp.concatenate([before, _modnorm(x, g_ref, mod, 1), after], axis=0)
    ys = []
    for gi, window in enumerate(POOL_WINDOWS):
        left = window // 2
        right = window - 1 - left
        cols = slice(gi * cg, (gi + 1) * cg)
        assert window & (window - 1) == 0 and right < POOL_HALO
        ext = h_ext[:, cols]
        run = ext
        span = 1
        while span < window:
            run = run + pltpu.roll(run, span, 0)
            span *= 2
        if right:
            run = pltpu.roll(run, n_ext - right, 0)
        total = run[POOL_HALO:POOL_HALO + ts]
        lo = jnp.maximum(t - left, 0)
        hi = jnp.minimum(t + right, seq_len - 1)
        count = (hi - lo + 1).astype(F32)
        p = total / count - ext[POOL_HALO:POOL_HALO + ts]
        ys.append(_dot(p.astype(BF16), pw_ref[gi]))
        yield
    y = (jnp.concatenate(ys, axis=1) + pb_ref[...]) * ps_ref[...]
    return x + mod[N_MOD + 2:N_MOD + 3] * y


def _s5_tail_update(x, ys, g_ref, mod, dskip_ref, gw_ref, gb_ref):
    d = x.shape[1]
    h = _modnorm(x, g_ref, mod, 1)
    y = _gelu_tanh(dskip_ref[...] * h + ys).astype(BF16)
    yield
    ms = []
    for k in range(d // GLU_CHUNK):
        cv = slice(GLU_CHUNK * k, GLU_CHUNK * (k + 1))
        cg = slice(d + GLU_CHUNK * k, d + GLU_CHUNK * (k + 1))
        val = _dot(y, gw_ref[:, cv]) + gb_ref[:, cv]
        gate = _dot(y, gw_ref[:, cg]) + gb_ref[:, cg]
        ms.append(val * _sigmoid(gate))
        yield
    return x + mod[N_MOD + 2:N_MOD + 3] * jnp.concatenate(ms, axis=1)


def _cast_weight_block(n_extra, win_ref, wout_ref, *refs):
    wg_ref, wu_ref, wom_ref = refs[n_extra:n_extra + 3]
    for src, dst in zip(refs[:n_extra], refs[n_extra + 3:]):
        dst[...] = src[...].astype(BF16)
    d_ff = win_ref.shape[1] // 2
    w = win_ref[...]
    for src, dst in ((w[:, :d_ff], wg_ref), (w[:, d_ff:], wu_ref)):
        dst[:, :d_ff] = src.astype(BF16)
        if dst.shape[1] > d_ff:
            dst[:, d_ff:] = jnp.zeros((dst.shape[0], dst.shape[1] - d_ff), BF16)
    wom_ref[...] = wout_ref[...].astype(BF16)


def _layer_kernel(mixer, emit_h, final_norm, cast_next, seq_len, n_sub, *refs):
    refs = list(refs)
    x_ref, mod_ref, g_ref = refs[:3]
    del refs[:3]
    mod = mod_ref[0]
    if mixer == "pool":
        xp_ref, xn_ref, pw_ref, pb_ref, ps_ref = refs[:5]
        del refs[:5]
    elif mixer == "s5":
        ys_ref, dskip_ref, gw_ref, gb_ref = refs[:4]
        del refs[:4]
    wg_ref, wu_ref, wom_ref, wot_ref = refs[:4]
    del refs[:4]
    if final_norm:
        fg_ref = refs.pop(0)
    if cast_next is not None:
        cast_in = refs[:2 + cast_next]
        del refs[:2 + cast_next]
    o_ref = refs.pop(0)
    if emit_h:
        h_ref = refs.pop(0)
    if cast_next is not None:
        _cast_weight_block(cast_next, *cast_in, *refs[:3 + cast_next])
        del refs[:3 + cast_next]
    act_ref = refs.pop(0)
    tm = x_ref.shape[0]
    ts = tm // n_sub
    chunks = ts // CHUNK
    if mixer == "s5":
        rows_ref = refs.pop(0)
        n_slab = rows_ref.shape[0]
        for l in range(CHUNK):
            for s in range(n_slab):
                rows_ref[s, pl.ds(l, tm // CHUNK, stride=CHUNK_PITCH), :] = (
                    ys_ref[l, :, LANES * s:LANES * (s + 1)].astype(F32))
    if emit_h:
        hrows_ref = refs.pop(0)

    def sub_tile(si):
        row0 = si * ts
        x = x_ref[row0:row0 + ts, :]
        if mixer == "pool":
            x = yield from _pool_update(x, row0, x_ref, xp_ref, xn_ref, g_ref, mod, pw_ref, pb_ref, ps_ref,
                                        seq_len)
        elif mixer == "s5":
            ys = jnp.concatenate(
                [jnp.concatenate([rows_ref[s, CHUNK_PITCH * c:CHUNK_PITCH * c + CHUNK, :]
                                  for c in range(si * chunks, (si + 1) * chunks)], axis=0)
                 for s in range(n_slab)], axis=1)
            x = yield from _s5_tail_update(x, ys, g_ref, mod, dskip_ref, gw_ref, gb_ref)
        sub = 0 if mixer is None else 2
        x = yield from _swiglu_step(x, g_ref, mod, sub, wg_ref, wu_ref, wom_ref, wot_ref, act_ref.at[si])
        if final_norm:
            x = _rms(x, fg_ref[...])
        o_ref[row0:row0 + ts, :] = x
        if emit_h:
            h = _modnorm(x, g_ref, mod, 1)
            for c in range(chunks):
                base = CHUNK_PITCH * (si * chunks + c)
                for s in range(hrows_ref.shape[0]):
                    hrows_ref[s, base:base + CHUNK, :] = h[CHUNK * c:CHUNK * (c + 1), LANES * s:LANES * (s + 1)]

    tiles = [sub_tile(si) for si in range(n_sub)]
    pending = [True] * n_sub
    for marker in tiles[0]:
        if marker == MATMUL_PHASE:
            pending[0] = False
            break
    for si, tile in enumerate(tiles):
        nxt = si + 1 if si + 1 < n_sub else None
        for _ in tile:
            if nxt is not None and pending[nxt]:
                pending[nxt] = next(tiles[nxt]) != MATMUL_PHASE
        while nxt is not None and pending[nxt]:
            pending[nxt] = next(tiles[nxt]) != MATMUL_PHASE
    if emit_h:
        for l in range(CHUNK):
            for s in range(hrows_ref.shape[0]):
                h_ref[l, :, LANES * s:LANES * (s + 1)] = (
                    hrows_ref[s, pl.ds(l, tm // CHUNK, stride=CHUNK_PITCH), :].astype(BF16))


def _const_spec(shape):
    nd = len(shape)
    return pl.BlockSpec(shape, lambda *_: (0,) * nd, pipeline_mode=pl.Buffered(1))


def _ffn_dims(ffn_w_in):
    d_ff = ffn_w_in.shape[-1] // 2
    return d_ff, d_ff + -d_ff % FF_CHUNK, d_ff // FF_CHUNK * FF_CHUNK


def _cast_io(ffn_w_in, ffn_w_out, layer, which, extras, steps):
    d = ffn_w_in.shape[2]
    d_ff, ffp, k_main = _ffn_dims(ffn_w_in)
    n = 1
    while n < CAST_BLOCKS and steps % (2 * n) == 0:
        n *= 2
    assert d % (n * 2 * SUBLANES) == 0 and k_main % (n * 2 * SUBLANES) == 0
    blk = lambda i: i * n // steps
    in_specs = [pl.BlockSpec((None, None, d // n, 2 * d_ff), lambda i: (layer, which, blk(i), 0)),
                pl.BlockSpec((None, None, k_main // n, d), lambda i: (layer, which, blk(i), 0))]
    out_specs = [pl.BlockSpec((d // n, ffp), lambda i: (blk(i), 0)),
                 pl.BlockSpec((d // n, ffp), lambda i: (blk(i), 0)),
                 pl.BlockSpec((k_main // n, d), lambda i: (blk(i), 0))]
    out_shape = [jax.ShapeDtypeStruct((d, ffp), BF16), jax.ShapeDtypeStruct((d, ffp), BF16),
                 jax.ShapeDtypeStruct((k_main, d), BF16)]
    for m in extras:
        rows, cols = m.shape
        assert rows % (n * 2 * SUBLANES) == 0
        in_specs.append(pl.BlockSpec((rows // n, cols), lambda i: (blk(i), 0)))
        out_specs.append(pl.BlockSpec((rows // n, cols), lambda i: (blk(i), 0)))
        out_shape.append(jax.ShapeDtypeStruct(m.shape, BF16))
    args = [ffn_w_in.astype(F32), ffn_w_out.astype(F32)] + [m.astype(F32) for m in extras]
    return args, in_specs, out_specs, out_shape


def _w_out_tail(ffn_w_out, layer, which, k_main, ffp):
    tail = lax.optimization_barrier(ffn_w_out[layer, which, k_main:, :]).astype(BF16)
    return jnp.pad(tail, ((0, ffp - k_main - tail.shape[0]), (0, 0)))


def _cast_first_weights(ffn_w_in, ffn_w_out, layer, which):
    steps = CAST_BLOCKS // 2
    args, in_specs, out_specs, out_shape = _cast_io(ffn_w_in, ffn_w_out, layer, which, (), steps)
    return pl.pallas_call(
        functools.partial(_cast_weight_block, 0),
        grid=(steps,),
        in_specs=in_specs,
        out_specs=out_specs,
        out_shape=out_shape,
        compiler_params=pltpu.CompilerParams(
            dimension_semantics=("arbitrary",), vmem_limit_bytes=VMEM_LIMIT),
        name="weight_cast",
    )(*args)


def _tile_plan(seq_len, mixer, emit_h):
    tm = min(1024, seq_len)
    ts = {None: 512 if emit_h else 1024, "pool": 256, "s5": 512}[mixer]
    return tm, max(1, tm // ts)


def _layer_call(x, mod, norm_g, weights, seq_len, *, mixer=None, mixer_args=(),
                emit_h=False, final_g=None, next_ffn=None):
    n_tok, d = x.shape
    tm, n_sub = _tile_plan(seq_len, mixer, emit_h)
    assert seq_len % tm == 0 and tm % (n_sub * CHUNK * SUBLANES) == 0
    tiles_per_seq = seq_len // tm
    wg, wu, wom, wot = weights
    ffp = wg.shape[1]
    row_spec = pl.BlockSpec((tm, d), lambda i: (i, 0))
    chunk_spec = pl.BlockSpec((None, CHUNK, tm // CHUNK, d),
                              lambda i: (i // tiles_per_seq, 0, i % tiles_per_seq, 0))
    args = [x, mod, norm_g.astype(F32)]
    specs = [row_spec,
             pl.BlockSpec((1, N_SUB * N_MOD, d), lambda i: (i // tiles_per_seq, 0, 0)),
             _const_spec((N_SUB, d))]
    scratch = [pltpu.VMEM((n_sub, tm // n_sub, ffp), BF16)]
    if mixer == "pool":
        pool_w, pool_b, pool_scale = mixer_args
        hb = tm // POOL_HALO
        last = n_tok // POOL_HALO - 1
        args += [x, x, pool_w.astype(BF16), pool_b.astype(F32).reshape(1, d),
                 pool_scale.astype(F32).reshape(1, d)]
        specs += [pl.BlockSpec((POOL_HALO, d), lambda i: (jnp.maximum(i * hb - 1, 0), 0)),
                  pl.BlockSpec((POOL_HALO, d), lambda i: (jnp.minimum((i + 1) * hb, last), 0)),
                  _const_spec(pool_w.shape), _const_spec((1, d)), _const_spec((1, d))]
    elif mixer == "s5":
        ys, d_skip, glu_w, glu_b = mixer_args
        args += [ys, d_skip.astype(F32).reshape(1, d), glu_w.astype(BF16),
                 glu_b.astype(F32).reshape(1, 2 * d)]
        specs += [chunk_spec, _const_spec((1, d)), _const_spec((d, 2 * d)), _const_spec((1, 2 * d))]
        scratch.append(pltpu.VMEM((d // LANES, tm // CHUNK * CHUNK_PITCH, LANES), F32))
    args += [wg, wu, wom, wot]
    specs += [_const_spec(w.shape) for w in (wg, wu, wom, wot)]
    if final_g is not None:
        args.append(final_g.astype(F32).reshape(1, d))
        specs.append(_const_spec((1, d)))
    if next_ffn is not None:
        cast_args, cast_in, cast_out, cast_shape = _cast_io(*next_ffn, n_tok // tm)
        args += cast_args
        specs += cast_in
    out_shape = [jax.ShapeDtypeStruct((n_tok, d), F32)]
    out_specs = [row_spec]
    if emit_h:
        out_shape.append(jax.ShapeDtypeStruct((n_tok // seq_len, CHUNK, seq_len // CHUNK, d), BF16))
        out_specs.append(chunk_spec)
        scratch.append(pltpu.VMEM((d // LANES, tm // CHUNK * CHUNK_PITCH, LANES), F32))
    if next_ffn is not None:
        out_shape += cast_shape
        out_specs += cast_out
    outs = pl.pallas_call(
        functools.partial(_layer_kernel, mixer, emit_h, final_g is not None,
                          None if next_ffn is None else len(next_ffn[4]), seq_len, n_sub),
        grid=(n_tok // tm,),
        in_specs=specs,
        out_specs=out_specs,
        out_shape=out_shape,
        scratch_shapes=scratch,
        compiler_params=pltpu.CompilerParams(
            dimension_semantics=("arbitrary",), vmem_limit_bytes=VMEM_LIMIT),
        name="layer_" + (mixer or "ffn"),
    )(*args)
    n_act = 2 if emit_h else 1
    acts = tuple(outs[:n_act]) if emit_h else outs[0]
    if next_ffn is None:
        return acts
    _, ffp_next, k_main = _ffn_dims(next_ffn[0])
    tail = _w_out_tail(next_ffn[1], next_ffn[2], next_ffn[3], k_main, ffp_next)
    return acts, (*outs[n_act:n_act + 3], tail), tuple(outs[n_act + 3:])


def _cmul(ar, ai, br, bi):
    return ar * br - ai * bi, ar * bi + ai * br


def _s5_prep_kernel(n_levels, lr_ref, li_ref, ldt_ref, bre_ref, bim_ref, cre_ref, cim_ref,
                    t_ref, bz_ref, cx_ref, dec_ref):
    gb = lr_ref.shape[1]
    w = t_ref.shape[1]
    chunk = w // SSM_H
    n_dbl = chunk.bit_length() - 1
    assert chunk == 1 << n_dbl
    lane = lax.broadcasted_iota(jnp.int32, (1, LANES), 1)
    re_half = lane < SSM_P
    wlane = lax.broadcasted_iota(jnp.int32, (1, w), 1)
    step_e = (1 << lax.broadcasted_iota(jnp.int32, (SUBLANES, 1), 0)).astype(F32)
    level_e = (chunk << lax.broadcasted_iota(jnp.int32, (n_levels, 1), 0)).astype(F32)
    assert n_dbl < SUBLANES

    def flip(v):
        return jnp.concatenate([v[SSM_H * l:SSM_H * (l + 1)] for l in reversed(range(chunk))], axis=0)

    for g in range(gb):
        rs, zs, cs = [], [], []
        for direction in range(2):
            lr, li = lr_ref[direction, g], li_ref[direction, g]
            dt = jnp.exp(ldt_ref[direction, g])
            lrdt, lidt = lr * dt, li * dt

            def a_pow(e):
                mag = jnp.exp(lrdt * e)
                return mag * jnp.cos(lidt * e), mag * jnp.sin(lidt * e)

            step_r, step_i = a_pow(step_e)
            ar, ai = step_r[0:1], step_i[0:1]
            den = lr * lr + li * li
            fr = ((ar - 1.0) * lr + ai * li) / den
            fi = (ai * lr - (ar - 1.0) * li) / den
            bre, bim = bre_ref[direction, g], bim_ref[direction, g]
            bbr, bbi = _cmul(fr, fi, bre, bim)
            cre16, cim16 = cre_ref[direction, g], cim_ref[direction, g]

            b_r, b_i, c_r, c_i = bbr, bbi, cre16, cim16
            for k in range(n_dbl):
                sr, si = step_r[k:k + 1], step_i[k:k + 1]
                nr, ni = _cmul(b_r, b_i, sr, si)
                b_r, b_i = jnp.concatenate([b_r, nr], axis=0), jnp.concatenate([b_i, ni], axis=0)
                nr, ni = _cmul(c_r, c_i, sr, si)
                c_r, c_i = jnp.concatenate([c_r, nr], axis=0), jnp.concatenate([c_i, ni], axis=0)
            top_r, top_i = _cmul(cre16, cim16, step_r[n_dbl:n_dbl + 1], step_i[n_dbl:n_dbl + 1])
            c1_r = jnp.concatenate([c_r[SSM_H:], top_r], axis=0)
            c1_i = jnp.concatenate([c_i[SSM_H:], top_i], axis=0)

            def mine(v):
                return jnp.where(re_half if g % 2 == 0 else ~re_half, v, 0.0)

            if direction == 0:
                b_r, b_i = flip(b_r), flip(b_i)
            else:
                c1_r, c1_i, c_r, c_i = flip(c1_r), flip(c1_i), flip(c_r), flip(c_i)
            zs += [mine(b_r), mine(b_i)]
            cs += [mine(c1_r).T, mine(-c1_i).T]
            gmat = jnp.where(re_half, c_r, -c_i).T
            bbt = jnp.where(re_half, bbr, bbi)
            rs.append(_dot3(bbt, gmat))
            er, ei = a_pow(level_e)
            base = 2 * n_levels * direction
            dec_ref[g, base:base + n_levels, :] = er
            dec_ref[g, base + n_levels:base + 2 * n_levels, :] = ei
        for i in range(chunk):
            s = SSM_H * i
            fwd_rows = jnp.where(wlane >= s, pltpu.roll(rs[0], s, 1) if s else rs[0], 0.0)
            e = (s + SSM_H) % w
            bwd_rows = jnp.where(wlane < s + SSM_H, pltpu.roll(rs[1], e, 1) if e else rs[1], 0.0)
            t_ref[g, s:s + SSM_H, :] = (fwd_rows + bwd_rows).astype(BF16)
        bz_ref[g] = jnp.concatenate(zs, axis=1).astype(BF16)
        cx_ref[g] = jnp.concatenate(cs, axis=0).astype(BF16)


def _s5_prep(lam_re, lam_im, log_dt, b_re, b_im, c_re, c_im, n_levels):
    n_dir, n_grp, n_state = lam_re.shape
    assert n_dir == 2 and n_state == SSM_P and b_re.shape[-1] == SSM_H
    w = CHUNK * SSM_H
    gb = GROUPS_PER_SLAB
    assert n_grp % gb == 0 and n_levels % SUBLANES == 0

    def twice(v):
        return jnp.concatenate([v, v], axis=-1).astype(F32)

    lr = twice(lam_re)[:, :, None, :]
    li = twice(lam_im)[:, :, None, :]
    ldt = jnp.broadcast_to(log_dt.astype(F32)[:, :, None, None], (n_dir, n_grp, 1, LANES))
    bre, bim = twice(jnp.swapaxes(b_re, 2, 3)), twice(jnp.swapaxes(b_im, 2, 3))
    cre, cim = twice(c_re), twice(c_im)
    n_dec = 2 * 2 * n_levels
    row = lambda r: pl.BlockSpec((n_dir, gb, r, LANES), lambda i: (0, i, 0, 0))
    return pl.pallas_call(
        functools.partial(_s5_prep_kernel, n_levels),
        grid=(n_grp // gb,),
        in_specs=[row(1), row(1), row(1), row(SSM_H), row(SSM_H), row(SSM_H), row(SSM_H)],
        out_specs=[pl.BlockSpec((gb, w, w), lambda i: (i, 0, 0)),
                   pl.BlockSpec((gb, w, 4 * LANES), lambda i: (i, 0, 0)),
                   pl.BlockSpec((gb, 4 * LANES, w), lambda i: (i, 0, 0)),
                   pl.BlockSpec((gb, n_dec, LANES), lambda i: (i, 0, 0))],
        out_shape=[jax.ShapeDtypeStruct((n_grp, w, w), BF16),
                   jax.ShapeDtypeStruct((n_grp, w, 4 * LANES), BF16),
                   jax.ShapeDtypeStruct((n_grp, 4 * LANES, w), BF16),
                   jax.ShapeDtypeStruct((n_grp, n_dec, LANES), F32)],
        compiler_params=pltpu.CompilerParams(
            dimension_semantics=("arbitrary",), vmem_limit_bytes=VMEM_LIMIT),
        name="s5_prep",
    )(lr, li, ldt, bre, bim, cre, cim)


def _chunk_scan(zr, zi, a_re, a_im, reverse):
    n_rows = zr.shape[0]
    assert n_rows % SUBLANES == 0
    row = lax.broadcasted_iota(jnp.int32, (SUBLANES, 1), 0)
    edge = SUBLANES - 1 if reverse else 0

    def shifted(v, dist):
        if reverse:
            return jnp.where(row < SUBLANES - dist, pltpu.roll(v, SUBLANES - dist, 0), 0.0)
        return jnp.where(row >= dist, pltpu.roll(v, dist, 0), 0.0)

    def local_scan(vr, vi):
        for k in range(SUBLANES.bit_length() - 1):
            hr, hi = shifted(vr, 1 << k), shifted(vi, 1 << k)
            vr, vi = vr + (hr * a_re[k] - hi * a_im[k]), vi + (hr * a_im[k] + hi * a_re[k])
        return vr, vi

    def spread(v, r):
        return jnp.broadcast_to(v[r:r + 1, :], v.shape)

    unit = jnp.where(row == edge, 1.0, 0.0) + jnp.zeros((SUBLANES, zr.shape[1]), F32)
    pr, pi = local_scan(unit, jnp.zeros_like(unit))
    pr, pi = pr * a_re[0] - pi * a_im[0], pr * a_im[0] + pi * a_re[0]
    n_tiles = n_rows // SUBLANES
    order = range(n_tiles - 1, -1, -1) if reverse else range(n_tiles)
    cr = ci = jnp.zeros_like(unit)
    out_r, out_i = [None] * n_tiles, [None] * n_tiles
    for n, v in enumerate(order):
        rows = slice(SUBLANES * v, SUBLANES * (v + 1))
        sr, si = local_scan(zr[rows], zi[rows])
        sr, si = sr + (pr * cr - pi * ci), si + (pr * ci + pi * cr)
        out_r[v] = jnp.where(row == edge, cr, shifted(sr, 1))
        out_i[v] = jnp.where(row == edge, ci, shifted(si, 1))
        cr, ci = spread(sr, SUBLANES - 1 - edge), spread(si, SUBLANES - 1 - edge)
        if n % SCAN_YIELD_TILES == SCAN_YIELD_TILES - 1:
            yield
    return jnp.concatenate(out_r, axis=0), jnp.concatenate(out_i, axis=0)


def _slab_permutation():
    n = GROUPS_PER_SLAB
    idx = np.arange(n * LANES)
    a, b, h = idx // LANES, (idx % LANES) // SSM_H, idx % SSM_H
    perm = np.zeros((n * LANES, n * LANES), np.float32)
    perm[idx, b * LANES + a * SSM_H + h] = 1.0
    return jnp.asarray(perm, BF16)


def _unit_transpose(vs):
    vs = list(vs)
    n = len(vs)
    assert n * SSM_H == LANES
    unit = lax.broadcasted_iota(jnp.int32, (1, LANES), 1) // SSM_H
    d = n // 2
    while d:
        upper = (unit & d) != 0
        for i in range(n):
            if i & d:
                continue
            a, b = vs[i], vs[i + d]
            vs[i] = jnp.where(upper, pltpu.roll(b, d * SSM_H, 1), a)
            vs[i + d] = jnp.where(upper, b, pltpu.roll(a, LANES - d * SSM_H, 1))
        d //= 2
    return vs


def _s5_core_kernel(n_levels, h_ref, perm_ref, t_ref, bz_ref, cx_ref, dec_ref, y_ref, u_ref, yg_ref):
    n_grp = GROUPS_PER_SLAB
    first_half = lax.broadcasted_iota(jnp.int32, (1, LANES), 1) < SSM_P
    pairs = [(g0, g0 + 1) for g0 in range(0, n_grp, 2)]
    n_oct = CHUNK // n_grp

    def sequence(b):
        def slot(g):
            return b * n_grp + g

        def gather_rows(p, o):
            v = jnp.concatenate([h_ref[b, n_grp * o + l8] for l8 in range(n_grp)], axis=1)
            uv = _dot(v, perm_ref[:, 2 * LANES * p:2 * LANES * (p + 1)]).astype(BF16)
            for k, g in enumerate(pairs[p]):
                u_ref[slot(g), :, LANES * o:LANES * (o + 1)] = uv[:, LANES * k:LANES * (k + 1)]

        def pair_scan(p):
            g0, g1 = pairs[p]
            z = _dot(u_ref[slot(g0)], bz_ref[g0]) + _dot(u_ref[slot(g1)], bz_ref[g1])
            xs = []
            for direction in range(2):
                base = 2 * n_levels * direction
                a_re, a_im = ([jnp.where(first_half, dec_ref[g0, r:r + 1, :], dec_ref[g1, r:r + 1, :])
                               for r in range(lo, lo + n_levels)] for lo in (base, base + n_levels))
                cols = 2 * LANES * direction
                xs += yield from _chunk_scan(z[:, cols:cols + LANES], z[:, cols + LANES:cols + 2 * LANES],
                                             a_re, a_im, direction == 1)
            return jnp.concatenate(xs, axis=1).astype(BF16)

        lag = {}
        matmuls = []

        def lag_product(g):
            lag[g] = _dot(u_ref[slot(g)], t_ref[g])

        def state_product(g, x_all):
            yg_ref[slot(g)] = lag[g] + _dot(x_all, cx_ref[g])

        for o in range(n_oct):
            gather_rows(0, o)
            yield
        for p in range(len(pairs)):
            if p + 1 < len(pairs):
                matmuls += [functools.partial(gather_rows, p + 1, o) for o in range(n_oct)]
            matmuls += [functools.partial(lag_product, g) for g in pairs[p]]
            scan = pair_scan(p)
            while True:
                try:
                    next(scan)
                except StopIteration as done:
                    x_all = done.value
                    break
                if matmuls:
                    matmuls.pop(0)()
                yield
            while matmuls:
                matmuls.pop(0)()
                yield
            matmuls += [functools.partial(state_product, g, x_all) for g in pairs[p]]
        while matmuls:
            matmuls.pop(0)()
        for o in range(n_oct):
            ys = _unit_transpose([yg_ref[slot(g), :, LANES * o:LANES * (o + 1)] for g in range(n_grp)])
            for l8 in range(n_grp):
                y_ref[b, n_grp * o + l8] = ys[l8].astype(BF16)
            yield

    running = []
    for b in range(h_ref.shape[0]):
        running.append(sequence(b))
        lead = SEQUENCE_LEAD_STEPS if b + 1 < h_ref.shape[0] else 0
        for _ in range(lead):
            for seq in list(running):
                if next(seq, MATMUL_PHASE) is MATMUL_PHASE:
                    running.remove(seq)
    while running:
        for seq in list(running):
            if next(seq, MATMUL_PHASE) is MATMUL_PHASE:
                running.remove(seq)


def _s5_mix(h, lam_re, lam_im, log_dt, b_re, b_im, c_re, c_im):
    bsz, _, rows, d = h.shape
    w = CHUNK * SSM_H
    gb = GROUPS_PER_SLAB
    assert h.shape[1] == CHUNK and CHUNK % gb == 0 and d % LANES == 0
    n_levels = -(-max(1, (rows - 1).bit_length()) // SUBLANES) * SUBLANES
    t_mat, bz, cx, dec = _s5_prep(lam_re, lam_im, log_dt, b_re, b_im, c_re, c_im, n_levels)
    perm = _slab_permutation()
    slab = pl.BlockSpec((bsz, CHUNK, rows, LANES), lambda s: (0, 0, 0, s))
    grp = lambda r, c: pl.BlockSpec((gb, r, c), lambda s: (s, 0, 0))
    return pl.pallas_call(
        functools.partial(_s5_core_kernel, n_levels),
        grid=(d // LANES,),
        in_specs=[slab, _const_spec(perm.shape), grp(w, w), grp(w, 4 * LANES), grp(4 * LANES, w),
                  grp(dec.shape[1], LANES)],
        out_specs=slab,
        out_shape=jax.ShapeDtypeStruct(h.shape, BF16),
        scratch_shapes=[pltpu.VMEM((bsz * gb, rows, w), BF16), pltpu.VMEM((bsz * gb, rows, w), F32)],
        compiler_params=pltpu.CompilerParams(
            dimension_semantics=("arbitrary",), vmem_limit_bytes=VMEM_LIMIT),
        name="s5_core",
    )(h, perm, t_mat, bz, cx, dec)


def kernel(x, c, mod_w, mod_b, norm_g, ffn_w_in, ffn_w_out, pool_w, pool_b, pool_scale,
           ssm_lam_re, ssm_lam_im, ssm_log_dt, ssm_b_re, ssm_b_im, ssm_c_re, ssm_c_im,
           ssm_d, glu_w, glu_b, final_g):
    bsz, seq_len, d = x.shape
    depth = mod_w.shape[0]
    n_mixers = 2
    mods = _modulation(c, mod_w, mod_b)
    xf = x.astype(F32).reshape(bsz * seq_len, d)
    _, ffp, k_main = _ffn_dims(ffn_w_in)
    weights = (*_cast_first_weights(ffn_w_in, ffn_w_out, 0, 0), _w_out_tail(ffn_w_out, 0, 0, k_main, ffp))
    for i in range(depth):
        j = i // n_mixers
        is_s5 = i % n_mixers == 1
        last = i == depth - 1
        extras = (glu_w[j],) if is_s5 else ()
        acts, weights, extras = _layer_call(xf, mods[i], norm_g[i], weights, seq_len, emit_h=is_s5,
                                            next_ffn=(ffn_w_in, ffn_w_out, i, 1, extras))
        if is_s5:
            xf, h = acts
            ys = _s5_mix(h, ssm_lam_re[j], ssm_lam_im[j], ssm_log_dt[j], ssm_b_re[j],
                         ssm_b_im[j], ssm_c_re[j], ssm_c_im[j])
            mixer, mixer_args = "s5", (ys, ssm_d[j], extras[0], glu_b[j])
        else:
            xf = acts
            mixer, mixer_args = "pool", (pool_w[j], pool_b[j], pool_scale[j])
        res = _layer_call(xf, mods[i], norm_g[i], weights, seq_len, mixer=mixer, mixer_args=mixer_args,
                          final_g=final_g if last else None,
                          next_ffn=None if last else (ffn_w_in, ffn_w_out, i + 1, 0, ()))
        xf, weights = (res, None) if last else res[:2]
    return xf.reshape(bsz, seq_len, d).astype(x.dtype)
```
